```python
import jax
import jax.numpy as jnp
from jax import lax
import numpy as np

D_MODEL = 1024
BATCH = 32
SEQ = 2048
DEPTH = 1
DEC_BATCH = 8
DEC_SEQ = 16
PAST_LEN = 1024

CHUNK = 64
SB_HEADS = 8
SB_HEAD_DIM = 64
SB_WIDTH = SB_HEADS * SB_HEAD_DIM
SB_Q_BLOCK = 128
HG_HEADS = 4
HG_KEY_DIM = 128
HG_VAL_DIM = 128
HG_KEY_WIDTH = HG_HEADS * HG_KEY_DIM
HG_VAL_WIDTH = HG_HEADS * HG_VAL_DIM
HG_BLOCK = 16
NORM_EPS = 1e-6
IN_SIZES = (SB_WIDTH, SB_WIDTH, SB_WIDTH, SB_WIDTH,
            HG_KEY_WIDTH, HG_KEY_WIDTH, HG_VAL_WIDTH, HG_VAL_WIDTH,
            D_MODEL, D_MODEL)
IN_WIDTH = sum(IN_SIZES)

kernel_name = 'stickbreak_hgrn2_streaming_step'


def rms_norm(x, g):
    xf = x.astype(jnp.float32)
    y = xf * lax.rsqrt(jnp.mean(xf * xf, axis=-1, keepdims=True) + NORM_EPS)
    return (y * g.astype(jnp.float32)).astype(x.dtype)


def stick_breaking_block(q, k, v, q_pos, k_pos):
    z = jnp.einsum('bthd,bshd->bhts', q.astype(jnp.float32), k.astype(jnp.float32)) * (SB_HEAD_DIM ** -0.5)
    mask = k_pos[None, :] < q_pos[:, None]
    log_fail = jnp.where(mask, jax.nn.log_sigmoid(-z), 0.0)
    after = lax.cumsum(log_fail, axis=3, reverse=True) - log_fail
    a = jnp.where(mask, jnp.exp(jax.nn.log_sigmoid(z) + after), 0.0)
    return jnp.einsum('bhts,bshd->bthd', a, v.astype(jnp.float32))


def stick_breaking_prompt(q, k, v):
    B, T, H, Dh = q.shape
    nq = T // SB_Q_BLOCK
    qb = q.reshape(B, nq, SB_Q_BLOCK, H, Dh).transpose(1, 0, 2, 3, 4)
    k_pos = jnp.arange(T)
    starts = jnp.arange(nq) * SB_Q_BLOCK

    def one_block(args):
        qi, s0 = args
        return stick_breaking_block(qi, k, v, s0 + jnp.arange(SB_Q_BLOCK), k_pos)

    o = lax.map(one_block, (qb, starts))
    return o.transpose(1, 0, 2, 3, 4).reshape(B, T, H, Dh)


def stick_breaking_cached(q, k, v, cache_k, cache_v):
    T = q.shape[1]
    P = cache_k.shape[1]
    k_all = jnp.concatenate([cache_k.astype(k.dtype), k], axis=1)
    v_all = jnp.concatenate([cache_v.astype(v.dtype), v], axis=1)
    return stick_breaking_block(q, k_all, v_all, P + jnp.arange(T), jnp.arange(P + T))


def hgrn2_chunkwise(q, log_f, v, s0):
    B, T, H, _ = q.shape
    pad = (-T) % HG_BLOCK
    if pad:
        widths = ((0, 0), (0, pad), (0, 0), (0, 0))
        q, log_f, v = jnp.pad(q, widths), jnp.pad(log_f, widths), jnp.pad(v, widths)
    n = (T + pad) // HG_BLOCK

    def to_blocks(a):
        return a.reshape(B, n, HG_BLOCK, H, a.shape[-1]).transpose(1, 0, 2, 3, 4)

    tril = jnp.tril(jnp.ones((HG_BLOCK, HG_BLOCK), dtype=bool))

    def step(S, blk):
        qb, lfb, vb = blk
        b = jnp.cumsum(lfb, axis=1)
        kb = -jnp.expm1(lfb)
        b_end = b[:, -1:]
        q_dec = qb * jnp.exp(b)
        k_inv = kb * jnp.exp(-b)
        att = jnp.where(tril, jnp.einsum('bthc,bshc->bhts', q_dec, k_inv), 0.0)
        o = jnp.einsum('bhts,bshv->bthv', att, vb) + jnp.einsum('bthc,bhcv->bthv', q_dec, S)
        k_end = kb * jnp.exp(b_end - b)
        S_new = jnp.exp(b_end[:, 0])[..., None] * S + jnp.einsum('bshc,bshv->bhcv', k_end, vb)
        return S_new, o

    S_fin, o = lax.scan(step, s0, (to_blocks(q), to_blocks(log_f), to_blocks(v)))
    o = o.transpose(1, 0, 2, 3, 4).reshape(B, T + pad, H, HG_VAL_DIM)[:, :T]
    return o, S_fin


def trunk_layer(x, cache_k, cache_v, s0, g_pre, w_in, b_gate, lb, hg_norm_g,
                w_branch_a, w_branch_b, w_out, g_post):
    B, T, _ = x.shape
    xn = rms_norm(x, g_pre)
    h = xn @ w_in
    points = np.cumsum(IN_SIZES)[:-1].tolist()
    sb_q, sb_k, sb_v, sb_z, hg_q, hg_f, hg_i, hg_z, gate_a, gate_b = jnp.split(h, points, axis=-1)
    b_a, b_b = jnp.split(b_gate, 2, axis=-1)

    q = sb_q.reshape(B, T, SB_HEADS, SB_HEAD_DIM)
    k = sb_k.reshape(B, T, SB_HEADS, SB_HEAD_DIM)
    v = sb_v.reshape(B, T, SB_HEADS, SB_HEAD_DIM)
    if cache_k is None:
        o_sb = stick_breaking_prompt(q, k, v)
    else:
        o_sb = stick_breaking_cached(q, k, v, cache_k, cache_v)

    f = lb + (1.0 - lb) * jax.nn.sigmoid(hg_f.astype(jnp.float32))
    log_f = jnp.log(f).reshape(B, T, HG_HEADS, HG_KEY_DIM)
    s_init = jnp.zeros((B, HG_HEADS, HG_KEY_DIM, HG_VAL_DIM), jnp.float32) if s0 is None else s0.astype(jnp.float32)
    o_hg, S_new = hgrn2_chunkwise(hg_q.astype(jnp.float32).reshape(B, T, HG_HEADS, HG_KEY_DIM), log_f,
                                  hg_i.astype(jnp.float32).reshape(B, T, HG_HEADS, HG_VAL_DIM), s_init)
    o_hg = rms_norm(o_hg, hg_norm_g)

    u_a = (o_sb.reshape(B, T, SB_WIDTH) * jax.nn.silu(sb_z)) @ w_branch_a
    u_b = (o_hg.reshape(B, T, HG_VAL_WIDTH) * jax.nn.silu(hg_z)) @ w_branch_b
    merged = jax.nn.sigmoid(gate_a + b_a) * u_a + jax.nn.sigmoid(gate_b + b_b) * u_b
    y = (x.astype(jnp.float32) + rms_norm((merged @ w_out).astype(jnp.float32), g_post)).astype(x.dtype)
    s_dtype = x.dtype if s0 is None else s0.dtype
    return y, k, v, S_new.astype(s_dtype)


def setup_inputs(seed: int = 0) -> dict:
    key = jax.random.key(seed)
    ks = jax.random.split(key, 16)
    f32 = jnp.float32
    nrm = lambda k, shape: jax.random.normal(k, shape, f32)
    return {
        'x_prompt': nrm(ks[0], (BATCH, SEQ, D_MODEL)),
        'x_sample': nrm(ks[1], (DEC_BATCH, DEC_SEQ, D_MODEL)),
        'cache_sb_k': nrm(ks[2], (DEPTH, DEC_BATCH, PAST_LEN, SB_HEADS, SB_HEAD_DIM)),
        'cache_sb_v': nrm(ks[3], (DEPTH, DEC_BATCH, PAST_LEN, SB_HEADS, SB_HEAD_DIM)),
        'state_hgrn': 0.5 * nrm(ks[4], (DEPTH, DEC_BATCH, HG_HEADS, HG_KEY_DIM, HG_VAL_DIM)),
        'g_pre': 1.0 + 0.02 * nrm(ks[5], (DEPTH, D_MODEL)),
        'w_in': nrm(ks[6], (DEPTH, D_MODEL, IN_WIDTH)) * D_MODEL ** -0.5,
        'b_gate': 0.02 * nrm(ks[7], (DEPTH, 2 * D_MODEL)),
        'hg_lb_raw': 0.1 * nrm(ks[8], (DEPTH + 1, HG_KEY_WIDTH)),
        'hg_norm_g': 1.0 + 0.02 * nrm(ks[9], (DEPTH, HG_HEADS, HG_VAL_DIM)),
        'w_branch_a': nrm(ks[10], (DEPTH, SB_WIDTH, D_MODEL)) * SB_WIDTH ** -0.5,
        'w_branch_b': nrm(ks[11], (DEPTH, HG_VAL_WIDTH, D_MODEL)) * HG_VAL_WIDTH ** -0.5,
        'w_out': nrm(ks[12], (DEPTH, D_MODEL, D_MODEL)) * D_MODEL ** -0.5,
        'g_post': 1.0 + 0.02 * nrm(ks[13], (DEPTH, D_MODEL)),
    }


def reference(x_prompt, x_sample, cache_sb_k, cache_sb_v, state_hgrn, g_pre, w_in, b_gate,
              hg_lb_raw, hg_norm_g, w_branch_a, w_branch_b, w_out, g_post):
    lb_all = jnp.cumsum(jax.nn.softmax(hg_lb_raw.astype(jnp.float32), axis=0), axis=0)
    y_p, y_s = x_prompt, x_sample
    kp_l, vp_l, sp_l, ks_l, vs_l, ss_l = [], [], [], [], [], []
    for l in range(DEPTH):
        y_p, kp, vp, sp = trunk_layer(y_p, None, None, None, g_pre[l], w_in[l], b_gate[l], lb_all[l],
                                      hg_norm_g[l], w_branch_a[l], w_branch_b[l], w_out[l], g_post[l])
        y_s, ksm, vsm, ssm = trunk_layer(y_s, cache_sb_k[l], cache_sb_v[l], state_hgrn[l], g_pre[l], w_in[l],
                                         b_gate[l], lb_all[l], hg_norm_g[l], w_branch_a[l], w_branch_b[l],
                                         w_out[l], g_post[l])
        kp_l.append(kp); vp_l.append(vp); sp_l.append(sp)
        ks_l.append(ksm); vs_l.append(vsm); ss_l.append(ssm)
    new_k_prompt = jnp.stack(kp_l)
    new_v_prompt = jnp.stack(vp_l)
    new_state_prompt = jnp.stack(sp_l)
    new_k_sample = jnp.stack(ks_l)
    new_v_sample = jnp.stack(vs_l)
    new_state_sample = jnp.stack(ss_l)
    return (y_p, y_s, new_k_prompt, new_v_prompt, new_state_prompt, new_k_sample, new_v_sample, new_state_sample)
```

```python
import functools
import math

import numpy as np
import jax
import jax.numpy as jnp
from jax import lax
from jax.experimental import pallas as pl
from jax.experimental.pallas import tpu as pltpu

SB_HEADS = 8
SB_HEAD_DIM = 64
SB_WIDTH = SB_HEADS * SB_HEAD_DIM
HG_HEADS = 4
HG_KEY_DIM = 128
HG_VAL_DIM = 128
HG_WIDTH = HG_HEADS * HG_KEY_DIM
HG_BLOCK = 16
NORM_EPS = 1e-6

V7X_LANES = 128
V7X_VMEM_LIMIT = 56 * 1024 * 1024

F32 = jnp.float32
BF16 = jnp.bfloat16

_NT = (((1,), (1,)), ((), ()))
_TN = (((0,), (0,)), ((), ()))


def _dot(a, b):
    return jnp.dot(a, b, preferred_element_type=F32)


def _dot_nt(a, b):
    return lax.dot_general(a, b, _NT, preferred_element_type=F32)


def _dot_tn(a, b):
    return lax.dot_general(a, b, _TN, preferred_element_type=F32)


def _split_bf16(x):
    hi = x.astype(BF16)
    lo = (x - hi.astype(F32)).astype(BF16)
    return hi, lo


def _in_proj_kernel(x_ref, g_ref, w_ref, *out_refs, widths, scales):
    x = x_ref[...]
    ms = jnp.mean(x * x, axis=-1, keepdims=True)
    xn = (x * lax.rsqrt(ms + NORM_EPS) * g_ref[...]).astype(BF16)
    off = 0
    for ref, width, scale in zip(out_refs, widths, scales):
        h = _dot(xn, w_ref[:, off:off + width])
        if scale != 1.0:
            h = h * scale
        ref[...] = h.astype(ref.dtype)
        off += width


def _in_proj(x2d, g_pre, w_in_bf16, *, tm):
    n, d = x2d.shape
    d_model = d
    widths = (SB_WIDTH,) * 4 + (HG_WIDTH,) * 4 + (d_model, d_model)
    scales = (SB_HEAD_DIM ** -0.5,) + (1.0,) * 9
    dtypes = (BF16,) + (F32,) * 9
    assert sum(widths) == w_in_bf16.shape[1]
    assert n % tm == 0
    out_shape = tuple(jax.ShapeDtypeStruct((n, w), dt) for w, dt in zip(widths, dtypes))
    out_specs = tuple(pl.BlockSpec((tm, w), lambda i: (i, 0)) for w in widths)
    return pl.pallas_call(
        functools.partial(_in_proj_kernel, widths=widths, scales=scales),
        grid=(n // tm,),
        in_specs=[
            pl.BlockSpec((tm, d), lambda i: (i, 0)),
            pl.BlockSpec((1, d), lambda i: (0, 0)),
            pl.BlockSpec(w_in_bf16.shape, lambda i: (0, 0)),
        ],
        out_specs=out_specs,
        out_shape=out_shape,
        compiler_params=pltpu.CompilerParams(
            dimension_semantics=("arbitrary",), vmem_limit_bytes=V7X_VMEM_LIMIT),
        name="in_proj",
    )(x2d, g_pre.reshape(1, d), w_in_bf16)


def _suffix_ones(n):
    r = np.arange(n)
    return jnp.asarray((r[:, None] >= r[None, :]).astype(np.float32), dtype=BF16)


def _sb_chunk(qh, kc, vc, u, carry, acc, mask):
    z = _dot_nt(qh, kc)
    sp = jnp.maximum(z, 0.0) + jnp.log(1.0 + jnp.exp(-jnp.abs(z)))
    if mask is not None:
        sp = jnp.where(mask, sp, 0.0)
    hi, lo = _split_bf16(sp)
    tot = _dot(hi, u) + _dot(lo, u) + carry
    a = jnp.exp(z - tot)
    if mask is not None:
        a = jnp.where(mask, a, 0.0)
    acc = acc + _dot(a.astype(BF16), vc)
    return tot[:, 0:1], acc


def _head_lane_masks(tq):
    lane = lax.broadcasted_iota(jnp.int32, (tq, V7X_LANES), 1)
    return lane < SB_HEAD_DIM


def _sb_prompt_kernel(q_ref, k_ref, v_ref, u_ref, o_ref, kb_scr, vb_scr, *, tq):
    i = pl.program_id(2)

    @pl.when(i == 0)
    def _():
        kb_scr[...] = k_ref[0].astype(BF16)
        vb_scr[...] = v_ref[0].astype(BF16)

    q2 = q_ref[0]
    first = _head_lane_masks(tq)
    zero = jnp.zeros_like(q2)
    qh = (jnp.where(first, q2, zero), jnp.where(first, zero, q2))
    u = u_ref[...]

    r = lax.broadcasted_iota(jnp.int32, (tq, tq), 0)
    c = lax.broadcasted_iota(jnp.int32, (tq, tq), 1)
    causal = c < r

    start = pl.multiple_of(i * tq, tq)
    kd = kb_scr[pl.ds(start, tq), :]
    vd = vb_scr[pl.ds(start, tq), :]
    state = []
    for h in range(2):
        carry = jnp.zeros((tq, 1), F32)
        acc = jnp.zeros((tq, V7X_LANES), F32)
        state.extend(_sb_chunk(qh[h], kd, vd, u, carry, acc, causal))

    def body(n, st):
        s0 = pl.multiple_of((i - 1 - n) * tq, tq)
        kc = kb_scr[pl.ds(s0, tq), :]
        vc = vb_scr[pl.ds(s0, tq), :]
        c0, a0 = _sb_chunk(qh[0], kc, vc, u, st[0], st[1], None)
        c1, a1 = _sb_chunk(qh[1], kc, vc, u, st[2], st[3], None)
        return (c0, a0, c1, a1)

    st = lax.fori_loop(0, i, body, tuple(state))
    o_ref[0] = jnp.where(first, st[1], st[3])


def _sb_prompt(q, k, v, *, tq):
    b, t, w = q.shape
    assert w == SB_WIDTH and t % tq == 0
    pairs = SB_WIDTH // V7X_LANES
    return pl.pallas_call(
        functools.partial(_sb_prompt_kernel, tq=tq),
        grid=(b, pairs, t // tq),
        in_specs=[
            pl.BlockSpec((1, tq, V7X_LANES), lambda bi, p, i: (bi, i, p)),
            pl.BlockSpec((1, t, V7X_LANES), lambda bi, p, i: (bi, 0, p)),
            pl.BlockSpec((1, t, V7X_LANES), lambda bi, p, i: (bi, 0, p)),
            pl.BlockSpec((tq, tq), lambda bi, p, i: (0, 0)),
        ],
        out_specs=pl.BlockSpec((1, tq, V7X_LANES), lambda bi, p, i: (bi, i, p)),
        out_shape=jax.ShapeDtypeStruct((b, t, w), F32),
        scratch_shapes=[pltpu.VMEM((t, V7X_LANES), BF16), pltpu.VMEM((t, V7X_LANES), BF16)],
        compiler_params=pltpu.CompilerParams(
            dimension_semantics=("arbitrary", "arbitrary", "arbitrary"),
            vmem_limit_bytes=V7X_VMEM_LIMIT),
        name="sb_prompt",
    )(q, k, v, _suffix_ones(tq))


def _sb_cached_kernel(q_ref, k_ref, v_ref, ck_ref, cv_ref, ud_ref, uc_ref, o_ref, *, tq, kn, n_chunks):
    q2 = q_ref[0]
    first = _head_lane_masks(tq)
    zero = jnp.zeros_like(q2)
    qh = (jnp.where(first, q2, zero), jnp.where(first, zero, q2))

    r = lax.broadcasted_iota(jnp.int32, (tq, tq), 0)
    c = lax.broadcasted_iota(jnp.int32, (tq, tq), 1)
    causal = c < r
    kd = k_ref[0].astype(BF16)
    vd = v_ref[0].astype(BF16)
    ud = ud_ref[...]
    uc = uc_ref[...]
    state = []
    for h in range(2):
        carry = jnp.zeros((tq, 1), F32)
        acc = jnp.zeros((tq, V7X_LANES), F32)
        state.append(_sb_chunk(qh[h], kd, vd, ud, carry, acc, causal))
    for n in range(n_chunks - 1, -1, -1):
        kc = ck_ref[0, n * kn:(n + 1) * kn, :].astype(BF16)
        vc = cv_ref[0, n * kn:(n + 1) * kn, :].astype(BF16)
        state = [_sb_chunk(qh[h], kc, vc, uc, state[h][0], state[h][1], None) for h in range(2)]
    o_ref[0] = jnp.where(first, state[0][1], state[1][1])


def _sb_cached(q, k, v, cache_k, cache_v, *, kn):
    b, tq, w = q.shape
    past = cache_k.shape[1]
    assert w == SB_WIDTH and past % kn == 0
    pairs = SB_WIDTH // V7X_LANES
    new_spec = pl.BlockSpec((1, tq, V7X_LANES), lambda bi, p: (bi, 0, p))
    cache_spec = pl.BlockSpec((1, past, V7X_LANES), lambda bi, p: (bi, 0, p))
    return pl.pallas_call(
        functools.partial(_sb_cached_kernel, tq=tq, kn=kn, n_chunks=past // kn),
        grid=(b, pairs),
        in_specs=[new_spec, new_spec, new_spec, cache_spec, cache_spec,
                  pl.BlockSpec((tq, tq), lambda bi, p: (0, 0)),
                  pl.BlockSpec((kn, kn), lambda bi, p: (0, 0))],
        out_specs=new_spec,
        out_shape=jax.ShapeDtypeStruct((b, tq, w), F32),
        compiler_params=pltpu.CompilerParams(
            dimension_semantics=("arbitrary", "arbitrary"), vmem_limit_bytes=V7X_VMEM_LIMIT),
        name="sb_cached",
    )(q, k, v, cache_k, cache_v, _suffix_ones(tq), _suffix_ones(kn))


def _hgrn_consts(c):
    t = np.arange(c)
    sub = t // HG_BLOCK
    same = sub[:, None] == sub[None, :]
    tri = (same & (t[None, :] <= t[:, None])).astype(np.float32)
    ones = same.astype(np.float32)
    x = sub[:, None] ^ sub[None, :]
    level = np.where(x == 0, 0, np.floor(np.log2(np.maximum(x, 1))).astype(np.int64) + 1)
    level = np.where(t[None, :] <= t[:, None], level, -1).astype(np.int32)
    return jnp.asarray(tri, dtype=BF16), jnp.asarray(ones, dtype=BF16), jnp.asarray(level)


def _prod(vecs):
    out = None
    for v in vecs:
        out = v if out is None else out * v
    return out


def _rows(vecs, width):
    one = jnp.ones((HG_BLOCK, width), F32)
    return jnp.concatenate(
        [one if v is None else jnp.broadcast_to(v, (HG_BLOCK, width)) for v in vecs], axis=0)


def _hgrn_kernel(*refs, c, layer, has_s0, n_chunks):
    if has_s0:
        (hq_ref, hf_ref, hi_ref, lbraw_ref, g_ref, tri_ref, ones_ref, lvl_ref, s0_ref,
         o_ref, sout_ref, st_scr) = refs
    else:
        (hq_ref, hf_ref, hi_ref, lbraw_ref, g_ref, tri_ref, ones_ref, lvl_ref,
         o_ref, sout_ref, st_scr) = refs
        s0_ref = None
    ci = pl.program_id(1)
    width = HG_WIDTH
    nsub = c // HG_BLOCK
    levels = int(round(math.log2(nsub)))

    @pl.when(ci == 0)
    def _():
        for h in range(HG_HEADS):
            if has_s0:
                st_scr[h] = s0_ref[0, h].T
            else:
                st_scr[h] = jnp.zeros((HG_VAL_DIM, HG_KEY_DIM), F32)

    raw = lbraw_ref[...]
    e = jnp.exp(raw - jnp.max(raw, axis=0, keepdims=True))
    lb = jnp.sum(e[:layer + 1], axis=0, keepdims=True) / jnp.sum(e, axis=0, keepdims=True)

    f = lb + (1.0 - lb) * jax.nn.sigmoid(hf_ref[0])
    lf = jnp.log(f)
    kb = 1.0 - f
    hi, lo = _split_bf16(lf)
    tri = tri_ref[...]
    onesb = ones_ref[...]
    b16 = _dot(tri, hi) + _dot(tri, lo)
    bend = _dot(onesb, hi) + _dot(onesb, lo)
    q_dec = hq_ref[0] * jnp.exp(b16)
    k_inv = kb * jnp.exp(-b16)
    k_end = kb * jnp.exp(bend - b16)
    v_bf = hi_ref[0].astype(BF16)
    dmat = jnp.exp(bend)
    d = [dmat[HG_BLOCK * i:HG_BLOCK * i + 1, :] for i in range(nsub)]

    q_lv = [q_dec.astype(BF16)]
    k_lv = [k_inv.astype(BF16)]
    if levels >= 1:
        q_lv.append(q_lv[0])
        k_lv.append(k_end.astype(BF16))
    for l in range(2, levels + 1):
        half = 2 ** (l - 1)
        alpha, beta = [], []
        for i in range(nsub):
            mid = (i // (2 * half)) * 2 * half + half
            alpha.append(_prod(d[mid:i]) if i > mid else None)
            beta.append(_prod(d[i + 1:mid]) if i + 1 < mid else None)
        q_lv.append((q_dec * _rows(alpha, width)).astype(BF16))
        k_lv.append((k_end * _rows(beta, width)).astype(BF16))
    alpha_c = [_prod(d[:i]) if i > 0 else None for i in range(nsub)]
    beta_c = [_prod(d[i + 1:]) if i + 1 < nsub else None for i in range(nsub)]
    d_all = _prod(d)
    if nsub > 1:
        q_c = (q_dec * _rows(alpha_c, width)).astype(BF16)
        k_c = (k_end * _rows(beta_c, width)).astype(BF16)
    else:
        q_c = q_lv[0]
        k_c = k_end.astype(BF16)

    lvl = lvl_ref[...]
    g = g_ref[...]
    for h in range(HG_HEADS):
        sl = slice(HG_KEY_DIM * h, HG_KEY_DIM * (h + 1))
        att = jnp.zeros((c, c), F32)
        for l in range(levels + 1):
            att = jnp.where(lvl == l, _dot_nt(q_lv[l][:, sl], k_lv[l][:, sl]), att)
        st = st_scr[h]
        o = _dot(att.astype(BF16), v_bf[:, sl]) + _dot_nt(q_c[:, sl], st.astype(BF16))
        ms = jnp.mean(o * o, axis=-1, keepdims=True)
        o_ref[0, :, sl] = o * lax.rsqrt(ms + NORM_EPS) * g[:, sl]
        st_scr[h] = st * d_all[:, sl] + _dot_tn(v_bf[:, sl], k_c[:, sl])

    @pl.when(ci == n_chunks - 1)
    def _():
        for h in range(HG_HEADS):
            sout_ref[0, h] = st_scr[h].T


def _hgrn(hq, hf, hi, lb_raw, norm_g, s0, *, layer, c):
    b, t, w = hq.shape
    assert w == HG_WIDTH and t % c == 0 and c % HG_BLOCK == 0
    n_chunks = t // c
    tri, ones, level = _hgrn_consts(c)
    tok = pl.BlockSpec((1, c, w), lambda bi, ci: (bi, ci, 0))
    const2 = lambda shape: pl.BlockSpec(shape, lambda bi, ci: (0, 0))
    st_spec = pl.BlockSpec((1, HG_HEADS, HG_KEY_DIM, HG_VAL_DIM), lambda bi, ci: (bi, 0, 0, 0))
    in_specs = [tok, tok, tok, const2(lb_raw.shape), const2((1, w)),
                const2((c, c)), const2((c, c)), const2((c, c))]
    args = [hq, hf, hi, lb_raw, norm_g.reshape(1, w), tri, ones, level]
    if s0 is not None:
        in_specs.append(st_spec)
        args.append(s0)
    return pl.pallas_call(
        functools.partial(_hgrn_kernel, c=c, layer=layer, has_s0=s0 is not None, n_chunks=n_chunks),
        grid=(b, n_chunks),
        in_specs=in_specs,
        out_specs=(tok, st_spec),
        out_shape=(jax.ShapeDtypeStruct((b, t, w), F32),
                   jax.ShapeDtypeStruct((b, HG_HEADS, HG_KEY_DIM, HG_VAL_DIM), F32)),
        scratch_shapes=[pltpu.VMEM((HG_HEADS, HG_VAL_DIM, HG_KEY_DIM), F32)],
        compiler_params=pltpu.CompilerParams(
            dimension_semantics=("arbitrary", "arbitrary"), vmem_limit_bytes=V7X_VMEM_LIMIT),
        name="hgrn2",
    )(*args)


def _out_kernel(x_ref, osb_ref, za_ref, ohg_ref, hz_ref, ga_ref, gb_ref, bg_ref,
                wa_ref, wb_ref, wo_ref, gp_ref, y_ref):
    d_model = x_ref.shape[-1]
    u_a = _dot((osb_ref[...] * jax.nn.silu(za_ref[...])).astype(BF16), wa_ref[...])
    u_b = _dot((ohg_ref[...] * jax.nn.silu(hz_ref[...])).astype(BF16), wb_ref[...])
    bg = bg_ref[...]
    merged = (jax.nn.sigmoid(ga_ref[...] + bg[:, :d_model]) * u_a
              + jax.nn.sigmoid(gb_ref[...] + bg[:, d_model:]) * u_b)
    m = _dot(merged.astype(BF16), wo_ref[...])
    ms = jnp.mean(m * m, axis=-1, keepdims=True)
    y_ref[...] = x_ref[...] + m * lax.rsqrt(ms + NORM_EPS) * gp_ref[...]


def _out_stage(x2d, o_sb, z_a, o_hg, hz, ga, gb, b_gate, wa, wb, wo, g_post, *, tm):
    n, d = x2d.shape
    assert n % tm == 0
    row = lambda w: pl.BlockSpec((tm, w), lambda i: (i, 0))
    const = lambda shape: pl.BlockSpec(shape, lambda i: (0, 0))
    return pl.pallas_call(
        _out_kernel,
        grid=(n // tm,),
        in_specs=[row(d), row(SB_WIDTH), row(SB_WIDTH), row(HG_WIDTH), row(HG_WIDTH), row(d), row(d),
                  const((1, 2 * d)), const(wa.shape), const(wb.shape), const(wo.shape), const((1, d))],
        out_specs=row(d),
        out_shape=jax.ShapeDtypeStruct((n, d), F32),
        compiler_params=pltpu.CompilerParams(
            dimension_semantics=("arbitrary",), vmem_limit_bytes=V7X_VMEM_LIMIT),
        name="out_stage",
    )(x2d, o_sb, z_a, o_hg, hz, ga, gb, b_gate.reshape(1, 2 * d), wa, wb, wo, g_post.reshape(1, d))


def _largest_tile(n, cap):
    t = cap
    while n % t:
        t //= 2
    return t


def _trunk_layer(x, cache_k, cache_v, s0, layer, g_pre, w_in, b_gate, lb_raw, hg_norm_g,
                 w_a, w_b, w_out, g_post):
    b, t, d = x.shape
    n = b * t
    x2d = x.reshape(n, d)
    tm = _largest_tile(n, 256)
    q, k, v, z_a, hq, hf, hi, hz, ga, gb = _in_proj(x2d, g_pre, w_in, tm=tm)
    q3, k3, v3 = (a.reshape(b, t, SB_WIDTH) for a in (q, k, v))
    if cache_k is None:
        o_sb = _sb_prompt(q3, k3, v3, tq=_largest_tile(t, 256))
    else:
        past = cache_k.shape[1]
        o_sb = _sb_cached(q3, k3, v3, cache_k.reshape(b, past, SB_WIDTH), cache_v.reshape(b, past, SB_WIDTH),
                          kn=_largest_tile(past, 256))
    hq3, hf3, hi3 = (a.reshape(b, t, HG_WIDTH) for a in (hq, hf, hi))
    o_hg, s_new = _hgrn(hq3, hf3, hi3, lb_raw, hg_norm_g.reshape(-1), s0, layer=layer,
                        c=_largest_tile(t, 128))
    y = _out_stage(x2d, o_sb.reshape(n, SB_WIDTH), z_a, o_hg.reshape(n, HG_WIDTH), hz, ga, gb,
                   b_gate, w_a, w_b, w_out, g_post, tm=tm)
    return (y.reshape(b, t, d), k.reshape(b, t, SB_HEADS, SB_HEAD_DIM),
            v.reshape(b, t, SB_HEADS, SB_HEAD_DIM), s_new)


def kernel(x_prompt, x_sample, cache_sb_k, cache_sb_v, state_hgrn, g_pre, w_in, b_gate, hg_lb_raw,
           hg_norm_g, w_branch_a, w_branch_b, w_out, g_post):
    depth = w_in.shape[0]
    y_p, y_s = x_prompt, x_sample
    outs = [[] for _ in range(6)]
    for l in range(depth):
        weights = (g_pre[l], w_in[l].astype(BF16), b_gate[l], hg_lb_raw, hg_norm_g[l],
                   w_branch_a[l].astype(BF16), w_branch_b[l].astype(BF16), w_out[l].astype(BF16), g_post[l])
        y_p, kp, vp, sp = _trunk_layer(y_p, None, None, None, l, *weights)
        y_s, ks, vs, ss = _trunk_layer(y_s, cache_sb_k[l], cache_sb_v[l], state_hgrn[l], l, *weights)
        for lst, a in zip(outs, (kp, vp, sp, ks, vs, ss)):
            lst.append(a)
    return (y_p, y_s) + tuple(jnp.stack(lst) for lst in outs)
```

```python
import functools
import math

import numpy as np
import jax
import jax.numpy as jnp
from jax import lax
from jax.experimental import pallas as pl
from jax.experimental.pallas import tpu as pltpu

SB_HEADS = 8
SB_HEAD_DIM = 64
SB_WIDTH = SB_HEADS * SB_HEAD_DIM
HG_HEADS = 4
HG_KEY_DIM = 128
HG_VAL_DIM = 128
HG_WIDTH = HG_HEADS * HG_KEY_DIM
HG_BLOCK = 16
NORM_EPS = 1e-6

V7X_LANES = 128
V7X_VMEM_LIMIT = 56 * 1024 * 1024
SB_PAIRS = SB_WIDTH // V7X_LANES

SB_LOG2_UNDERFLOW = 150.5

F32 = jnp.float32
BF16 = jnp.bfloat16

_NT = (((1,), (1,)), ((), ()))
_TN = (((0,), (0,)), ((), ()))


def _dot(a, b):
    return jnp.dot(a, b, preferred_element_type=F32)


def _dot_nt(a, b):
    return lax.dot_general(a, b, _NT, preferred_element_type=F32)


def _dot_tn(a, b):
    return lax.dot_general(a, b, _TN, preferred_element_type=F32)


def _split_bf16(x):
    hi = x.astype(BF16)
    lo = (x - hi.astype(F32)).astype(BF16)
    return hi, lo


def _in_proj_kernel(*refs, widths, scales, kv_t):
    if kv_t:
        x_ref, g_ref, w_ref, wkv_t_ref = refs[:4]
        out_refs = refs[4:]
    else:
        x_ref, g_ref, w_ref = refs[:3]
        out_refs = refs[3:]
    x = x_ref[0]
    ms = jnp.mean(x * x, axis=-1, keepdims=True)
    xn = (x * lax.rsqrt(ms + NORM_EPS) * g_ref[...]).astype(BF16)
    if kv_t:
        kv = _dot_nt(wkv_t_ref[...], xn)
        out_refs[1][0] = kv[:SB_WIDTH]
        out_refs[2][0] = kv[SB_WIDTH:]
    off = 0
    for idx, (ref, width, scale) in enumerate(zip(out_refs, widths, scales)):
        if not (kv_t and idx in (1, 2)):
            h = _dot(xn, w_ref[:, off:off + width])
            if scale != 1.0:
                h = h * scale
            ref[0] = h.astype(ref.dtype)
        off += width


def _in_proj(x, g_pre, w_in_bf16, *, tm, kv_t):
    b, t, d = x.shape
    widths = (SB_WIDTH,) * 4 + (HG_WIDTH,) * 4 + (d, d)
    scales = (SB_HEAD_DIM ** -0.5 * math.log2(math.e),) + (1.0,) * 9
    dtypes = (BF16,) + (F32,) * 9
    assert sum(widths) == w_in_bf16.shape[1] and t % tm == 0
    out_shape, out_specs = [], []
    for idx, (w, dt) in enumerate(zip(widths, dtypes)):
        if kv_t and idx in (1, 2):
            out_shape.append(jax.ShapeDtypeStruct((b, w, t), dt))
            out_specs.append(pl.BlockSpec((1, w, tm), lambda bi, ti: (bi, 0, ti)))
        else:
            out_shape.append(jax.ShapeDtypeStruct((b, t, w), dt))
            out_specs.append(pl.BlockSpec((1, tm, w), lambda bi, ti: (bi, ti, 0)))
    in_specs = [
        pl.BlockSpec((1, tm, d), lambda bi, ti: (bi, ti, 0)),
        pl.BlockSpec((1, d), lambda bi, ti: (0, 0)),
        pl.BlockSpec(w_in_bf16.shape, lambda bi, ti: (0, 0)),
    ]
    args = [x, g_pre.reshape(1, d), w_in_bf16]
    if kv_t:
        wkv_t = w_in_bf16[:, SB_WIDTH:3 * SB_WIDTH].T
        in_specs.append(pl.BlockSpec(wkv_t.shape, lambda bi, ti: (0, 0)))
        args.append(wkv_t)
    return pl.pallas_call(
        functools.partial(_in_proj_kernel, widths=widths, scales=scales, kv_t=kv_t),
        grid=(b, t // tm),
        in_specs=in_specs,
        out_specs=tuple(out_specs),
        out_shape=tuple(out_shape),
        compiler_params=pltpu.CompilerParams(
            dimension_semantics=("arbitrary", "arbitrary"), vmem_limit_bytes=V7X_VMEM_LIMIT),
        name="in_proj",
    )(*args)


def _suffix_ones(n, copies):
    r = np.arange(n)
    u = (r[:, None] >= r[None, :]).astype(np.float32)
    return jnp.asarray(np.concatenate([u] * copies, axis=0), dtype=BF16)


def _sb_weights(z, u2, carry, mask):
    sp = jnp.maximum(z, 0.0) + jnp.log2(1.0 + jnp.exp2(-jnp.abs(z)))
    if mask is not None:
        sp = jnp.where(mask, sp, 0.0)
    hi, lo = _split_bf16(sp)
    tot = _dot(jnp.concatenate([hi, lo], axis=1), u2) + carry
    a = jnp.exp2(z - tot)
    if mask is not None:
        a = jnp.where(mask, a, 0.0)
    return a.astype(BF16), tot[:, 0:1]


def _head_split(x2, first):
    zero = jnp.zeros_like(x2)
    return jnp.where(first, x2, zero), jnp.where(first, zero, x2)


def _sb_prompt_kernel(q_ref, kt_ref, vt_ref, u2_ref, o_ref, kt_scr, ve_scr, vo_scr, acc_scr, car_scr, *, tq):
    i = pl.program_id(1)
    n_chunks = kt_scr.shape[0]

    @pl.when(i == 0)
    def _():
        row = lax.broadcasted_iota(jnp.int32, (SB_WIDTH, tq), 0)
        even = (row // SB_HEAD_DIM) % 2 == 0
        for n in range(n_chunks):
            kt_scr[n] = kt_ref[0, :, n * tq:(n + 1) * tq].astype(BF16)
            v = vt_ref[0, :, n * tq:(n + 1) * tq]
            ve_scr[n] = jnp.where(even, v, 0.0).astype(BF16)
            vo_scr[n] = jnp.where(even, 0.0, v).astype(BF16)

    first = lax.broadcasted_iota(jnp.int32, (tq, V7X_LANES), 1) < SB_HEAD_DIM
    r = lax.broadcasted_iota(jnp.int32, (tq, tq), 0)
    c = lax.broadcasted_iota(jnp.int32, (tq, tq), 1)
    causal = c < r
    u2 = u2_ref[...]

    def sweep(n, diag):
        lowest = None
        for p in range(SB_PAIRS):
            rows = slice(V7X_LANES * p, V7X_LANES * (p + 1))
            k2 = kt_scr[n, rows, :]
            vv = jnp.concatenate([ve_scr[n, rows, :], vo_scr[n, rows, :]], axis=1)
            parts = []
            for h, qm in enumerate(_head_split(q_ref[0, :, rows], first)):
                carry = jnp.zeros((tq, 1), F32) if diag else car_scr[2 * p + h]
                a, carry = _sb_weights(_dot(qm, k2), u2, carry, causal if diag else None)
                car_scr[2 * p + h] = carry
                parts.append(a)
                lowest = carry if lowest is None else jnp.minimum(lowest, carry)
            pv = _dot_nt(jnp.concatenate(parts, axis=1), vv)
            acc_scr[p] = pv if diag else acc_scr[p] + pv
        return jnp.min(lowest)

    def cond(st):
        n, lowest = st
        return jnp.logical_and(n >= 0, lowest <= SB_LOG2_UNDERFLOW)

    def body(st):
        n, _ = st
        return n - 1, sweep(n, False)

    lax.while_loop(cond, body, (i - 1, sweep(i, True)))
    for p in range(SB_PAIRS):
        o_ref[0, :, V7X_LANES * p:V7X_LANES * (p + 1)] = acc_scr[p]


def _sb_prompt(q, kt, vt, *, tq):
    b, t, w = q.shape
    assert w == SB_WIDTH and t % tq == 0 and kt.shape == (b, w, t)
    n_chunks = t // tq
    kv_spec = pl.BlockSpec((1, w, t), lambda bi, i: (bi, 0, 0))
    q_spec = pl.BlockSpec((1, tq, w), lambda bi, i: (bi, i, 0))
    return pl.pallas_call(
        functools.partial(_sb_prompt_kernel, tq=tq),
        grid=(b, n_chunks),
        in_specs=[q_spec, kv_spec, kv_spec, pl.BlockSpec((2 * tq, tq), lambda bi, i: (0, 0))],
        out_specs=q_spec,
        out_shape=jax.ShapeDtypeStruct((b, t, w), F32),
        scratch_shapes=[pltpu.VMEM((n_chunks, w, tq), BF16)] * 3
        + [pltpu.VMEM((SB_PAIRS, tq, V7X_LANES), F32), pltpu.VMEM((SB_HEADS, tq, 1), F32)],
        compiler_params=pltpu.CompilerParams(
            dimension_semantics=("arbitrary", "arbitrary"), vmem_limit_bytes=V7X_VMEM_LIMIT),
        name="sb_prompt",
    )(q, kt, vt, _suffix_ones(tq, 2))


def _sb_cached_kernel(q_ref, k_ref, v_ref, ckt_ref, cvt_ref, ud_ref, uc_ref, o_ref, *, tq, kn, n_chunks):
    first = lax.broadcasted_iota(jnp.int32, (tq, V7X_LANES), 1) < SB_HEAD_DIM
    r = lax.broadcasted_iota(jnp.int32, (tq, tq), 0)
    c = lax.broadcasted_iota(jnp.int32, (tq, tq), 1)
    causal = c < r
    even = (lax.broadcasted_iota(jnp.int32, (V7X_LANES, kn), 0) // SB_HEAD_DIM) % 2 == 0
    ud = ud_ref[...]
    uc = uc_ref[...]
    for p in range(SB_PAIRS):
        rows = slice(V7X_LANES * p, V7X_LANES * (p + 1))
        qs = _head_split(q_ref[0, :, rows], first)
        kd = k_ref[0, :, rows].astype(BF16)
        vs = _head_split(v_ref[0, :, rows], first)
        acc = jnp.zeros((tq, V7X_LANES), F32)
        carries = []
        for h in range(2):
            a, carry = _sb_weights(_dot_nt(qs[h], kd), ud, jnp.zeros((tq, 1), F32), causal)
            acc = acc + _dot(a, vs[h].astype(BF16))
            carries.append(carry)
        for n in range(n_chunks - 1, -1, -1):
            k2 = ckt_ref[0, rows, n * kn:(n + 1) * kn].astype(BF16)
            v = cvt_ref[0, rows, n * kn:(n + 1) * kn]
            vv = jnp.concatenate([jnp.where(even, v, 0.0), jnp.where(even, 0.0, v)], axis=1).astype(BF16)
            parts = []
            for h in range(2):
                a, carries[h] = _sb_weights(_dot(qs[h], k2), uc, carries[h], None)
                parts.append(a)
            acc = acc + _dot_nt(jnp.concatenate(parts, axis=1), vv)
        o_ref[0, :, rows] = acc


def _sb_cached(q, k, v, cache_kt, cache_vt, *, kn):
    b, tq, w = q.shape
    past = cache_kt.shape[2]
    assert w == SB_WIDTH and past % kn == 0 and cache_kt.shape == (b, w, past)
    new_spec = pl.BlockSpec((1, tq, w), lambda bi: (bi, 0, 0))
    cache_spec = pl.BlockSpec((1, w, past), lambda bi: (bi, 0, 0))
    return pl.pallas_call(
        functools.partial(_sb_cached_kernel, tq=tq, kn=kn, n_chunks=past // kn),
        grid=(b,),
        in_specs=[new_spec, new_spec, new_spec, cache_spec, cache_spec,
                  pl.BlockSpec((2 * tq, tq), lambda bi: (0, 0)),
                  pl.BlockSpec((2 * kn, kn), lambda bi: (0, 0))],
        out_specs=new_spec,
        out_shape=jax.ShapeDtypeStruct((b, tq, w), F32),
        compiler_params=pltpu.CompilerParams(
            dimension_semantics=("arbitrary",), vmem_limit_bytes=V7X_VMEM_LIMIT),
        name="sb_cached",
    )(q, k, v, cache_kt, cache_vt, _suffix_ones(tq, 2), _suffix_ones(kn, 2))


def _hgrn_consts(c):
    t = np.arange(c)
    sub = t // HG_BLOCK
    same = sub[:, None] == sub[None, :]
    tri = (same & (t[None, :] <= t[:, None])).astype(np.float32)
    ones = same.astype(np.float32)
    x = sub[:, None] ^ sub[None, :]
    level = np.where(x == 0, 0, np.floor(np.log2(np.maximum(x, 1))).astype(np.int64) + 1)
    level = np.where(t[None, :] <= t[:, None], level, -1).astype(np.int32)
    return jnp.asarray(tri, dtype=BF16), jnp.asarray(ones, dtype=BF16), jnp.asarray(level)


def _prod(vecs):
    out = None
    for v in vecs:
        out = v if out is None else out * v
    return out


def _rows(vecs, width):
    one = jnp.ones((HG_BLOCK, width), F32)
    return jnp.concatenate(
        [one if v is None else jnp.broadcast_to(v, (HG_BLOCK, width)) for v in vecs], axis=0)


def _hgrn_kernel(*refs, c, layer, has_s0, n_chunks):
    if has_s0:
        (hq_ref, hf_ref, hi_ref, lbraw_ref, g_ref, tri_ref, ones_ref, lvl_ref, s0_ref,
         o_ref, sout_ref, st_scr) = refs
    else:
        (hq_ref, hf_ref, hi_ref, lbraw_ref, g_ref, tri_ref, ones_ref, lvl_ref,
         o_ref, sout_ref, st_scr) = refs
        s0_ref = None
    ci = pl.program_id(1)
    width = HG_WIDTH
    nsub = c // HG_BLOCK
    levels = int(round(math.log2(nsub)))

    @pl.when(ci == 0)
    def _():
        for h in range(HG_HEADS):
            if has_s0:
                st_scr[h] = s0_ref[0, h].T
            else:
                st_scr[h] = jnp.zeros((HG_VAL_DIM, HG_KEY_DIM), F32)

    raw = lbraw_ref[...]
    e = jnp.exp(raw - jnp.max(raw, axis=0, keepdims=True))
    lb = jnp.sum(e[:layer + 1], axis=0, keepdims=True) / jnp.sum(e, axis=0, keepdims=True)

    f = lb + (1.0 - lb) * jax.nn.sigmoid(hf_ref[0])
    lf = jnp.log(f)
    kb = 1.0 - f
    hi, lo = _split_bf16(lf)
    tri = tri_ref[...]
    onesb = ones_ref[...]
    b16 = _dot(tri, hi) + _dot(tri, lo)
    bend = _dot(onesb, hi) + _dot(onesb, lo)
    q_dec = hq_ref[0] * jnp.exp(b16)
    k_inv = kb * jnp.exp(-b16)
    k_end = kb * jnp.exp(bend - b16)
    v_bf = hi_ref[0].astype(BF16)
    dmat = jnp.exp(bend)
    d = [dmat[HG_BLOCK * i:HG_BLOCK * i + 1, :] for i in range(nsub)]

    q_lv = [q_dec.astype(BF16)]
    k_lv = [k_inv.astype(BF16)]
    if levels >= 1:
        q_lv.append(q_lv[0])
        k_lv.append(k_end.astype(BF16))
    for l in range(2, levels + 1):
        half = 2 ** (l - 1)
        alpha, beta = [], []
        for i in range(nsub):
            mid = (i // (2 * half)) * 2 * half + half
            alpha.append(_prod(d[mid:i]) if i > mid else None)
            beta.append(_prod(d[i + 1:mid]) if i + 1 < mid else None)
        q_lv.append((q_dec * _rows(alpha, width)).astype(BF16))
        k_lv.append((k_end * _rows(beta, width)).astype(BF16))
    alpha_c = [_prod(d[:i]) if i > 0 else None for i in range(nsub)]
    beta_c = [_prod(d[i + 1:]) if i + 1 < nsub else None for i in range(nsub)]
    d_all = _prod(d)
    if nsub > 1:
        q_c = (q_dec * _rows(alpha_c, width)).astype(BF16)
        k_c = (k_end * _rows(beta_c, width)).astype(BF16)
    else:
        q_c = q_lv[0]
        k_c = k_end.astype(BF16)

    lvl = lvl_ref[...]
    g = g_ref[...]
    for h in range(HG_HEADS):
        sl = slice(HG_KEY_DIM * h, HG_KEY_DIM * (h + 1))
        att = jnp.zeros((c, c), F32)
        for l in range(levels + 1):
            att = jnp.where(lvl == l, _dot_nt(q_lv[l][:, sl], k_lv[l][:, sl]), att)
        st = st_scr[h]
        o = _dot(att.astype(BF16), v_bf[:, sl]) + _dot_nt(q_c[:, sl], st.astype(BF16))
        ms = jnp.mean(o * o, axis=-1, keepdims=True)
        o_ref[0, :, sl] = o * lax.rsqrt(ms + NORM_EPS) * g[:, sl]
        st_scr[h] = st * d_all[:, sl] + _dot_tn(v_bf[:, sl], k_c[:, sl])

    @pl.when(ci == n_chunks - 1)
    def _():
        for h in range(HG_HEADS):
            sout_ref[0, h] = st_scr[h].T


def _hgrn(hq, hf, hi, lb_raw, norm_g, s0, *, layer, c):
    b, t, w = hq.shape
    assert w == HG_WIDTH and t % c == 0 and c % HG_BLOCK == 0
    n_chunks = t // c
    tri, ones, level = _hgrn_consts(c)
    tok = pl.BlockSpec((1, c, w), lambda bi, ci: (bi, ci, 0))
    const2 = lambda shape: pl.BlockSpec(shape, lambda bi, ci: (0, 0))
    st_spec = pl.BlockSpec((1, HG_HEADS, HG_KEY_DIM, HG_VAL_DIM), lambda bi, ci: (bi, 0, 0, 0))
    in_specs = [tok, tok, tok, const2(lb_raw.shape), const2((1, w)),
                const2((c, c)), const2((c, c)), const2((c, c))]
    args = [hq, hf, hi, lb_raw, norm_g.reshape(1, w), tri, ones, level]
    if s0 is not None:
        in_specs.append(st_spec)
        args.append(s0)
    return pl.pallas_call(
        functools.partial(_hgrn_kernel, c=c, layer=layer, has_s0=s0 is not None, n_chunks=n_chunks),
        grid=(b, n_chunks),
        in_specs=in_specs,
        out_specs=(tok, st_spec),
        out_shape=(jax.ShapeDtypeStruct((b, t, w), F32),
                   jax.ShapeDtypeStruct((b, HG_HEADS, HG_KEY_DIM, HG_VAL_DIM), F32)),
        scratch_shapes=[pltpu.VMEM((HG_HEADS, HG_VAL_DIM, HG_KEY_DIM), F32)],
        compiler_params=pltpu.CompilerParams(
            dimension_semantics=("arbitrary", "arbitrary"), vmem_limit_bytes=V7X_VMEM_LIMIT),
        name="hgrn2",
    )(*args)


def _out_kernel(x_ref, osb_ref, za_ref, ohg_ref, hz_ref, ga_ref, gb_ref, bg_ref,
                wa_ref, wb_ref, wo_ref, gp_ref, y_ref):
    d_model = x_ref.shape[-1]
    u_a = _dot((osb_ref[...] * jax.nn.silu(za_ref[...])).astype(BF16), wa_ref[...])
    u_b = _dot((ohg_ref[...] * jax.nn.silu(hz_ref[...])).astype(BF16), wb_ref[...])
    bg = bg_ref[...]
    merged = (jax.nn.sigmoid(ga_ref[...] + bg[:, :d_model]) * u_a
              + jax.nn.sigmoid(gb_ref[...] + bg[:, d_model:]) * u_b)
    m = _dot(merged.astype(BF16), wo_ref[...])
    ms = jnp.mean(m * m, axis=-1, keepdims=True)
    y_ref[...] = x_ref[...] + m * lax.rsqrt(ms + NORM_EPS) * gp_ref[...]


def _out_stage(x2d, o_sb, z_a, o_hg, hz, ga, gb, b_gate, wa, wb, wo, g_post, *, tm):
    n, d = x2d.shape
    assert n % tm == 0
    row = lambda w: pl.BlockSpec((tm, w), lambda i: (i, 0))
    const = lambda shape: pl.BlockSpec(shape, lambda i: (0, 0))
    return pl.pallas_call(
        _out_kernel,
        grid=(n // tm,),
        in_specs=[row(d), row(SB_WIDTH), row(SB_WIDTH), row(HG_WIDTH), row(HG_WIDTH), row(d), row(d),
                  const((1, 2 * d)), const(wa.shape), const(wb.shape), const(wo.shape), const((1, d))],
        out_specs=row(d),
        out_shape=jax.ShapeDtypeStruct((n, d), F32),
        compiler_params=pltpu.CompilerParams(
            dimension_semantics=("arbitrary",), vmem_limit_bytes=V7X_VMEM_LIMIT),
        name="out_stage",
    )(x2d, o_sb, z_a, o_hg, hz, ga, gb, b_gate.reshape(1, 2 * d), wa, wb, wo, g_post.reshape(1, d))


def _largest_tile(n, cap):
    t = cap
    while n % t:
        t //= 2
    return t


def _trunk_layer(x, cache_k, cache_v, s0, layer, g_pre, w_in, b_gate, lb_raw, hg_norm_g,
                 w_a, w_b, w_out, g_post):
    b, t, d = x.shape
    n = b * t
    heads = (SB_HEADS, SB_HEAD_DIM)
    if cache_k is None:
        tm = _largest_tile(t, 256)
        q, kt, vt, z_a, hq, hf, hi, hz, ga, gb = _in_proj(x, g_pre, w_in, tm=tm, kv_t=True)
        o_sb = _sb_prompt(q, kt, vt, tq=tm)
        k_out, v_out = (a.reshape(b, *heads, t).transpose(0, 3, 1, 2) for a in (kt, vt))
    else:
        tm = _largest_tile(n, 256)
        outs = _in_proj(x.reshape(1, n, d), g_pre, w_in, tm=tm, kv_t=False)
        q, k, v, z_a, hq, hf, hi, hz, ga, gb = (a.reshape(b, t, a.shape[-1]) for a in outs)
        past = cache_k.shape[1]
        cache_kt, cache_vt = (a.transpose(0, 2, 3, 1).reshape(b, SB_WIDTH, past) for a in (cache_k, cache_v))
        o_sb = _sb_cached(q, k, v, cache_kt, cache_vt, kn=_largest_tile(past, 256))
        k_out, v_out = k.reshape(b, t, *heads), v.reshape(b, t, *heads)
    o_hg, s_new = _hgrn(hq, hf, hi, lb_raw, hg_norm_g.reshape(-1), s0, layer=layer, c=_largest_tile(t, 128))
    flat = lambda a: a.reshape(n, a.shape[-1])
    y = _out_stage(flat(x), flat(o_sb), flat(z_a), flat(o_hg), flat(hz), flat(ga), flat(gb),
                   b_gate, w_a, w_b, w_out, g_post, tm=_largest_tile(n, 256))
    return y.reshape(b, t, d), k_out, v_out, s_new


def kernel(x_prompt, x_sample, cache_sb_k, cache_sb_v, state_hgrn, g_pre, w_in, b_gate, hg_lb_raw,
           hg_norm_g, w_branch_a, w_branch_b, w_out, g_post):
    depth = w_in.shape[0]
    y_p, y_s = x_prompt, x_sample
    outs = [[] for _ in range(6)]
    for l in range(depth):
        weights = (g_pre[l], w_in[l].astype(BF16), b_gate[l], hg_lb_raw, hg_norm_g[l],
                   w_branch_a[l].astype(BF16), w_branch_b[l].astype(BF16), w_out[l].astype(BF16), g_post[l])
        y_p, kp, vp, sp = _trunk_layer(y_p, None, None, None, l, *weights)
        y_s, ks, vs, ss = _trunk_layer(y_s, cache_sb_k[l], cache_sb_v[l], state_hgrn[l], l, *weights)
        for lst, a in zip(outs, (kp, vp, sp, ks, vs, ss)):
            lst.append(a)
    return (y_p, y_s) + tuple(jnp.stack(lst) for lst in outs)
```

```python
import functools
import math

import numpy as np
import jax
import jax.numpy as jnp
from jax import lax
from jax.experimental import pallas as pl
from jax.experimental.pallas import tpu as pltpu

SB_HEADS = 8
SB_HEAD_DIM = 64
SB_WIDTH = SB_HEADS * SB_HEAD_DIM
HG_HEADS = 4
HG_KEY_DIM = 128
HG_VAL_DIM = 128
HG_WIDTH = HG_HEADS * HG_KEY_DIM
HG_BLOCK = 16
NORM_EPS = 1e-6

V7X_LANES = 128
V7X_VMEM_LIMIT = 56 * 1024 * 1024
SB_PAIRS = SB_WIDTH // V7X_LANES

SB_LOG2_UNDERFLOW = 150.5
LOG2_E = math.log2(math.e)

F32 = jnp.float32
BF16 = jnp.bfloat16

_NT = (((1,), (1,)), ((), ()))
_TN = (((0,), (0,)), ((), ()))


def _dot(a, b):
    return jnp.dot(a, b, preferred_element_type=F32)


def _dot_nt(a, b):
    return lax.dot_general(a, b, _NT, preferred_element_type=F32)


def _dot_tn(a, b):
    return lax.dot_general(a, b, _TN, preferred_element_type=F32)


def _split_bf16(x):
    hi = x.astype(BF16)
    lo = (x - hi.astype(F32)).astype(BF16)
    return hi, lo


IN_K, IN_V, IN_HG_V = 1, 2, 6


def _in_proj_kernel(*refs, widths, scales, t_groups):
    if t_groups:
        x_ref, g_ref, w_ref, wt_ref = refs[:4]
        out_refs = refs[4:]
    else:
        x_ref, g_ref, w_ref = refs[:3]
        out_refs = refs[3:]
    x = x_ref[0]
    ms = jnp.mean(x * x, axis=-1, keepdims=True)
    xn = (x * lax.rsqrt(ms + NORM_EPS) * g_ref[...]).astype(BF16)
    if t_groups:
        ht = _dot_nt(wt_ref[...], xn)
        off = 0
        for idx in t_groups:
            out_refs[idx][0] = ht[off:off + widths[idx]]
            off += widths[idx]
    off = 0
    for idx, (ref, width, scale) in enumerate(zip(out_refs, widths, scales)):
        if idx not in t_groups:
            h = _dot(xn, w_ref[:, off:off + width])
            if scale != 1.0:
                h = h * scale
            ref[0] = h.astype(ref.dtype)
        off += width


def _in_proj(x, g_pre, w_in_bf16, *, tm, t_groups):
    b, t, d = x.shape
    widths = (SB_WIDTH,) * 4 + (HG_WIDTH,) * 4 + (d, d)
    offsets = np.concatenate([[0], np.cumsum(widths)])
    scales = (SB_HEAD_DIM ** -0.5 * LOG2_E,) + (1.0,) * 9
    dtypes = (BF16,) + (F32,) * 9
    assert sum(widths) == w_in_bf16.shape[1] and t % tm == 0
    out_shape, out_specs = [], []
    for idx, (w, dt) in enumerate(zip(widths, dtypes)):
        if idx in t_groups:
            out_shape.append(jax.ShapeDtypeStruct((b, w, t), dt))
            out_specs.append(pl.BlockSpec((1, w, tm), lambda bi, ti: (bi, 0, ti)))
        else:
            out_shape.append(jax.ShapeDtypeStruct((b, t, w), dt))
            out_specs.append(pl.BlockSpec((1, tm, w), lambda bi, ti: (bi, ti, 0)))
    in_specs = [
        pl.BlockSpec((1, tm, d), lambda bi, ti: (bi, ti, 0)),
        pl.BlockSpec((1, d), lambda bi, ti: (0, 0)),
        pl.BlockSpec(w_in_bf16.shape, lambda bi, ti: (0, 0)),
    ]
    args = [x, g_pre.reshape(1, d), w_in_bf16]
    if t_groups:
        w_t = jnp.concatenate([w_in_bf16[:, offsets[i]:offsets[i + 1]] for i in t_groups], axis=1).T
        in_specs.append(pl.BlockSpec(w_t.shape, lambda bi, ti: (0, 0)))
        args.append(w_t)
    return pl.pallas_call(
        functools.partial(_in_proj_kernel, widths=widths, scales=scales, t_groups=t_groups),
        grid=(b, t // tm),
        in_specs=in_specs,
        out_specs=tuple(out_specs),
        out_shape=tuple(out_shape),
        compiler_params=pltpu.CompilerParams(
            dimension_semantics=("arbitrary", "arbitrary"), vmem_limit_bytes=V7X_VMEM_LIMIT),
        name="in_proj",
    )(*args)


def _suffix_ones(n, copies):
    r = np.arange(n)
    u = (r[:, None] >= r[None, :]).astype(np.float32)
    return jnp.asarray(np.concatenate([u] * copies, axis=0), dtype=BF16)


def _sb_weights(z, u2, carry, mask):
    sp = jnp.maximum(z, 0.0) + jnp.log(1.0 + jnp.exp2(-jnp.abs(z))) * LOG2_E
    if mask is not None:
        sp = jnp.where(mask, sp, 0.0)
    hi, lo = _split_bf16(sp)
    tot = _dot(jnp.concatenate([hi, lo], axis=1), u2) + carry
    a = jnp.exp2(z - tot)
    if mask is not None:
        a = jnp.where(mask, a, 0.0)
    return a.astype(BF16), tot[:, 0:1]


def _head_split(x2, first):
    zero = jnp.zeros_like(x2)
    return jnp.where(first, x2, zero), jnp.where(first, zero, x2)


def _sb_prompt_kernel(q_ref, kt_ref, vt_ref, u2_ref, o_ref, kt_scr, ve_scr, vo_scr, acc_scr, car_scr, *, tq):
    i = pl.program_id(1)
    n_chunks = kt_scr.shape[0]

    @pl.when(i == 0)
    def _():
        row = lax.broadcasted_iota(jnp.int32, (SB_WIDTH, tq), 0)
        even = (row // SB_HEAD_DIM) % 2 == 0
        for n in range(n_chunks):
            kt_scr[n] = kt_ref[0, :, n * tq:(n + 1) * tq].astype(BF16)
            v = vt_ref[0, :, n * tq:(n + 1) * tq]
            ve_scr[n] = jnp.where(even, v, 0.0).astype(BF16)
            vo_scr[n] = jnp.where(even, 0.0, v).astype(BF16)

    first = lax.broadcasted_iota(jnp.int32, (tq, V7X_LANES), 1) < SB_HEAD_DIM
    r = lax.broadcasted_iota(jnp.int32, (tq, tq), 0)
    c = lax.broadcasted_iota(jnp.int32, (tq, tq), 1)
    causal = c < r
    u2 = u2_ref[...]

    def sweep(chunks, diag):
        lowest = None
        for p in range(SB_PAIRS):
            rows = slice(V7X_LANES * p, V7X_LANES * (p + 1))
            qs = _head_split(q_ref[0, :, rows], first)
            carries = [jnp.zeros((tq, 1), F32) if diag else car_scr[2 * p + h] for h in range(2)]
            pv = None if diag else acc_scr[p]
            for j, n in enumerate(chunks):
                k2 = kt_scr[n, rows, :]
                vv = jnp.concatenate([ve_scr[n, rows, :], vo_scr[n, rows, :]], axis=1)
                parts = []
                for h in range(2):
                    a, carries[h] = _sb_weights(_dot(qs[h], k2), u2, carries[h],
                                                causal if diag and j == 0 else None)
                    parts.append(a)
                new = _dot_nt(jnp.concatenate(parts, axis=1), vv)
                pv = new if pv is None else pv + new
            acc_scr[p] = pv
            for h in range(2):
                car_scr[2 * p + h] = carries[h]
                lowest = carries[h] if lowest is None else jnp.minimum(lowest, carries[h])
        return jnp.min(lowest)

    def cond(st):
        n, lowest = st
        return jnp.logical_and(n >= 0, lowest <= SB_LOG2_UNDERFLOW)

    def body(st):
        n, _ = st
        return n - 1, sweep([n], False)

    lax.while_loop(cond, body, (i - 1, sweep([i], True)))
    for p in range(SB_PAIRS):
        o_ref[0, :, V7X_LANES * p:V7X_LANES * (p + 1)] = acc_scr[p]


def _sb_prompt(q, kt, vt, *, tq):
    b, t, w = q.shape
    assert w == SB_WIDTH and t % tq == 0 and kt.shape == (b, w, t)
    n_chunks = t // tq
    kv_spec = pl.BlockSpec((1, w, t), lambda bi, i: (bi, 0, 0))
    q_spec = pl.BlockSpec((1, tq, w), lambda bi, i: (bi, i, 0))
    return pl.pallas_call(
        functools.partial(_sb_prompt_kernel, tq=tq),
        grid=(b, n_chunks),
        in_specs=[q_spec, kv_spec, kv_spec, pl.BlockSpec((2 * tq, tq), lambda bi, i: (0, 0))],
        out_specs=q_spec,
        out_shape=jax.ShapeDtypeStruct((b, t, w), F32),
        scratch_shapes=[pltpu.VMEM((n_chunks, w, tq), BF16)] * 3
        + [pltpu.VMEM((SB_PAIRS, tq, V7X_LANES), F32), pltpu.VMEM((SB_HEADS, tq, 1), F32)],
        compiler_params=pltpu.CompilerParams(
            dimension_semantics=("arbitrary", "arbitrary"), vmem_limit_bytes=V7X_VMEM_LIMIT),
        name="sb_prompt",
    )(q, kt, vt, _suffix_ones(tq, 2))


def _sb_cached_kernel(q_ref, k_ref, v_ref, ckt_ref, cvt_ref, ud_ref, uc_ref, o_ref, *, tq, kn, n_chunks):
    first = lax.broadcasted_iota(jnp.int32, (tq, V7X_LANES), 1) < SB_HEAD_DIM
    r = lax.broadcasted_iota(jnp.int32, (tq, tq), 0)
    c = lax.broadcasted_iota(jnp.int32, (tq, tq), 1)
    causal = c < r
    even = (lax.broadcasted_iota(jnp.int32, (V7X_LANES, kn), 0) // SB_HEAD_DIM) % 2 == 0
    ud = ud_ref[...]
    uc = uc_ref[...]
    for p in range(SB_PAIRS):
        rows = slice(V7X_LANES * p, V7X_LANES * (p + 1))
        qs = _head_split(q_ref[0, :, rows], first)
        kd = k_ref[0, :, rows].astype(BF16)
        vs = _head_split(v_ref[0, :, rows], first)
        acc = jnp.zeros((tq, V7X_LANES), F32)
        carries = []
        for h in range(2):
            a, carry = _sb_weights(_dot_nt(qs[h], kd), ud, jnp.zeros((tq, 1), F32), causal)
            acc = acc + _dot(a, vs[h].astype(BF16))
            carries.append(carry)
        for n in range(n_chunks - 1, -1, -1):
            k2 = ckt_ref[0, rows, n * kn:(n + 1) * kn].astype(BF16)
            v = cvt_ref[0, rows, n * kn:(n + 1) * kn]
            vv = jnp.concatenate([jnp.where(even, v, 0.0), jnp.where(even, 0.0, v)], axis=1).astype(BF16)
            parts = []
            for h in range(2):
                a, carries[h] = _sb_weights(_dot(qs[h], k2), uc, carries[h], None)
                parts.append(a)
            acc = acc + _dot_nt(jnp.concatenate(parts, axis=1), vv)
        o_ref[0, :, rows] = acc


def _sb_cached(q, k, v, cache_kt, cache_vt, *, kn):
    b, tq, w = q.shape
    past = cache_kt.shape[2]
    assert w == SB_WIDTH and past % kn == 0 and cache_kt.shape == (b, w, past)
    new_spec = pl.BlockSpec((1, tq, w), lambda bi: (bi, 0, 0))
    cache_spec = pl.BlockSpec((1, w, past), lambda bi: (bi, 0, 0))
    return pl.pallas_call(
        functools.partial(_sb_cached_kernel, tq=tq, kn=kn, n_chunks=past // kn),
        grid=(b,),
        in_specs=[new_spec, new_spec, new_spec, cache_spec, cache_spec,
                  pl.BlockSpec((2 * tq, tq), lambda bi: (0, 0)),
                  pl.BlockSpec((2 * kn, kn), lambda bi: (0, 0))],
        out_specs=new_spec,
        out_shape=jax.ShapeDtypeStruct((b, tq, w), F32),
        compiler_params=pltpu.CompilerParams(
            dimension_semantics=("arbitrary",), vmem_limit_bytes=V7X_VMEM_LIMIT),
        name="sb_cached",
    )(q, k, v, cache_kt, cache_vt, _suffix_ones(tq, 2), _suffix_ones(kn, 2))


def _hgrn_consts(c):
    t = np.arange(c)
    sub = t // HG_BLOCK
    same = sub[:, None] == sub[None, :]
    tri = (same & (t[None, :] <= t[:, None])).astype(np.float32)
    ones = same.astype(np.float32)
    x = sub[:, None] ^ sub[None, :]
    level = np.where(x == 0, 0, np.floor(np.log2(np.maximum(x, 1))).astype(np.int64) + 1)
    level = np.where(t[None, :] <= t[:, None], level, -1).astype(np.int32)
    pre = np.block([[tri, tri], [ones, ones]])
    return jnp.asarray(pre, dtype=BF16), jnp.asarray(level)


def _prod(vecs):
    out = None
    for v in vecs:
        out = v if out is None else out * v
    return out


def _rows(vecs, width):
    one = jnp.ones((HG_BLOCK, width), F32)
    return jnp.concatenate(
        [one if v is None else jnp.broadcast_to(v, (HG_BLOCK, width)) for v in vecs], axis=0)


def _hgrn_chunk(hq, hf, v, v_t, lb, g, pre, lvl, states, c):
    width = HG_WIDTH
    nsub = c // HG_BLOCK
    levels = int(round(math.log2(nsub)))
    f = lb + (1.0 - lb) * jax.nn.sigmoid(hf)
    lf = jnp.log(f)
    kb = 1.0 - f
    sums = _dot(pre, jnp.concatenate(_split_bf16(lf), axis=0))
    b16 = sums[:c]
    bend = sums[c:]
    q_dec = hq * jnp.exp(b16)
    k_inv = kb * jnp.exp(-b16)
    k_end = kb * jnp.exp(bend - b16)
    v_bf = v.astype(BF16)
    dmat = jnp.exp(bend)
    d = [dmat[HG_BLOCK * i:HG_BLOCK * i + 1, :] for i in range(nsub)]

    q_lv = [q_dec.astype(BF16)]
    k_lv = [k_inv.astype(BF16)]
    if levels >= 1:
        q_lv.append(q_lv[0])
        k_lv.append(k_end.astype(BF16))
    for l in range(2, levels + 1):
        half = 2 ** (l - 1)
        alpha, beta = [], []
        for i in range(nsub):
            mid = (i // (2 * half)) * 2 * half + half
            alpha.append(_prod(d[mid:i]) if i > mid else None)
            beta.append(_prod(d[i + 1:mid]) if i + 1 < mid else None)
        q_lv.append((q_dec * _rows(alpha, width)).astype(BF16))
        k_lv.append((k_end * _rows(beta, width)).astype(BF16))
    alpha_c = [_prod(d[:i]) if i > 0 else None for i in range(nsub)]
    beta_c = [_prod(d[i + 1:]) if i + 1 < nsub else None for i in range(nsub)]
    d_all = _prod(d)
    if nsub > 1:
        q_c = (q_dec * _rows(alpha_c, width)).astype(BF16)
        k_c = (k_end * _rows(beta_c, width)).astype(BF16)
    else:
        q_c = q_lv[0]
        k_c = k_end.astype(BF16)

    outs, new_states = [], []
    for h in range(HG_HEADS):
        sl = slice(HG_KEY_DIM * h, HG_KEY_DIM * (h + 1))
        att = jnp.zeros((c, c), F32)
        for l in range(levels + 1):
            att = jnp.where(lvl == l, _dot_nt(q_lv[l][:, sl], k_lv[l][:, sl]), att)
        st = states[h]
        if v_t:
            vt_h = v_bf[sl, :]
            o_intra = _dot_nt(att.astype(BF16), vt_h)
            st_add = _dot(vt_h, k_c[:, sl])
        else:
            o_intra = _dot(att.astype(BF16), v_bf[:, sl])
            st_add = _dot_tn(v_bf[:, sl], k_c[:, sl])
        o = o_intra + _dot_nt(q_c[:, sl], st.astype(BF16))
        ms = jnp.mean(o * o, axis=-1, keepdims=True)
        outs.append(o * lax.rsqrt(ms + NORM_EPS) * g[:, sl])
        new_states.append(st * d_all[:, sl] + st_add)
    return outs, new_states


def _hgrn_kernel(*refs, c, group, layer, has_s0, v_t, n_steps):
    if has_s0:
        (hq_ref, hf_ref, hi_ref, lbraw_ref, g_ref, pre_ref, lvl_ref, s0_ref,
         o_ref, sout_ref, st_scr) = refs
    else:
        (hq_ref, hf_ref, hi_ref, lbraw_ref, g_ref, pre_ref, lvl_ref,
         o_ref, sout_ref, st_scr) = refs
        s0_ref = None
    ci = pl.program_id(1)

    @pl.when(ci == 0)
    def _():
        for h in range(HG_HEADS):
            if has_s0:
                st_scr[h] = s0_ref[0, h].T
            else:
                st_scr[h] = jnp.zeros((HG_VAL_DIM, HG_KEY_DIM), F32)

    raw = lbraw_ref[...]
    e = jnp.exp(raw - jnp.max(raw, axis=0, keepdims=True))
    lb = jnp.sum(e[:layer + 1], axis=0, keepdims=True) / jnp.sum(e, axis=0, keepdims=True)

    pre = pre_ref[...]
    lvl = lvl_ref[...]
    g = g_ref[...]
    states = [st_scr[h] for h in range(HG_HEADS)]
    for gi in range(group):
        rs = slice(gi * c, (gi + 1) * c)
        v = hi_ref[0, :, rs] if v_t else hi_ref[0, rs, :]
        outs, states = _hgrn_chunk(hq_ref[0, rs, :], hf_ref[0, rs, :], v, v_t, lb, g, pre, lvl, states, c)
        for h in range(HG_HEADS):
            o_ref[0, rs, HG_VAL_DIM * h:HG_VAL_DIM * (h + 1)] = outs[h]
    for h in range(HG_HEADS):
        st_scr[h] = states[h]

    @pl.when(ci == n_steps - 1)
    def _():
        for h in range(HG_HEADS):
            sout_ref[0, h] = st_scr[h].T


def _hgrn(hq, hf, hi, lb_raw, norm_g, s0, *, layer, c, group, v_t):
    b, t, w = hq.shape
    span = c * group
    assert w == HG_WIDTH and t % span == 0 and c % HG_BLOCK == 0
    n_steps = t // span
    pre, level = _hgrn_consts(c)
    tok = pl.BlockSpec((1, span, w), lambda bi, ci: (bi, ci, 0))
    tok_t = pl.BlockSpec((1, w, span), lambda bi, ci: (bi, 0, ci))
    const2 = lambda shape: pl.BlockSpec(shape, lambda bi, ci: (0, 0))
    st_spec = pl.BlockSpec((1, HG_HEADS, HG_KEY_DIM, HG_VAL_DIM), lambda bi, ci: (bi, 0, 0, 0))
    in_specs = [tok, tok, tok_t if v_t else tok, const2(lb_raw.shape), const2((1, w)),
                const2(pre.shape), const2(level.shape)]
    args = [hq, hf, hi, lb_raw, norm_g.reshape(1, w), pre, level]
    if s0 is not None:
        in_specs.append(st_spec)
        args.append(s0)
    return pl.pallas_call(
        functools.partial(_hgrn_kernel, c=c, group=group, layer=layer, has_s0=s0 is not None, v_t=v_t,
                          n_steps=n_steps),
        grid=(b, n_steps),
        in_specs=in_specs,
        out_specs=(tok, st_spec),
        out_shape=(jax.ShapeDtypeStruct((b, t, w), F32),
                   jax.ShapeDtypeStruct((b, HG_HEADS, HG_KEY_DIM, HG_VAL_DIM), F32)),
        scratch_shapes=[pltpu.VMEM((HG_HEADS, HG_VAL_DIM, HG_KEY_DIM), F32)],
        compiler_params=pltpu.CompilerParams(
            dimension_semantics=("arbitrary", "arbitrary"), vmem_limit_bytes=V7X_VMEM_LIMIT),
        name="hgrn2",
    )(*args)


def _out_kernel(x_ref, osb_ref, za_ref, ohg_ref, hz_ref, ga_ref, gb_ref, bg_ref,
                wa_ref, wb_ref, wo_ref, gp_ref, y_ref):
    d_model = x_ref.shape[-1]
    u_a = _dot((osb_ref[...] * jax.nn.silu(za_ref[...])).astype(BF16), wa_ref[...])
    u_b = _dot((ohg_ref[...] * jax.nn.silu(hz_ref[...])).astype(BF16), wb_ref[...])
    bg = bg_ref[...]
    merged = (jax.nn.sigmoid(ga_ref[...] + bg[:, :d_model]) * u_a
              + jax.nn.sigmoid(gb_ref[...] + bg[:, d_model:]) * u_b)
    m = _dot(merged.astype(BF16), wo_ref[...])
    ms = jnp.mean(m * m, axis=-1, keepdims=True)
    y_ref[...] = x_ref[...] + m * lax.rsqrt(ms + NORM_EPS) * gp_ref[...]


def _out_stage(x2d, o_sb, z_a, o_hg, hz, ga, gb, b_gate, wa, wb, wo, g_post, *, tm):
    n, d = x2d.shape
    assert n % tm == 0
    row = lambda w: pl.BlockSpec((tm, w), lambda i: (i, 0))
    const = lambda shape: pl.BlockSpec(shape, lambda i: (0, 0))
    return pl.pallas_call(
        _out_kernel,
        grid=(n // tm,),
        in_specs=[row(d), row(SB_WIDTH), row(SB_WIDTH), row(HG_WIDTH), row(HG_WIDTH), row(d), row(d),
                  const((1, 2 * d)), const(wa.shape), const(wb.shape), const(wo.shape), const((1, d))],
        out_specs=row(d),
        out_shape=jax.ShapeDtypeStruct((n, d), F32),
        compiler_params=pltpu.CompilerParams(
            dimension_semantics=("arbitrary",), vmem_limit_bytes=V7X_VMEM_LIMIT),
        name="out_stage",
    )(x2d, o_sb, z_a, o_hg, hz, ga, gb, b_gate.reshape(1, 2 * d), wa, wb, wo, g_post.reshape(1, d))


def _largest_tile(n, cap):
    t = cap
    while n % t:
        t //= 2
    return t


def _trunk_layer(x, cache_k, cache_v, s0, layer, g_pre, w_in, b_gate, lb_raw, hg_norm_g,
                 w_a, w_b, w_out, g_post):
    b, t, d = x.shape
    n = b * t
    heads = (SB_HEADS, SB_HEAD_DIM)
    if cache_k is None:
        tm = _largest_tile(t, 256)
        q, kt, vt, z_a, hq, hf, hi, hz, ga, gb = _in_proj(x, g_pre, w_in, tm=tm,
                                                          t_groups=(IN_K, IN_V, IN_HG_V))
        o_sb = _sb_prompt(q, kt, vt, tq=tm)
        k_out, v_out = (a.reshape(b, *heads, t).transpose(0, 3, 1, 2) for a in (kt, vt))
    else:
        tm = _largest_tile(n, 256)
        outs = _in_proj(x.reshape(1, n, d), g_pre, w_in, tm=tm, t_groups=())
        q, k, v, z_a, hq, hf, hi, hz, ga, gb = (a.reshape(b, t, a.shape[-1]) for a in outs)
        past = cache_k.shape[1]
        cache_kt, cache_vt = (a.transpose(0, 2, 3, 1).reshape(b, SB_WIDTH, past) for a in (cache_k, cache_v))
        o_sb = _sb_cached(q, k, v, cache_kt, cache_vt, kn=_largest_tile(past, 256))
        k_out, v_out = k.reshape(b, t, *heads), v.reshape(b, t, *heads)
    c = _largest_tile(t, 128)
    o_hg, s_new = _hgrn(hq, hf, hi, lb_raw, hg_norm_g.reshape(-1), s0, layer=layer, c=c,
                        group=_largest_tile(t // c, 4), v_t=cache_k is None)
    flat = lambda a: a.reshape(n, a.shape[-1])
    y = _out_stage(flat(x), flat(o_sb), flat(z_a), flat(o_hg), flat(hz), flat(ga), flat(gb),
                   b_gate, w_a, w_b, w_out, g_post, tm=_largest_tile(n, 256))
    return y.reshape(b, t, d), k_out, v_out, s_new


def kernel(x_prompt, x_sample, cache_sb_k, cache_sb_v, state_hgrn, g_pre, w_in, b_gate, hg_lb_raw,
           hg_norm_g, w_branch_a, w_branch_b, w_out, g_post):
    depth = w_in.shape[0]
    y_p, y_s = x_prompt, x_sample
    outs = [[] for _ in range(6)]
    for l in range(depth):
        weights = (g_pre[l], w_in[l].astype(BF16), b_gate[l], hg_lb_raw, hg_norm_g[l],
                   w_branch_a[l].astype(BF16), w_branch_b[l].astype(BF16), w_out[l].astype(BF16), g_post[l])
        y_p, kp, vp, sp = _trunk_layer(y_p, None, None, None, l, *weights)
        y_s, ks, vs, ss = _trunk_layer(y_s, cache_sb_k[l], cache_sb_v[l], state_hgrn[l], l, *weights)
        for lst, a in zip(outs, (kp, vp, sp, ks, vs, ss)):
            lst.append(a)
    return (y_p, y_s) + tuple(jnp.stack(lst) for lst in outs)
```

```python
import functools
import math

import numpy as np
import jax
import jax.numpy as jnp
from jax import lax
from jax.experimental import pallas as pl
from jax.experimental.pallas import tpu as pltpu

SB_HEADS = 8
SB_HEAD_DIM = 64
SB_WIDTH = SB_HEADS * SB_HEAD_DIM
HG_HEADS = 4
HG_KEY_DIM = 128
HG_VAL_DIM = 128
HG_WIDTH = HG_HEADS * HG_KEY_DIM
HG_BLOCK = 16
NORM_EPS = 1e-6

V7X_LANES = 128
V7X_VMEM_LIMIT = 56 * 1024 * 1024
SB_PAIRS = SB_WIDTH // V7X_LANES

SB_LOG2_UNDERFLOW = 150.5
LOG2_E = math.log2(math.e)

F32 = jnp.float32
BF16 = jnp.bfloat16

_NT = (((1,), (1,)), ((), ()))
_TN = (((0,), (0,)), ((), ()))


def _dot(a, b):
    return jnp.dot(a, b, preferred_element_type=F32)


def _dot_nt(a, b):
    return lax.dot_general(a, b, _NT, preferred_element_type=F32)


def _dot_tn(a, b):
    return lax.dot_general(a, b, _TN, preferred_element_type=F32)


def _split_bf16(x):
    hi = x.astype(BF16)
    lo = (x - hi.astype(F32)).astype(BF16)
    return hi, lo


IN_Q, IN_K, IN_V, IN_ZA, IN_HG_Q, IN_HG_F, IN_HG_V, IN_HG_Z, IN_GATE_A, IN_GATE_B = range(10)
IN_MIXER_GROUPS = (IN_Q, IN_K, IN_V, IN_HG_Q, IN_HG_F, IN_HG_V)
IN_GATE_GROUPS = (IN_ZA, IN_HG_Z, IN_GATE_A, IN_GATE_B)


def _in_widths(d_model):
    widths = (SB_WIDTH,) * 4 + (HG_WIDTH,) * 4 + (d_model, d_model)
    return widths, np.concatenate([[0], np.cumsum(widths)])


def _rms_norm_bf16(x, g):
    ms = jnp.mean(x * x, axis=-1, keepdims=True)
    return (x * lax.rsqrt(ms + NORM_EPS) * g).astype(BF16)


def _in_proj_kernel(*refs, groups, t_groups, widths, offsets, q_scale):
    if t_groups:
        x_ref, g_ref, w_ref, wt_ref = refs[:4]
        out_refs = refs[4:]
    else:
        x_ref, g_ref, w_ref = refs[:3]
        out_refs = refs[3:]
    xn = _rms_norm_bf16(x_ref[0], g_ref[...])
    if t_groups:
        ht = _dot_nt(wt_ref[...], xn)
        off = 0
        for idx in t_groups:
            out_refs[groups.index(idx)][0] = ht[off:off + widths[idx]]
            off += widths[idx]
    for ref, idx in zip(out_refs, groups):
        if idx not in t_groups:
            h = _dot(xn, w_ref[:, offsets[idx]:offsets[idx + 1]])
            if idx == IN_Q:
                h = h * q_scale
            ref[0] = h.astype(ref.dtype)


def _in_proj(x, g_pre, w_in_bf16, *, tm, t_groups):
    b, t, d = x.shape
    widths, offsets = _in_widths(d)
    offsets = tuple(int(o) for o in offsets)
    assert offsets[-1] == w_in_bf16.shape[1] and t % tm == 0
    out_shape, out_specs = [], []
    for idx in IN_MIXER_GROUPS:
        w = widths[idx]
        dt = BF16 if idx == IN_Q else F32
        if idx in t_groups:
            out_shape.append(jax.ShapeDtypeStruct((b, w, t), dt))
            out_specs.append(pl.BlockSpec((1, w, tm), lambda bi, ti: (bi, 0, ti)))
        else:
            out_shape.append(jax.ShapeDtypeStruct((b, t, w), dt))
            out_specs.append(pl.BlockSpec((1, tm, w), lambda bi, ti: (bi, ti, 0)))
    in_specs = [
        pl.BlockSpec((1, tm, d), lambda bi, ti: (bi, ti, 0)),
        pl.BlockSpec((1, d), lambda bi, ti: (0, 0)),
        pl.BlockSpec(w_in_bf16.shape, lambda bi, ti: (0, 0)),
    ]
    args = [x, g_pre.reshape(1, d), w_in_bf16]
    if t_groups:
        w_t = jnp.concatenate([w_in_bf16[:, offsets[i]:offsets[i + 1]] for i in t_groups], axis=1).T
        in_specs.append(pl.BlockSpec(w_t.shape, lambda bi, ti: (0, 0)))
        args.append(w_t)
    return pl.pallas_call(
        functools.partial(_in_proj_kernel, groups=IN_MIXER_GROUPS, t_groups=t_groups, widths=widths,
                          offsets=offsets, q_scale=SB_HEAD_DIM ** -0.5 * LOG2_E),
        grid=(b, t // tm),
        in_specs=in_specs,
        out_specs=tuple(out_specs),
        out_shape=tuple(out_shape),
        compiler_params=pltpu.CompilerParams(
            dimension_semantics=("arbitrary", "arbitrary"), vmem_limit_bytes=V7X_VMEM_LIMIT),
        name="in_proj",
    )(*args)


def _suffix_ones(n, copies):
    r = np.arange(n)
    u = (r[:, None] >= r[None, :]).astype(np.float32)
    return jnp.asarray(np.concatenate([u] * copies, axis=0), dtype=BF16)


def _sb_weights(z, u2, carry, mask):
    sp = jnp.maximum(z, 0.0) + jnp.log(1.0 + jnp.exp2(-jnp.abs(z))) * LOG2_E
    if mask is not None:
        sp = jnp.where(mask, sp, 0.0)
    hi, lo = _split_bf16(sp)
    tot = _dot(jnp.concatenate([hi, lo], axis=1), u2) + carry
    a = jnp.exp2(z - tot)
    if mask is not None:
        a = jnp.where(mask, a, 0.0)
    return a.astype(BF16), tot[:, 0:1]


def _head_split(x2, first):
    zero = jnp.zeros_like(x2)
    return jnp.where(first, x2, zero), jnp.where(first, zero, x2)


def _sb_prompt_kernel(q_ref, kt_ref, vt_ref, u2_ref, o_ref, kt_scr, ve_scr, vo_scr, acc_scr, car_scr, *, tq):
    i = pl.program_id(1)
    n_chunks = kt_scr.shape[0]

    @pl.when(i == 0)
    def _():
        row = lax.broadcasted_iota(jnp.int32, (SB_WIDTH, tq), 0)
        even = (row // SB_HEAD_DIM) % 2 == 0
        for n in range(n_chunks):
            kt_scr[n] = kt_ref[0, :, n * tq:(n + 1) * tq].astype(BF16)
            v = vt_ref[0, :, n * tq:(n + 1) * tq]
            ve_scr[n] = jnp.where(even, v, 0.0).astype(BF16)
            vo_scr[n] = jnp.where(even, 0.0, v).astype(BF16)

    first = lax.broadcasted_iota(jnp.int32, (tq, V7X_LANES), 1) < SB_HEAD_DIM
    r = lax.broadcasted_iota(jnp.int32, (tq, tq), 0)
    c = lax.broadcasted_iota(jnp.int32, (tq, tq), 1)
    causal = c < r
    u2 = u2_ref[...]

    def sweep(chunks, diag):
        lowest = None
        for p in range(SB_PAIRS):
            rows = slice(V7X_LANES * p, V7X_LANES * (p + 1))
            qs = _head_split(q_ref[0, :, rows], first)
            carries = [jnp.zeros((tq, 1), F32) if diag else car_scr[2 * p + h] for h in range(2)]
            pv = None if diag else acc_scr[p]
            for j, n in enumerate(chunks):
                k2 = kt_scr[n, rows, :]
                vv = jnp.concatenate([ve_scr[n, rows, :], vo_scr[n, rows, :]], axis=1)
                parts = []
                for h in range(2):
                    a, carries[h] = _sb_weights(_dot(qs[h], k2), u2, carries[h],
                                                causal if diag and j == 0 else None)
                    parts.append(a)
                new = _dot_nt(jnp.concatenate(parts, axis=1), vv)
                pv = new if pv is None else pv + new
            acc_scr[p] = pv
            for h in range(2):
                car_scr[2 * p + h] = carries[h]
                lowest = carries[h] if lowest is None else jnp.minimum(lowest, carries[h])
        return jnp.min(lowest)

    def cond(st):
        n, lowest = st
        return jnp.logical_and(n >= 0, lowest <= SB_LOG2_UNDERFLOW)

    def body(st):
        n, _ = st
        return n - 1, sweep([n], False)

    lax.while_loop(cond, body, (i - 1, sweep([i], True)))
    for p in range(SB_PAIRS):
        o_ref[0, :, V7X_LANES * p:V7X_LANES * (p + 1)] = acc_scr[p]


def _sb_prompt(q, kt, vt, *, tq):
    b, t, w = q.shape
    assert w == SB_WIDTH and t % tq == 0 and kt.shape == (b, w, t)
    n_chunks = t // tq
    kv_spec = pl.BlockSpec((1, w, t), lambda bi, i: (bi, 0, 0))
    q_spec = pl.BlockSpec((1, tq, w), lambda bi, i: (bi, i, 0))
    return pl.pallas_call(
        functools.partial(_sb_prompt_kernel, tq=tq),
        grid=(b, n_chunks),
        in_specs=[q_spec, kv_spec, kv_spec, pl.BlockSpec((2 * tq, tq), lambda bi, i: (0, 0))],
        out_specs=q_spec,
        out_shape=jax.ShapeDtypeStruct((b, t, w), F32),
        scratch_shapes=[pltpu.VMEM((n_chunks, w, tq), BF16)] * 3
        + [pltpu.VMEM((SB_PAIRS, tq, V7X_LANES), F32), pltpu.VMEM((SB_HEADS, tq, 1), F32)],
        compiler_params=pltpu.CompilerParams(
            dimension_semantics=("arbitrary", "arbitrary"), vmem_limit_bytes=V7X_VMEM_LIMIT),
        name="sb_prompt",
    )(q, kt, vt, _suffix_ones(tq, 2))


def _sb_cached_kernel(q_ref, k_ref, v_ref, ckt_ref, cvt_ref, ud_ref, uc_ref, o_ref, *, tq, kn, n_chunks):
    first = lax.broadcasted_iota(jnp.int32, (tq, V7X_LANES), 1) < SB_HEAD_DIM
    r = lax.broadcasted_iota(jnp.int32, (tq, tq), 0)
    c = lax.broadcasted_iota(jnp.int32, (tq, tq), 1)
    causal = c < r
    even = (lax.broadcasted_iota(jnp.int32, (V7X_LANES, kn), 0) // SB_HEAD_DIM) % 2 == 0
    ud = ud_ref[...]
    uc = uc_ref[...]
    for p in range(SB_PAIRS):
        rows = slice(V7X_LANES * p, V7X_LANES * (p + 1))
        qs = _head_split(q_ref[0, :, rows], first)
        kd = k_ref[0, :, rows].astype(BF16)
        vs = _head_split(v_ref[0, :, rows], first)
        acc = jnp.zeros((tq, V7X_LANES), F32)
        carries = []
        for h in range(2):
            a, carry = _sb_weights(_dot_nt(qs[h], kd), ud, jnp.zeros((tq, 1), F32), causal)
            acc = acc + _dot(a, vs[h].astype(BF16))
            carries.append(carry)
        for n in range(n_chunks - 1, -1, -1):
            k2 = ckt_ref[0, rows, n * kn:(n + 1) * kn].astype(BF16)
            v = cvt_ref[0, rows, n * kn:(n + 1) * kn]
            vv = jnp.concatenate([jnp.where(even, v, 0.0), jnp.where(even, 0.0, v)], axis=1).astype(BF16)
            parts = []
            for h in range(2):
                a, carries[h] = _sb_weights(_dot(qs[h], k2), uc, carries[h], None)
                parts.append(a)
            acc = acc + _dot_nt(jnp.concatenate(parts, axis=1), vv)
        o_ref[0, :, rows] = acc


def _sb_cached(q, k, v, cache_kt, cache_vt, *, kn):
    b, tq, w = q.shape
    past = cache_kt.shape[2]
    assert w == SB_WIDTH and past % kn == 0 and cache_kt.shape == (b, w, past)
    new_spec = pl.BlockSpec((1, tq, w), lambda bi: (bi, 0, 0))
    cache_spec = pl.BlockSpec((1, w, past), lambda bi: (bi, 0, 0))
    return pl.pallas_call(
        functools.partial(_sb_cached_kernel, tq=tq, kn=kn, n_chunks=past // kn),
        grid=(b,),
        in_specs=[new_spec, new_spec, new_spec, cache_spec, cache_spec,
                  pl.BlockSpec((2 * tq, tq), lambda bi: (0, 0)),
                  pl.BlockSpec((2 * kn, kn), lambda bi: (0, 0))],
        out_specs=new_spec,
        out_shape=jax.ShapeDtypeStruct((b, tq, w), F32),
        compiler_params=pltpu.CompilerParams(
            dimension_semantics=("arbitrary",), vmem_limit_bytes=V7X_VMEM_LIMIT),
        name="sb_cached",
    )(q, k, v, cache_kt, cache_vt, _suffix_ones(tq, 2), _suffix_ones(kn, 2))


def _hgrn_consts(c):
    t = np.arange(c)
    sub = t // HG_BLOCK
    same = sub[:, None] == sub[None, :]
    tri = (same & (t[None, :] <= t[:, None])).astype(np.float32)
    ones = same.astype(np.float32)
    x = sub[:, None] ^ sub[None, :]
    level = np.where(x == 0, 0, np.floor(np.log2(np.maximum(x, 1))).astype(np.int64) + 1)
    level = np.where(t[None, :] <= t[:, None], level, -1).astype(np.int32)
    pre = np.block([[tri, tri], [ones, ones]])
    return jnp.asarray(pre, dtype=BF16), jnp.asarray(level)


def _prod(vecs):
    out = None
    for v in vecs:
        out = v if out is None else out * v
    return out


def _rows(vecs, width):
    one = jnp.ones((HG_BLOCK, width), F32)
    return jnp.concatenate(
        [one if v is None else jnp.broadcast_to(v, (HG_BLOCK, width)) for v in vecs], axis=0)


def _hgrn_chunk(hq, hf, v, v_t, lb, g, pre, lvl, states, c):
    width = HG_WIDTH
    nsub = c // HG_BLOCK
    levels = int(round(math.log2(nsub)))
    f = lb + (1.0 - lb) * jax.nn.sigmoid(hf)
    lf = jnp.log(f)
    kb = 1.0 - f
    sums = _dot(pre, jnp.concatenate(_split_bf16(lf), axis=0))
    b16 = sums[:c]
    bend = sums[c:]
    q_dec = hq * jnp.exp(b16)
    k_inv = kb * jnp.exp(-b16)
    k_end = kb * jnp.exp(bend - b16)
    v_bf = v.astype(BF16)
    dmat = jnp.exp(bend)
    d = [dmat[HG_BLOCK * i:HG_BLOCK * i + 1, :] for i in range(nsub)]

    q_lv = [q_dec.astype(BF16)]
    k_lv = [k_inv.astype(BF16)]
    if levels >= 1:
        q_lv.append(q_lv[0])
        k_lv.append(k_end.astype(BF16))
    for l in range(2, levels + 1):
        half = 2 ** (l - 1)
        alpha, beta = [], []
        for i in range(nsub):
            mid = (i // (2 * half)) * 2 * half + half
            alpha.append(_prod(d[mid:i]) if i > mid else None)
            beta.append(_prod(d[i + 1:mid]) if i + 1 < mid else None)
        q_lv.append((q_dec * _rows(alpha, width)).astype(BF16))
        k_lv.append((k_end * _rows(beta, width)).astype(BF16))
    alpha_c = [_prod(d[:i]) if i > 0 else None for i in range(nsub)]
    beta_c = [_prod(d[i + 1:]) if i + 1 < nsub else None for i in range(nsub)]
    d_all = _prod(d)
    if nsub > 1:
        q_c = (q_dec * _rows(alpha_c, width)).astype(BF16)
        k_c = (k_end * _rows(beta_c, width)).astype(BF16)
    else:
        q_c = q_lv[0]
        k_c = k_end.astype(BF16)

    outs, new_states = [], []
    for h in range(HG_HEADS):
        sl = slice(HG_KEY_DIM * h, HG_KEY_DIM * (h + 1))
        att = jnp.zeros((c, c), F32)
        for l in range(levels + 1):
            att = jnp.where(lvl == l, _dot_nt(q_lv[l][:, sl], k_lv[l][:, sl]), att)
        st = states[h]
        if v_t:
            vt_h = v_bf[sl, :]
            o_intra = _dot_nt(att.astype(BF16), vt_h)
            st_add = _dot(vt_h, k_c[:, sl])
        else:
            o_intra = _dot(att.astype(BF16), v_bf[:, sl])
            st_add = _dot_tn(v_bf[:, sl], k_c[:, sl])
        o = o_intra + _dot_nt(q_c[:, sl], st.astype(BF16))
        ms = jnp.mean(o * o, axis=-1, keepdims=True)
        outs.append(o * lax.rsqrt(ms + NORM_EPS) * g[:, sl])
        new_states.append(st * d_all[:, sl] + st_add)
    return outs, new_states


def _hgrn_kernel(*refs, c, group, layer, has_s0, v_t, n_steps):
    if has_s0:
        (hq_ref, hf_ref, hi_ref, lbraw_ref, g_ref, pre_ref, lvl_ref, s0_ref,
         o_ref, sout_ref, st_scr) = refs
    else:
        (hq_ref, hf_ref, hi_ref, lbraw_ref, g_ref, pre_ref, lvl_ref,
         o_ref, sout_ref, st_scr) = refs
        s0_ref = None
    ci = pl.program_id(1)

    @pl.when(ci == 0)
    def _():
        for h in range(HG_HEADS):
            if has_s0:
                st_scr[h] = s0_ref[0, h].T
            else:
                st_scr[h] = jnp.zeros((HG_VAL_DIM, HG_KEY_DIM), F32)

    raw = lbraw_ref[...]
    e = jnp.exp(raw - jnp.max(raw, axis=0, keepdims=True))
    lb = jnp.sum(e[:layer + 1], axis=0, keepdims=True) / jnp.sum(e, axis=0, keepdims=True)

    pre = pre_ref[...]
    lvl = lvl_ref[...]
    g = g_ref[...]
    states = [st_scr[h] for h in range(HG_HEADS)]
    for gi in range(group):
        rs = slice(gi * c, (gi + 1) * c)
        v = hi_ref[0, :, rs] if v_t else hi_ref[0, rs, :]
        outs, states = _hgrn_chunk(hq_ref[0, rs, :], hf_ref[0, rs, :], v, v_t, lb, g, pre, lvl, states, c)
        for h in range(HG_HEADS):
            o_ref[0, rs, HG_VAL_DIM * h:HG_VAL_DIM * (h + 1)] = outs[h]
    for h in range(HG_HEADS):
        st_scr[h] = states[h]

    @pl.when(ci == n_steps - 1)
    def _():
        for h in range(HG_HEADS):
            sout_ref[0, h] = st_scr[h].T


def _hgrn(hq, hf, hi, lb_raw, norm_g, s0, *, layer, c, group, v_t):
    b, t, w = hq.shape
    span = c * group
    assert w == HG_WIDTH and t % span == 0 and c % HG_BLOCK == 0
    n_steps = t // span
    pre, level = _hgrn_consts(c)
    tok = pl.BlockSpec((1, span, w), lambda bi, ci: (bi, ci, 0))
    tok_t = pl.BlockSpec((1, w, span), lambda bi, ci: (bi, 0, ci))
    const2 = lambda shape: pl.BlockSpec(shape, lambda bi, ci: (0, 0))
    st_spec = pl.BlockSpec((1, HG_HEADS, HG_KEY_DIM, HG_VAL_DIM), lambda bi, ci: (bi, 0, 0, 0))
    in_specs = [tok, tok, tok_t if v_t else tok, const2(lb_raw.shape), const2((1, w)),
                const2(pre.shape), const2(level.shape)]
    args = [hq, hf, hi, lb_raw, norm_g.reshape(1, w), pre, level]
    if s0 is not None:
        in_specs.append(st_spec)
        args.append(s0)
    return pl.pallas_call(
        functools.partial(_hgrn_kernel, c=c, group=group, layer=layer, has_s0=s0 is not None, v_t=v_t,
                          n_steps=n_steps),
        grid=(b, n_steps),
        in_specs=in_specs,
        out_specs=(tok, st_spec),
        out_shape=(jax.ShapeDtypeStruct((b, t, w), F32),
                   jax.ShapeDtypeStruct((b, HG_HEADS, HG_KEY_DIM, HG_VAL_DIM), F32)),
        scratch_shapes=[pltpu.VMEM((HG_HEADS, HG_VAL_DIM, HG_KEY_DIM), F32)],
        compiler_params=pltpu.CompilerParams(
            dimension_semantics=("arbitrary", "arbitrary"), vmem_limit_bytes=V7X_VMEM_LIMIT),
        name="hgrn2",
    )(*args)


def _out_kernel(x_ref, osb_ref, ohg_ref, gpre_ref, wg_ref, bg_ref, wa_ref, wb_ref, wo_ref, gp_ref, y_ref):
    d_model = x_ref.shape[-1]
    x = x_ref[...]
    xn = _rms_norm_bf16(x, gpre_ref[...])
    off = np.cumsum([0, SB_WIDTH, HG_WIDTH, d_model, d_model])
    z_a, hz, ga, gb = (_dot(xn, wg_ref[:, off[i]:off[i + 1]]) for i in range(4))
    u_a = _dot((osb_ref[...] * jax.nn.silu(z_a)).astype(BF16), wa_ref[...])
    u_b = _dot((ohg_ref[...] * jax.nn.silu(hz)).astype(BF16), wb_ref[...])
    bg = bg_ref[...]
    merged = (jax.nn.sigmoid(ga + bg[:, :d_model]) * u_a + jax.nn.sigmoid(gb + bg[:, d_model:]) * u_b)
    m = _dot(merged.astype(BF16), wo_ref[...])
    ms = jnp.mean(m * m, axis=-1, keepdims=True)
    y_ref[...] = x + m * lax.rsqrt(ms + NORM_EPS) * gp_ref[...]


def _out_stage(x2d, o_sb, o_hg, g_pre, w_in_bf16, b_gate, wa, wb, wo, g_post, *, tm):
    n, d = x2d.shape
    assert n % tm == 0
    _, offsets = _in_widths(d)
    w_gate = jnp.concatenate([w_in_bf16[:, offsets[i]:offsets[i + 1]] for i in IN_GATE_GROUPS], axis=1)
    row = lambda w: pl.BlockSpec((tm, w), lambda i: (i, 0))
    const = lambda shape: pl.BlockSpec(shape, lambda i: (0, 0))
    return pl.pallas_call(
        _out_kernel,
        grid=(n // tm,),
        in_specs=[row(d), row(SB_WIDTH), row(HG_WIDTH), const((1, d)), const(w_gate.shape),
                  const((1, 2 * d)), const(wa.shape), const(wb.shape), const(wo.shape), const((1, d))],
        out_specs=row(d),
        out_shape=jax.ShapeDtypeStruct((n, d), F32),
        compiler_params=pltpu.CompilerParams(
            dimension_semantics=("arbitrary",), vmem_limit_bytes=V7X_VMEM_LIMIT),
        name="out_stage",
    )(x2d, o_sb, o_hg, g_pre.reshape(1, d), w_gate, b_gate.reshape(1, 2 * d), wa, wb, wo, g_post.reshape(1, d))


def _largest_tile(n, cap):
    t = cap
    while n % t:
        t //= 2
    return t


def _trunk_layer(x, cache_k, cache_v, s0, layer, g_pre, w_in, b_gate, lb_raw, hg_norm_g,
                 w_a, w_b, w_out, g_post):
    b, t, d = x.shape
    n = b * t
    heads = (SB_HEADS, SB_HEAD_DIM)
    if cache_k is None:
        tm = _largest_tile(t, 256)
        q, kt, vt, hq, hf, hi = _in_proj(x, g_pre, w_in, tm=tm, t_groups=(IN_K, IN_V, IN_HG_V))
        o_sb = _sb_prompt(q, kt, vt, tq=tm)
        k_out, v_out = (a.reshape(b, *heads, t).transpose(0, 3, 1, 2) for a in (kt, vt))
    else:
        tm = _largest_tile(n, 256)
        outs = _in_proj(x.reshape(1, n, d), g_pre, w_in, tm=tm, t_groups=())
        q, k, v, hq, hf, hi = (a.reshape(b, t, a.shape[-1]) for a in outs)
        past = cache_k.shape[1]
        cache_kt, cache_vt = (a.transpose(0, 2, 3, 1).reshape(b, SB_WIDTH, past) for a in (cache_k, cache_v))
        o_sb = _sb_cached(q, k, v, cache_kt, cache_vt, kn=_largest_tile(past, 256))
        k_out, v_out = k.reshape(b, t, *heads), v.reshape(b, t, *heads)
    c = _largest_tile(t, 128)
    o_hg, s_new = _hgrn(hq, hf, hi, lb_raw, hg_norm_g.reshape(-1), s0, layer=layer, c=c,
                        group=_largest_tile(t // c, 4), v_t=cache_k is None)
    flat = lambda a: a.reshape(n, a.shape[-1])
    y = _out_stage(flat(x), flat(o_sb), flat(o_hg), g_pre, w_in, b_gate, w_a, w_b, w_out, g_post,
                   tm=_largest_tile(n, 256))
    return y.reshape(b, t, d), k_out, v_out, s_new


def kernel(x_prompt, x_sample, cache_sb_k, cache_sb_v, state_hgrn, g_pre, w_in, b_gate, hg_lb_raw,
           hg_norm_g, w_branch_a, w_branch_b, w_out, g_post):
    depth = w_in.shape[0]
    y_p, y_s = x_prompt, x_sample
    outs = [[] for _ in range(6)]
    for l in range(depth):
        weights = (g_pre[l], w_in[l].astype(BF16), b_gate[l], hg_lb_raw, hg_norm_g[l],
                   w_branch_a[l].astype(BF16), w_branch_b[l].astype(BF16), w_out[l].astype(BF16), g_post[l])
        y_p, kp, vp, sp = _trunk_layer(y_p, None, None, None, l, *weights)
        y_s, ks, vs, ss = _trunk_layer(y_s, cache_sb_k[l], cache_sb_v[l], state_hgrn[l], l, *weights)
        for lst, a in zip(outs, (kp, vp, sp, ks, vs, ss)):
            lst.append(a)
    return (y_p, y_s) + tuple(jnp.stack(lst) for lst in outs)
```

```python
import functools
import math

import numpy as np
import jax
import jax.numpy as jnp
from jax import lax
from jax.experimental import pallas as pl
from jax.experimental.pallas import tpu as pltpu

SB_HEADS = 8
SB_HEAD_DIM = 64
SB_WIDTH = SB_HEADS * SB_HEAD_DIM
HG_HEADS = 4
HG_KEY_DIM = 128
HG_VAL_DIM = 128
HG_WIDTH = HG_HEADS * HG_KEY_DIM
HG_BLOCK = 16
NORM_EPS = 1e-6

V7X_LANES = 128
V7X_VMEM_LIMIT = 56 * 1024 * 1024
SB_PAIRS = SB_WIDTH // V7X_LANES

SB_LOG2_UNDERFLOW = 150.5
LOG2_E = math.log2(math.e)

F32 = jnp.float32
BF16 = jnp.bfloat16

_NT = (((1,), (1,)), ((), ()))
_TN = (((0,), (0,)), ((), ()))


def _dot(a, b):
    return jnp.dot(a, b, preferred_element_type=F32)


def _dot_nt(a, b):
    return lax.dot_general(a, b, _NT, preferred_element_type=F32)


def _dot_tn(a, b):
    return lax.dot_general(a, b, _TN, preferred_element_type=F32)


def _split_bf16(x):
    hi = x.astype(BF16)
    lo = (x - hi.astype(F32)).astype(BF16)
    return hi, lo


IN_Q, IN_K, IN_V, IN_ZA, IN_HG_Q, IN_HG_F, IN_HG_V, IN_HG_Z, IN_GATE_A, IN_GATE_B = range(10)
IN_MIXER_GROUPS = (IN_Q, IN_K, IN_V, IN_HG_Q, IN_HG_F, IN_HG_V)
IN_GATE_GROUPS = (IN_ZA, IN_HG_Z, IN_GATE_A, IN_GATE_B)


def _in_widths(d_model):
    widths = (SB_WIDTH,) * 4 + (HG_WIDTH,) * 4 + (d_model, d_model)
    return widths, np.concatenate([[0], np.cumsum(widths)])


def _rms_norm_bf16(x, g):
    ms = jnp.mean(x * x, axis=-1, keepdims=True)
    return (x * lax.rsqrt(ms + NORM_EPS) * g).astype(BF16)


def _in_proj_kernel(*refs, groups, t_groups, widths, offsets, q_scale):
    if t_groups:
        x_ref, g_ref, w_ref, wt_ref = refs[:4]
        out_refs = refs[4:]
    else:
        x_ref, g_ref, w_ref = refs[:3]
        out_refs = refs[3:]
    xn = _rms_norm_bf16(x_ref[0], g_ref[...])
    if t_groups:
        ht = _dot_nt(wt_ref[...], xn)
        off = 0
        for idx in t_groups:
            out_refs[groups.index(idx)][0] = ht[off:off + widths[idx]]
            off += widths[idx]
    for ref, idx in zip(out_refs, groups):
        if idx not in t_groups:
            h = _dot(xn, w_ref[:, offsets[idx]:offsets[idx + 1]])
            if idx == IN_Q:
                h = h * q_scale
            ref[0] = h.astype(ref.dtype)


def _in_proj(x, g_pre, w_in_bf16, *, tm, t_groups):
    b, t, d = x.shape
    widths, offsets = _in_widths(d)
    offsets = tuple(int(o) for o in offsets)
    assert offsets[-1] == w_in_bf16.shape[1] and t % tm == 0
    out_shape, out_specs = [], []
    for idx in IN_MIXER_GROUPS:
        w = widths[idx]
        dt = BF16 if idx == IN_Q else F32
        if idx in t_groups:
            out_shape.append(jax.ShapeDtypeStruct((b, w, t), dt))
            out_specs.append(pl.BlockSpec((1, w, tm), lambda bi, ti: (bi, 0, ti)))
        else:
            out_shape.append(jax.ShapeDtypeStruct((b, t, w), dt))
            out_specs.append(pl.BlockSpec((1, tm, w), lambda bi, ti: (bi, ti, 0)))
    in_specs = [
        pl.BlockSpec((1, tm, d), lambda bi, ti: (bi, ti, 0)),
        pl.BlockSpec((1, d), lambda bi, ti: (0, 0)),
        pl.BlockSpec(w_in_bf16.shape, lambda bi, ti: (0, 0)),
    ]
    args = [x, g_pre.reshape(1, d), w_in_bf16]
    if t_groups:
        w_t = jnp.concatenate([w_in_bf16[:, offsets[i]:offsets[i + 1]] for i in t_groups], axis=1).T
        in_specs.append(pl.BlockSpec(w_t.shape, lambda bi, ti: (0, 0)))
        args.append(w_t)
    return pl.pallas_call(
        functools.partial(_in_proj_kernel, groups=IN_MIXER_GROUPS, t_groups=t_groups, widths=widths,
                          offsets=offsets, q_scale=SB_HEAD_DIM ** -0.5 * LOG2_E),
        grid=(b, t // tm),
        in_specs=in_specs,
        out_specs=tuple(out_specs),
        out_shape=tuple(out_shape),
        compiler_params=pltpu.CompilerParams(
            dimension_semantics=("arbitrary", "arbitrary"), vmem_limit_bytes=V7X_VMEM_LIMIT),
        name="in_proj",
    )(*args)


def _suffix_ones(n, copies):
    r = np.arange(n)
    u = (r[:, None] >= r[None, :]).astype(np.float32)
    return jnp.asarray(np.concatenate([u] * copies, axis=0), dtype=BF16)


def _sb_weights(z, u2, carry, mask):
    sp = jnp.maximum(z, 0.0) + jnp.log(1.0 + jnp.exp2(-jnp.abs(z))) * LOG2_E
    if mask is not None:
        sp = jnp.where(mask, sp, 0.0)
    hi, lo = _split_bf16(sp)
    tot = _dot(jnp.concatenate([hi, lo], axis=1), u2) + carry
    a = jnp.exp2(z - tot)
    if mask is not None:
        a = jnp.where(mask, a, 0.0)
    return a.astype(BF16), tot[:, 0:1]


def _head_split(x2, first):
    zero = jnp.zeros_like(x2)
    return jnp.where(first, x2, zero), jnp.where(first, zero, x2)


def _sb_prompt_kernel(q_ref, kt_ref, vt_ref, u2_ref, o_ref, kt_scr, ve_scr, vo_scr, acc_scr, car_scr, *, tq):
    i = pl.program_id(1)
    n_chunks = kt_scr.shape[0]

    @pl.when(i == 0)
    def _():
        row = lax.broadcasted_iota(jnp.int32, (SB_WIDTH, tq), 0)
        even = (row // SB_HEAD_DIM) % 2 == 0
        for n in range(n_chunks):
            kt_scr[n] = kt_ref[0, :, n * tq:(n + 1) * tq].astype(BF16)
            v = vt_ref[0, :, n * tq:(n + 1) * tq]
            ve_scr[n] = jnp.where(even, v, 0.0).astype(BF16)
            vo_scr[n] = jnp.where(even, 0.0, v).astype(BF16)

    first = lax.broadcasted_iota(jnp.int32, (tq, V7X_LANES), 1) < SB_HEAD_DIM
    r = lax.broadcasted_iota(jnp.int32, (tq, tq), 0)
    c = lax.broadcasted_iota(jnp.int32, (tq, tq), 1)
    causal = c < r
    u2 = u2_ref[...]

    def sweep(n, diag):
        mask = causal if diag else None
        qs, z, split, tot, wts, pv = [], {}, {}, {}, {}, {}
        for p in range(SB_PAIRS):
            qs.extend(_head_split(q_ref[0, :, V7X_LANES * p:V7X_LANES * (p + 1)], first))

        def rows(h):
            return slice(V7X_LANES * (h // 2), V7X_LANES * (h // 2 + 1))

        def scores(h):
            z[h] = _dot(qs[h], kt_scr[n, rows(h), :])

        def softplus(h):
            sp = jnp.maximum(z[h], 0.0) + jnp.log(1.0 + jnp.exp2(-jnp.abs(z[h]))) * LOG2_E
            if diag:
                sp = jnp.where(mask, sp, 0.0)
            split[h] = jnp.concatenate(_split_bf16(sp), axis=1)

        def suffix_sums(h):
            carry = jnp.zeros((tq, 1), F32) if diag else car_scr[h]
            tot[h] = _dot(split.pop(h), u2) + carry

        def weights(h):
            a = jnp.exp2(z.pop(h) - tot[h])
            if diag:
                a = jnp.where(mask, a, 0.0)
            wts[h] = a.astype(BF16)
            tot[h] = tot[h][:, 0:1]

        def weighted_values(h):
            v_scr = vo_scr if h % 2 else ve_scr
            pv[h] = _dot_nt(wts.pop(h), v_scr[n, rows(h), :])

        stages = (scores, softplus, suffix_sums, weights, weighted_values)
        for step in range(SB_HEADS + len(stages) - 1):
            for k, stage in enumerate(stages):
                if 0 <= step - k < SB_HEADS:
                    stage(step - k)

        lowest = None
        for p in range(SB_PAIRS):
            new = pv[2 * p] + pv[2 * p + 1]
            acc_scr[p] = new if diag else acc_scr[p] + new
        for h in range(SB_HEADS):
            car_scr[h] = tot[h]
            lowest = tot[h] if lowest is None else jnp.minimum(lowest, tot[h])
        return jnp.min(lowest)

    def cond(st):
        n, lowest = st
        return jnp.logical_and(n >= 0, lowest <= SB_LOG2_UNDERFLOW)

    def body(st):
        n, _ = st
        return n - 1, sweep(n, False)

    lax.while_loop(cond, body, (i - 1, sweep(i, True)))
    for p in range(SB_PAIRS):
        o_ref[0, :, V7X_LANES * p:V7X_LANES * (p + 1)] = acc_scr[p]


def _sb_prompt(q, kt, vt, *, tq):
    b, t, w = q.shape
    assert w == SB_WIDTH and t % tq == 0 and kt.shape == (b, w, t)
    n_chunks = t // tq
    kv_spec = pl.BlockSpec((1, w, t), lambda bi, i: (bi, 0, 0))
    q_spec = pl.BlockSpec((1, tq, w), lambda bi, i: (bi, i, 0))
    return pl.pallas_call(
        functools.partial(_sb_prompt_kernel, tq=tq),
        grid=(b, n_chunks),
        in_specs=[q_spec, kv_spec, kv_spec, pl.BlockSpec((2 * tq, tq), lambda bi, i: (0, 0))],
        out_specs=q_spec,
        out_shape=jax.ShapeDtypeStruct((b, t, w), F32),
        scratch_shapes=[pltpu.VMEM((n_chunks, w, tq), BF16)] * 3
        + [pltpu.VMEM((SB_PAIRS, tq, V7X_LANES), F32), pltpu.VMEM((SB_HEADS, tq, 1), F32)],
        compiler_params=pltpu.CompilerParams(
            dimension_semantics=("arbitrary", "arbitrary"), vmem_limit_bytes=V7X_VMEM_LIMIT),
        name="sb_prompt",
    )(q, kt, vt, _suffix_ones(tq, 2))


def _sb_cached_kernel(q_ref, k_ref, v_ref, ckt_ref, cvt_ref, ud_ref, uc_ref, o_ref, *, tq, kn, n_chunks):
    first = lax.broadcasted_iota(jnp.int32, (tq, V7X_LANES), 1) < SB_HEAD_DIM
    r = lax.broadcasted_iota(jnp.int32, (tq, tq), 0)
    c = lax.broadcasted_iota(jnp.int32, (tq, tq), 1)
    causal = c < r
    even = (lax.broadcasted_iota(jnp.int32, (V7X_LANES, kn), 0) // SB_HEAD_DIM) % 2 == 0
    ud = ud_ref[...]
    uc = uc_ref[...]
    for p in range(SB_PAIRS):
        rows = slice(V7X_LANES * p, V7X_LANES * (p + 1))
        qs = _head_split(q_ref[0, :, rows], first)
        kd = k_ref[0, :, rows].astype(BF16)
        vs = _head_split(v_ref[0, :, rows], first)
        acc = jnp.zeros((tq, V7X_LANES), F32)
        carries = []
        for h in range(2):
            a, carry = _sb_weights(_dot_nt(qs[h], kd), ud, jnp.zeros((tq, 1), F32), causal)
            acc = acc + _dot(a, vs[h].astype(BF16))
            carries.append(carry)
        for n in range(n_chunks - 1, -1, -1):
            k2 = ckt_ref[0, rows, n * kn:(n + 1) * kn].astype(BF16)
            v = cvt_ref[0, rows, n * kn:(n + 1) * kn]
            vv = jnp.concatenate([jnp.where(even, v, 0.0), jnp.where(even, 0.0, v)], axis=1).astype(BF16)
            parts = []
            for h in range(2):
                a, carries[h] = _sb_weights(_dot(qs[h], k2), uc, carries[h], None)
                parts.append(a)
            acc = acc + _dot_nt(jnp.concatenate(parts, axis=1), vv)
        o_ref[0, :, rows] = acc


def _sb_cached(q, k, v, cache_kt, cache_vt, *, kn):
    b, tq, w = q.shape
    past = cache_kt.shape[2]
    assert w == SB_WIDTH and past % kn == 0 and cache_kt.shape == (b, w, past)
    new_spec = pl.BlockSpec((1, tq, w), lambda bi: (bi, 0, 0))
    cache_spec = pl.BlockSpec((1, w, past), lambda bi: (bi, 0, 0))
    return pl.pallas_call(
        functools.partial(_sb_cached_kernel, tq=tq, kn=kn, n_chunks=past // kn),
        grid=(b,),
        in_specs=[new_spec, new_spec, new_spec, cache_spec, cache_spec,
                  pl.BlockSpec((2 * tq, tq), lambda bi: (0, 0)),
                  pl.BlockSpec((2 * kn, kn), lambda bi: (0, 0))],
        out_specs=new_spec,
        out_shape=jax.ShapeDtypeStruct((b, tq, w), F32),
        compiler_params=pltpu.CompilerParams(
            dimension_semantics=("arbitrary",), vmem_limit_bytes=V7X_VMEM_LIMIT),
        name="sb_cached",
    )(q, k, v, cache_kt, cache_vt, _suffix_ones(tq, 2), _suffix_ones(kn, 2))


def _hgrn_consts(c):
    t = np.arange(c)
    sub = t // HG_BLOCK
    same = sub[:, None] == sub[None, :]
    tri = (same & (t[None, :] <= t[:, None])).astype(np.float32)
    ones = same.astype(np.float32)
    x = sub[:, None] ^ sub[None, :]
    level = np.where(x == 0, 0, np.floor(np.log2(np.maximum(x, 1))).astype(np.int64) + 1)
    level = np.where(t[None, :] <= t[:, None], level, -1).astype(np.int32)
    pre = np.block([[tri, tri], [ones, ones]])
    return jnp.asarray(pre, dtype=BF16), jnp.asarray(level)


def _prod(vecs):
    out = None
    for v in vecs:
        out = v if out is None else out * v
    return out


def _rows(vecs, width):
    one = jnp.ones((HG_BLOCK, width), F32)
    return jnp.concatenate(
        [one if v is None else jnp.broadcast_to(v, (HG_BLOCK, width)) for v in vecs], axis=0)


def _hgrn_group(read, write, lb, g, pre, lvl, states, *, c, group, v_t):
    width = HG_WIDTH
    nsub = c // HG_BLOCK
    levels = int(round(math.log2(nsub)))
    states = list(states)
    ch = [dict() for _ in range(group)]
    item = {}

    def head(h):
        return slice(HG_KEY_DIM * h, HG_KEY_DIM * (h + 1))

    def gates(gi):
        hq, hf, v = read(gi)
        f = lb + (1.0 - lb) * jax.nn.sigmoid(hf)
        ch[gi].update(hq=hq, kb=1.0 - f,
                      v_bf=v.astype(BF16), split=jnp.concatenate(_split_bf16(jnp.log(f)), axis=0))

    def prefix(gi):
        ch[gi]["sums"] = _dot(pre, ch[gi].pop("split"))

    def decays(gi):
        sums, hq, kb = (ch[gi].pop(k) for k in ("sums", "hq", "kb"))
        b16 = sums[:c]
        bend = sums[c:]
        q_dec = hq * jnp.exp(b16)
        k_inv = kb * jnp.exp(-b16)
        k_end = kb * jnp.exp(bend - b16)
        dmat = jnp.exp(bend)
        d = [dmat[HG_BLOCK * i:HG_BLOCK * i + 1, :] for i in range(nsub)]
        q_lv = [q_dec.astype(BF16)]
        k_lv = [k_inv.astype(BF16)]
        if levels >= 1:
            q_lv.append(q_lv[0])
            k_lv.append(k_end.astype(BF16))
        for l in range(2, levels + 1):
            half = 2 ** (l - 1)
            alpha, beta = [], []
            for i in range(nsub):
                mid = (i // (2 * half)) * 2 * half + half
                alpha.append(_prod(d[mid:i]) if i > mid else None)
                beta.append(_prod(d[i + 1:mid]) if i + 1 < mid else None)
            q_lv.append((q_dec * _rows(alpha, width)).astype(BF16))
            k_lv.append((k_end * _rows(beta, width)).astype(BF16))
        if nsub > 1:
            alpha_c = [_prod(d[:i]) if i > 0 else None for i in range(nsub)]
            beta_c = [_prod(d[i + 1:]) if i + 1 < nsub else None for i in range(nsub)]
            q_c = (q_dec * _rows(alpha_c, width)).astype(BF16)
            k_c = (k_end * _rows(beta_c, width)).astype(BF16)
        else:
            q_c = q_lv[0]
            k_c = k_end.astype(BF16)
        ch[gi].update(q_lv=q_lv, k_lv=k_lv, q_c=q_c, k_c=k_c, d_all=_prod(d))

    def level_scores(gi, h):
        q_lv, k_lv = ch[gi]["q_lv"], ch[gi]["k_lv"]
        item[gi, h] = [_dot_nt(q_lv[l][:, head(h)], k_lv[l][:, head(h)]) for l in range(levels + 1)]

    def combine(gi, h):
        att = jnp.zeros((c, c), F32)
        for l, scores in enumerate(item[gi, h]):
            att = jnp.where(lvl == l, scores, att)
        item[gi, h] = att.astype(BF16)

    def outputs(gi, h):
        att = item[gi, h]
        v_bf, q_c, k_c = ch[gi]["v_bf"], ch[gi]["q_c"], ch[gi]["k_c"]
        if v_t:
            vt_h = v_bf[head(h), :]
            o_intra = _dot_nt(att, vt_h)
            st_add = _dot(vt_h, k_c[:, head(h)])
        else:
            o_intra = _dot(att, v_bf[:, head(h)])
            st_add = _dot_tn(v_bf[:, head(h)], k_c[:, head(h)])
        item[gi, h] = (o_intra + _dot_nt(q_c[:, head(h)], states[h].astype(BF16)), st_add)

    def finish(gi, h):
        o, st_add = item.pop((gi, h))
        ms = jnp.mean(o * o, axis=-1, keepdims=True)
        write(gi, h, o * lax.rsqrt(ms + NORM_EPS) * g[:, head(h)])
        states[h] = states[h] * ch[gi]["d_all"][:, head(h)] + st_add

    chunk_stages = (gates, prefix, decays)
    item_stages = (level_scores, combine, outputs, finish)
    n_items = group * HG_HEADS
    for step in range(-len(chunk_stages), n_items + len(item_stages) - 1):
        for gi in range(group):
            for k, stage in enumerate(chunk_stages):
                if step == HG_HEADS * gi - len(chunk_stages) + k:
                    stage(gi)
        for k, stage in enumerate(item_stages):
            if 0 <= step - k < n_items:
                stage(*divmod(step - k, HG_HEADS))
    return states


def _hgrn_kernel(*refs, c, group, layer, has_s0, v_t, n_steps):
    if has_s0:
        (hq_ref, hf_ref, hi_ref, lbraw_ref, g_ref, pre_ref, lvl_ref, s0_ref,
         o_ref, sout_ref, st_scr) = refs
    else:
        (hq_ref, hf_ref, hi_ref, lbraw_ref, g_ref, pre_ref, lvl_ref,
         o_ref, sout_ref, st_scr) = refs
        s0_ref = None
    ci = pl.program_id(1)

    @pl.when(ci == 0)
    def _():
        for h in range(HG_HEADS):
            if has_s0:
                st_scr[h] = s0_ref[0, h].T
            else:
                st_scr[h] = jnp.zeros((HG_VAL_DIM, HG_KEY_DIM), F32)

    raw = lbraw_ref[...]
    e = jnp.exp(raw - jnp.max(raw, axis=0, keepdims=True))
    lb = jnp.sum(e[:layer + 1], axis=0, keepdims=True) / jnp.sum(e, axis=0, keepdims=True)

    pre = pre_ref[...]
    lvl = lvl_ref[...]
    g = g_ref[...]
    def read(gi):
        rs = slice(gi * c, (gi + 1) * c)
        return hq_ref[0, rs, :], hf_ref[0, rs, :], hi_ref[0, :, rs] if v_t else hi_ref[0, rs, :]

    def write(gi, h, o):
        o_ref[0, gi * c:(gi + 1) * c, HG_VAL_DIM * h:HG_VAL_DIM * (h + 1)] = o

    states = _hgrn_group(read, write, lb, g, pre, lvl, [st_scr[h] for h in range(HG_HEADS)],
                         c=c, group=group, v_t=v_t)
    for h in range(HG_HEADS):
        st_scr[h] = states[h]

    @pl.when(ci == n_steps - 1)
    def _():
        for h in range(HG_HEADS):
            sout_ref[0, h] = st_scr[h].T


def _hgrn(hq, hf, hi, lb_raw, norm_g, s0, *, layer, c, group, v_t):
    b, t, w = hq.shape
    span = c * group
    assert w == HG_WIDTH and t % span == 0 and c % HG_BLOCK == 0
    n_steps = t // span
    pre, level = _hgrn_consts(c)
    tok = pl.BlockSpec((1, span, w), lambda bi, ci: (bi, ci, 0))
    tok_t = pl.BlockSpec((1, w, span), lambda bi, ci: (bi, 0, ci))
    const2 = lambda shape: pl.BlockSpec(shape, lambda bi, ci: (0, 0))
    st_spec = pl.BlockSpec((1, HG_HEADS, HG_KEY_DIM, HG_VAL_DIM), lambda bi, ci: (bi, 0, 0, 0))
    in_specs = [tok, tok, tok_t if v_t else tok, const2(lb_raw.shape), const2((1, w)),
                const2(pre.shape), const2(level.shape)]
    args = [hq, hf, hi, lb_raw, norm_g.reshape(1, w), pre, level]
    if s0 is not None:
        in_specs.append(st_spec)
        args.append(s0)
    return pl.pallas_call(
        functools.partial(_hgrn_kernel, c=c, group=group, layer=layer, has_s0=s0 is not None, v_t=v_t,
                          n_steps=n_steps),
        grid=(b, n_steps),
        in_specs=in_specs,
        out_specs=(tok, st_spec),
        out_shape=(jax.ShapeDtypeStruct((b, t, w), F32),
                   jax.ShapeDtypeStruct((b, HG_HEADS, HG_KEY_DIM, HG_VAL_DIM), F32)),
        scratch_shapes=[pltpu.VMEM((HG_HEADS, HG_VAL_DIM, HG_KEY_DIM), F32)],
        compiler_params=pltpu.CompilerParams(
            dimension_semantics=("arbitrary", "arbitrary"), vmem_limit_bytes=V7X_VMEM_LIMIT),
        name="hgrn2",
    )(*args)


def _out_kernel(x_ref, osb_ref, ohg_ref, gpre_ref, wg_ref, bg_ref, wa_ref, wb_ref, wo_ref, gp_ref, y_ref):
    d_model = x_ref.shape[-1]
    x = x_ref[...]
    xn = _rms_norm_bf16(x, gpre_ref[...])
    off = np.cumsum([0, SB_WIDTH, HG_WIDTH, d_model, d_model])
    z_a, hz, ga, gb = (_dot(xn, wg_ref[:, off[i]:off[i + 1]]) for i in range(4))
    u_a = _dot((osb_ref[...] * jax.nn.silu(z_a)).astype(BF16), wa_ref[...])
    u_b = _dot((ohg_ref[...] * jax.nn.silu(hz)).astype(BF16), wb_ref[...])
    bg = bg_ref[...]
    merged = (jax.nn.sigmoid(ga + bg[:, :d_model]) * u_a + jax.nn.sigmoid(gb + bg[:, d_model:]) * u_b)
    m = _dot(merged.astype(BF16), wo_ref[...])
    ms = jnp.mean(m * m, axis=-1, keepdims=True)
    y_ref[...] = x + m * lax.rsqrt(ms + NORM_EPS) * gp_ref[...]


def _out_stage(x2d, o_sb, o_hg, g_pre, w_in_bf16, b_gate, wa, wb, wo, g_post, *, tm):
    n, d = x2d.shape
    assert n % tm == 0
    _, offsets = _in_widths(d)
    w_gate = jnp.concatenate([w_in_bf16[:, offsets[i]:offsets[i + 1]] for i in IN_GATE_GROUPS], axis=1)
    row = lambda w: pl.BlockSpec((tm, w), lambda i: (i, 0))
    const = lambda shape: pl.BlockSpec(shape, lambda i: (0, 0))
    return pl.pallas_call(
        _out_kernel,
        grid=(n // tm,),
        in_specs=[row(d), row(SB_WIDTH), row(HG_WIDTH), const((1, d)), const(w_gate.shape),
                  const((1, 2 * d)), const(wa.shape), const(wb.shape), const(wo.shape), const((1, d))],
        out_specs=row(d),
        out_shape=jax.ShapeDtypeStruct((n, d), F32),
        compiler_params=pltpu.CompilerParams(
            dimension_semantics=("arbitrary",), vmem_limit_bytes=V7X_VMEM_LIMIT),
        name="out_stage",
    )(x2d, o_sb, o_hg, g_pre.reshape(1, d), w_gate, b_gate.reshape(1, 2 * d), wa, wb, wo, g_post.reshape(1, d))


def _largest_tile(n, cap):
    t = cap
    while n % t:
        t //= 2
    return t


def _trunk_layer(x, cache_k, cache_v, s0, layer, g_pre, w_in, b_gate, lb_raw, hg_norm_g,
                 w_a, w_b, w_out, g_post):
    b, t, d = x.shape
    n = b * t
    heads = (SB_HEADS, SB_HEAD_DIM)
    if cache_k is None:
        q, kt, vt, hq, hf, hi = _in_proj(x, g_pre, w_in, tm=_largest_tile(t, 512), t_groups=(IN_K, IN_V, IN_HG_V))
        o_sb = _sb_prompt(q, kt, vt, tq=_largest_tile(t, 256))
        k_out, v_out = (a.reshape(b, *heads, t).transpose(0, 3, 1, 2) for a in (kt, vt))
    else:
        tm = _largest_tile(n, 256)
        outs = _in_proj(x.reshape(1, n, d), g_pre, w_in, tm=tm, t_groups=())
        q, k, v, hq, hf, hi = (a.reshape(b, t, a.shape[-1]) for a in outs)
        past = cache_k.shape[1]
        cache_kt, cache_vt = (a.transpose(0, 2, 3, 1).reshape(b, SB_WIDTH, past) for a in (cache_k, cache_v))
        o_sb = _sb_cached(q, k, v, cache_kt, cache_vt, kn=_largest_tile(past, 256))
        k_out, v_out = k.reshape(b, t, *heads), v.reshape(b, t, *heads)
    c = _largest_tile(t, 128)
    o_hg, s_new = _hgrn(hq, hf, hi, lb_raw, hg_norm_g.reshape(-1), s0, layer=layer, c=c,
                        group=_largest_tile(t // c, 4), v_t=cache_k is None)
    flat = lambda a: a.reshape(n, a.shape[-1])
    y = _out_stage(flat(x), flat(o_sb), flat(o_hg), g_pre, w_in, b_gate, w_a, w_b, w_out, g_post,
                   tm=_largest_tile(n, 256))
    return y.reshape(b, t, d), k_out, v_out, s_new


def kernel(x_prompt, x_sample, cache_sb_k, cache_sb_v, state_hgrn, g_pre, w_in, b_gate, hg_lb_raw,
           hg_norm_g, w_branch_a, w_branch_b, w_out, g_post):
    depth = w_in.shape[0]
    y_p, y_s = x_prompt, x_sample
    outs = [[] for _ in range(6)]
    for l in range(depth):
        weights = (g_pre[l], w_in[l].astype(BF16), b_gate[l], hg_lb_raw, hg_norm_g[l],
                   w_branch_a[l].astype(BF16), w_branch_b[l].astype(BF16), w_out[l].astype(BF16), g_post[l])
        y_p, kp, vp, sp = _trunk_layer(y_p, None, None, None, l, *weights)
        y_s, ks, vs, ss = _trunk_layer(y_s, cache_sb_k[l], cache_sb_v[l], state_hgrn[l], l, *weights)
        for lst, a in zip(outs, (kp, vp, sp, ks, vs, ss)):
            lst.append(a)
    return (y_p, y_s) + tuple(jnp.stack(lst) for lst in outs)
```

```python
import functools
import math

import numpy as np
import jax
import jax.numpy as jnp
from jax import lax
from jax.experimental import pallas as pl
from jax.experimental.pallas import tpu as pltpu

SB_HEADS = 8
SB_HEAD_DIM = 64
SB_WIDTH = SB_HEADS * SB_HEAD_DIM
HG_HEADS = 4
HG_KEY_DIM = 128
HG_VAL_DIM = 128
HG_WIDTH = HG_HEADS * HG_KEY_DIM
HG_BLOCK = 16
NORM_EPS = 1e-6

V7X_LANES = 128
V7X_VMEM_LIMIT = 56 * 1024 * 1024
SB_PAIRS = SB_WIDTH // V7X_LANES

SB_LOG2_UNDERFLOW = 150.5
LOG2_E = math.log2(math.e)

F32 = jnp.float32
BF16 = jnp.bfloat16

_NT = (((1,), (1,)), ((), ()))
_TN = (((0,), (0,)), ((), ()))


def _dot(a, b):
    return jnp.dot(a, b, preferred_element_type=F32)


def _dot_nt(a, b):
    return lax.dot_general(a, b, _NT, preferred_element_type=F32)


def _dot_tn(a, b):
    return lax.dot_general(a, b, _TN, preferred_element_type=F32)


def _split_bf16(x):
    hi = x.astype(BF16)
    lo = (x - hi.astype(F32)).astype(BF16)
    return hi, lo


IN_Q, IN_K, IN_V, IN_ZA, IN_HG_Q, IN_HG_F, IN_HG_V, IN_HG_Z, IN_GATE_A, IN_GATE_B = range(10)
IN_MIXER_GROUPS = (IN_Q, IN_K, IN_V, IN_HG_Q, IN_HG_F, IN_HG_V)
IN_GATE_GROUPS = (IN_ZA, IN_HG_Z, IN_GATE_A, IN_GATE_B)


def _in_widths(d_model):
    widths = (SB_WIDTH,) * 4 + (HG_WIDTH,) * 4 + (d_model, d_model)
    return widths, np.concatenate([[0], np.cumsum(widths)])


def _rms_norm_bf16(x, g):
    ms = jnp.mean(x * x, axis=-1, keepdims=True)
    return (x * lax.rsqrt(ms + NORM_EPS) * g).astype(BF16)


def _in_proj_kernel(*refs, groups, t_groups, widths, offsets, q_scale):
    if t_groups:
        x_ref, g_ref, w_ref, wt_ref = refs[:4]
        out_refs = refs[4:]
    else:
        x_ref, g_ref, w_ref = refs[:3]
        out_refs = refs[3:]
    xn = _rms_norm_bf16(x_ref[0], g_ref[...])
    if t_groups:
        ht = _dot_nt(wt_ref[...], xn)
        off = 0
        for idx in t_groups:
            out_refs[groups.index(idx)][0] = ht[off:off + widths[idx]]
            off += widths[idx]
    for ref, idx in zip(out_refs, groups):
        if idx not in t_groups:
            h = _dot(xn, w_ref[:, offsets[idx]:offsets[idx + 1]])
            if idx == IN_Q:
                h = h * q_scale
            ref[0] = h.astype(ref.dtype)


def _in_proj(x, g_pre, w_in_bf16, *, tm, t_groups):
    b, t, d = x.shape
    widths, offsets = _in_widths(d)
    offsets = tuple(int(o) for o in offsets)
    assert offsets[-1] == w_in_bf16.shape[1] and t % tm == 0
    out_shape, out_specs = [], []
    for idx in IN_MIXER_GROUPS:
        w = widths[idx]
        dt = BF16 if idx == IN_Q else F32
        if idx in t_groups:
            out_shape.append(jax.ShapeDtypeStruct((b, w, t), dt))
            out_specs.append(pl.BlockSpec((1, w, tm), lambda bi, ti: (bi, 0, ti)))
        else:
            out_shape.append(jax.ShapeDtypeStruct((b, t, w), dt))
            out_specs.append(pl.BlockSpec((1, tm, w), lambda bi, ti: (bi, ti, 0)))
    in_specs = [
        pl.BlockSpec((1, tm, d), lambda bi, ti: (bi, ti, 0)),
        pl.BlockSpec((1, d), lambda bi, ti: (0, 0)),
        pl.BlockSpec(w_in_bf16.shape, lambda bi, ti: (0, 0)),
    ]
    args = [x, g_pre.reshape(1, d), w_in_bf16]
    if t_groups:
        w_t = jnp.concatenate([w_in_bf16[:, offsets[i]:offsets[i + 1]] for i in t_groups], axis=1).T
        in_specs.append(pl.BlockSpec(w_t.shape, lambda bi, ti: (0, 0)))
        args.append(w_t)
    return pl.pallas_call(
        functools.partial(_in_proj_kernel, groups=IN_MIXER_GROUPS, t_groups=t_groups, widths=widths,
                          offsets=offsets, q_scale=SB_HEAD_DIM ** -0.5 * LOG2_E),
        grid=(b, t // tm),
        in_specs=in_specs,
        out_specs=tuple(out_specs),
        out_shape=tuple(out_shape),
        compiler_params=pltpu.CompilerParams(
            dimension_semantics=("arbitrary", "arbitrary"), vmem_limit_bytes=V7X_VMEM_LIMIT),
        name="in_proj",
    )(*args)


def _suffix_ones(n, copies):
    r = np.arange(n)
    u = (r[:, None] >= r[None, :]).astype(np.float32)
    return jnp.asarray(np.concatenate([u] * copies, axis=0), dtype=BF16)


def _sb_weights(z, u2, carry, mask):
    sp = jnp.maximum(z, 0.0) + jnp.log(1.0 + jnp.exp2(-jnp.abs(z))) * LOG2_E
    if mask is not None:
        sp = jnp.where(mask, sp, 0.0)
    hi, lo = _split_bf16(sp)
    tot = _dot(jnp.concatenate([hi, lo], axis=1), u2) + carry
    a = jnp.exp2(z - tot)
    if mask is not None:
        a = jnp.where(mask, a, 0.0)
    return a.astype(BF16), tot[:, 0:1]


def _head_split(x2, first):
    zero = jnp.zeros_like(x2)
    return jnp.where(first, x2, zero), jnp.where(first, zero, x2)


def _sb_prompt_kernel(q_ref, kt_ref, vt_ref, u_ref, o_ref, kt_scr, ve_scr, vo_scr, acc_scr, car_scr, *, tq):
    i = pl.program_id(1)
    n_chunks = kt_scr.shape[0]

    @pl.when(i == 0)
    def _():
        row = lax.broadcasted_iota(jnp.int32, (SB_WIDTH, tq), 0)
        even = (row // SB_HEAD_DIM) % 2 == 0
        for n in range(n_chunks):
            kt_scr[n] = kt_ref[0, :, n * tq:(n + 1) * tq].astype(BF16)
            v = vt_ref[0, :, n * tq:(n + 1) * tq]
            ve_scr[n] = jnp.where(even, v, 0.0).astype(BF16)
            vo_scr[n] = jnp.where(even, 0.0, v).astype(BF16)

    first = lax.broadcasted_iota(jnp.int32, (tq, V7X_LANES), 1) < SB_HEAD_DIM
    r = lax.broadcasted_iota(jnp.int32, (tq, tq), 0)
    c = lax.broadcasted_iota(jnp.int32, (tq, tq), 1)
    causal = c < r
    u = u_ref[...]

    def sweep(n, diag):
        mask = causal if diag else None
        qs, z, sp16, tot, wts, pv = [], {}, {}, {}, {}, {}
        for p in range(SB_PAIRS):
            qs.extend(_head_split(q_ref[0, :, V7X_LANES * p:V7X_LANES * (p + 1)], first))

        def rows(h):
            return slice(V7X_LANES * (h // 2), V7X_LANES * (h // 2 + 1))

        def scores(h):
            z[h] = _dot(qs[h], kt_scr[n, rows(h), :])

        def softplus(h):
            sp = jnp.maximum(z[h], 0.0) + jnp.log(1.0 + jnp.exp2(-jnp.abs(z[h]))) * LOG2_E
            if diag:
                sp = jnp.where(mask, sp, 0.0)
            sp16[h] = sp.astype(BF16)

        def suffix_sums(h):
            carry = jnp.zeros((tq, 1), F32) if diag else car_scr[h]
            tot[h] = _dot(sp16.pop(h), u) + carry

        def weights(h):
            a = jnp.exp2(z.pop(h) - tot[h])
            if diag:
                a = jnp.where(mask, a, 0.0)
            wts[h] = a.astype(BF16)
            tot[h] = tot[h][:, 0:1]

        def weighted_values(h):
            v_scr = vo_scr if h % 2 else ve_scr
            pv[h] = _dot_nt(wts.pop(h), v_scr[n, rows(h), :])

        stages = (scores, softplus, suffix_sums, weights, weighted_values)
        for step in range(SB_HEADS + len(stages) - 1):
            for k, stage in enumerate(stages):
                if 0 <= step - k < SB_HEADS:
                    stage(step - k)

        lowest = None
        for p in range(SB_PAIRS):
            new = pv[2 * p] + pv[2 * p + 1]
            acc_scr[p] = new if diag else acc_scr[p] + new
        for h in range(SB_HEADS):
            car_scr[h] = tot[h]
            lowest = tot[h] if lowest is None else jnp.minimum(lowest, tot[h])
        return jnp.min(lowest)

    def cond(st):
        n, lowest = st
        return jnp.logical_and(n >= 0, lowest <= SB_LOG2_UNDERFLOW)

    def body(st):
        n, _ = st
        return n - 1, sweep(n, False)

    lax.while_loop(cond, body, (i - 1, sweep(i, True)))
    for p in range(SB_PAIRS):
        o_ref[0, :, V7X_LANES * p:V7X_LANES * (p + 1)] = acc_scr[p]


def _sb_prompt(q, kt, vt, *, tq):
    b, t, w = q.shape
    assert w == SB_WIDTH and t % tq == 0 and kt.shape == (b, w, t)
    n_chunks = t // tq
    kv_spec = pl.BlockSpec((1, w, t), lambda bi, i: (bi, 0, 0))
    q_spec = pl.BlockSpec((1, tq, w), lambda bi, i: (bi, i, 0))
    return pl.pallas_call(
        functools.partial(_sb_prompt_kernel, tq=tq),
        grid=(b, n_chunks),
        in_specs=[q_spec, kv_spec, kv_spec, pl.BlockSpec((tq, tq), lambda bi, i: (0, 0))],
        out_specs=q_spec,
        out_shape=jax.ShapeDtypeStruct((b, t, w), F32),
        scratch_shapes=[pltpu.VMEM((n_chunks, w, tq), BF16)] * 3
        + [pltpu.VMEM((SB_PAIRS, tq, V7X_LANES), F32), pltpu.VMEM((SB_HEADS, tq, 1), F32)],
        compiler_params=pltpu.CompilerParams(
            dimension_semantics=("arbitrary", "arbitrary"), vmem_limit_bytes=V7X_VMEM_LIMIT),
        name="sb_prompt",
    )(q, kt, vt, _suffix_ones(tq, 1))


def _sb_cached_kernel(q_ref, k_ref, v_ref, ckt_ref, cvt_ref, ud_ref, uc_ref, o_ref, *, tq, kn, n_chunks):
    first = lax.broadcasted_iota(jnp.int32, (tq, V7X_LANES), 1) < SB_HEAD_DIM
    r = lax.broadcasted_iota(jnp.int32, (tq, tq), 0)
    c = lax.broadcasted_iota(jnp.int32, (tq, tq), 1)
    causal = c < r
    even = (lax.broadcasted_iota(jnp.int32, (V7X_LANES, kn), 0) // SB_HEAD_DIM) % 2 == 0
    ud = ud_ref[...]
    uc = uc_ref[...]
    for p in range(SB_PAIRS):
        rows = slice(V7X_LANES * p, V7X_LANES * (p + 1))
        qs = _head_split(q_ref[0, :, rows], first)
        kd = k_ref[0, :, rows].astype(BF16)
        vs = _head_split(v_ref[0, :, rows], first)
        acc = jnp.zeros((tq, V7X_LANES), F32)
        carries = []
        for h in range(2):
            a, carry = _sb_weights(_dot_nt(qs[h], kd), ud, jnp.zeros((tq, 1), F32), causal)
            acc = acc + _dot(a, vs[h].astype(BF16))
            carries.append(carry)
        for n in range(n_chunks - 1, -1, -1):
            k2 = ckt_ref[0, rows, n * kn:(n + 1) * kn].astype(BF16)
            v = cvt_ref[0, rows, n * kn:(n + 1) * kn]
            vv = jnp.concatenate([jnp.where(even, v, 0.0), jnp.where(even, 0.0, v)], axis=1).astype(BF16)
            parts = []
            for h in range(2):
                a, carries[h] = _sb_weights(_dot(qs[h], k2), uc, carries[h], None)
                parts.append(a)
            acc = acc + _dot_nt(jnp.concatenate(parts, axis=1), vv)
        o_ref[0, :, rows] = acc


def _sb_cached(q, k, v, cache_kt, cache_vt, *, kn):
    b, tq, w = q.shape
    past = cache_kt.shape[2]
    assert w == SB_WIDTH and past % kn == 0 and cache_kt.shape == (b, w, past)
    new_spec = pl.BlockSpec((1, tq, w), lambda bi: (bi, 0, 0))
    cache_spec = pl.BlockSpec((1, w, past), lambda bi: (bi, 0, 0))
    return pl.pallas_call(
        functools.partial(_sb_cached_kernel, tq=tq, kn=kn, n_chunks=past // kn),
        grid=(b,),
        in_specs=[new_spec, new_spec, new_spec, cache_spec, cache_spec,
                  pl.BlockSpec((2 * tq, tq), lambda bi: (0, 0)),
                  pl.BlockSpec((2 * kn, kn), lambda bi: (0, 0))],
        out_specs=new_spec,
        out_shape=jax.ShapeDtypeStruct((b, tq, w), F32),
        compiler_params=pltpu.CompilerParams(
            dimension_semantics=("arbitrary",), vmem_limit_bytes=V7X_VMEM_LIMIT),
        name="sb_cached",
    )(q, k, v, cache_kt, cache_vt, _suffix_ones(tq, 2), _suffix_ones(kn, 2))


def _hgrn_consts(c):
    t = np.arange(c)
    sub = t // HG_BLOCK
    same = sub[:, None] == sub[None, :]
    tri = (same & (t[None, :] <= t[:, None])).astype(np.float32)
    ones = same.astype(np.float32)
    x = sub[:, None] ^ sub[None, :]
    level = np.where(x == 0, 0, np.floor(np.log2(np.maximum(x, 1))).astype(np.int64) + 1)
    level = np.where(t[None, :] <= t[:, None], level, -1).astype(np.int32)
    pre = np.block([[tri, tri], [ones, ones]])
    return jnp.asarray(pre, dtype=BF16), jnp.asarray(level)


def _prod(vecs):
    out = None
    for v in vecs:
        out = v if out is None else out * v
    return out


def _rows(vecs, width):
    one = jnp.ones((HG_BLOCK, width), F32)
    return jnp.concatenate(
        [one if v is None else jnp.broadcast_to(v, (HG_BLOCK, width)) for v in vecs], axis=0)


def _hgrn_group(read, write, lb, g, pre, lvl, states, *, c, group, v_t):
    width = HG_WIDTH
    nsub = c // HG_BLOCK
    levels = int(round(math.log2(nsub)))
    states = list(states)
    ch = [dict() for _ in range(group)]
    item = {}

    def head(h):
        return slice(HG_KEY_DIM * h, HG_KEY_DIM * (h + 1))

    def gates(gi):
        hq, hf, v = read(gi)
        f = lb + (1.0 - lb) * jax.nn.sigmoid(hf)
        ch[gi].update(hq=hq, kb=1.0 - f,
                      v_bf=v.astype(BF16), split=jnp.concatenate(_split_bf16(jnp.log(f)), axis=0))

    def prefix(gi):
        ch[gi]["sums"] = _dot(pre, ch[gi].pop("split"))

    def decays(gi):
        sums, hq, kb = (ch[gi].pop(k) for k in ("sums", "hq", "kb"))
        b16 = sums[:c]
        bend = sums[c:]
        q_dec = hq * jnp.exp(b16)
        k_inv = kb * jnp.exp(-b16)
        k_end = kb * jnp.exp(bend - b16)
        dmat = jnp.exp(bend)
        d = [dmat[HG_BLOCK * i:HG_BLOCK * i + 1, :] for i in range(nsub)]
        q_lv = [q_dec.astype(BF16)]
        k_lv = [k_inv.astype(BF16)]
        if levels >= 1:
            q_lv.append(q_lv[0])
            k_lv.append(k_end.astype(BF16))
        for l in range(2, levels + 1):
            half = 2 ** (l - 1)
            alpha, beta = [], []
            for i in range(nsub):
                mid = (i // (2 * half)) * 2 * half + half
                alpha.append(_prod(d[mid:i]) if i > mid else None)
                beta.append(_prod(d[i + 1:mid]) if i + 1 < mid else None)
            q_lv.append((q_dec * _rows(alpha, width)).astype(BF16))
            k_lv.append((k_end * _rows(beta, width)).astype(BF16))
        if nsub > 1:
            alpha_c = [_prod(d[:i]) if i > 0 else None for i in range(nsub)]
            beta_c = [_prod(d[i + 1:]) if i + 1 < nsub else None for i in range(nsub)]
            q_c = (q_dec * _rows(alpha_c, width)).astype(BF16)
            k_c = (k_end * _rows(beta_c, width)).astype(BF16)
        else:
            q_c = q_lv[0]
            k_c = k_end.astype(BF16)
        ch[gi].update(q_lv=q_lv, k_lv=k_lv, q_c=q_c, k_c=k_c, d_all=_prod(d))

    def level_scores(gi, h):
        q_lv, k_lv = ch[gi]["q_lv"], ch[gi]["k_lv"]
        item[gi, h] = [_dot_nt(q_lv[l][:, head(h)], k_lv[l][:, head(h)]) for l in range(levels + 1)]

    def combine(gi, h):
        att = jnp.zeros((c, c), F32)
        for l, scores in enumerate(item[gi, h]):
            att = jnp.where(lvl == l, scores, att)
        item[gi, h] = att.astype(BF16)

    def outputs(gi, h):
        att = item[gi, h]
        v_bf, q_c, k_c = ch[gi]["v_bf"], ch[gi]["q_c"], ch[gi]["k_c"]
        if v_t:
            vt_h = v_bf[head(h), :]
            o_intra = _dot_nt(att, vt_h)
            st_add = _dot(vt_h, k_c[:, head(h)])
        else:
            o_intra = _dot(att, v_bf[:, head(h)])
            st_add = _dot_tn(v_bf[:, head(h)], k_c[:, head(h)])
        item[gi, h] = (o_intra + _dot_nt(q_c[:, head(h)], states[h].astype(BF16)), st_add)

    def finish(gi, h):
        o, st_add = item.pop((gi, h))
        ms = jnp.mean(o * o, axis=-1, keepdims=True)
        write(gi, h, o * lax.rsqrt(ms + NORM_EPS) * g[:, head(h)])
        states[h] = states[h] * ch[gi]["d_all"][:, head(h)] + st_add

    chunk_stages = (gates, prefix, decays)
    item_stages = (level_scores, combine, outputs, finish)
    n_items = group * HG_HEADS
    for step in range(-len(chunk_stages), n_items + len(item_stages) - 1):
        for gi in range(group):
            for k, stage in enumerate(chunk_stages):
                if step == HG_HEADS * gi - len(chunk_stages) + k:
                    stage(gi)
        for k, stage in enumerate(item_stages):
            if 0 <= step - k < n_items:
                stage(*divmod(step - k, HG_HEADS))
    return states


def _hgrn_kernel(*refs, c, group, layer, has_s0, v_t, n_steps):
    if has_s0:
        (hq_ref, hf_ref, hi_ref, lbraw_ref, g_ref, pre_ref, lvl_ref, s0_ref,
         o_ref, sout_ref, st_scr) = refs
    else:
        (hq_ref, hf_ref, hi_ref, lbraw_ref, g_ref, pre_ref, lvl_ref,
         o_ref, sout_ref, st_scr) = refs
        s0_ref = None
    ci = pl.program_id(1)

    @pl.when(ci == 0)
    def _():
        for h in range(HG_HEADS):
            if has_s0:
                st_scr[h] = s0_ref[0, h].T
            else:
                st_scr[h] = jnp.zeros((HG_VAL_DIM, HG_KEY_DIM), F32)

    raw = lbraw_ref[...]
    e = jnp.exp(raw - jnp.max(raw, axis=0, keepdims=True))
    lb = jnp.sum(e[:layer + 1], axis=0, keepdims=True) / jnp.sum(e, axis=0, keepdims=True)

    pre = pre_ref[...]
    lvl = lvl_ref[...]
    g = g_ref[...]
    def read(gi):
        rs = slice(gi * c, (gi + 1) * c)
        return hq_ref[0, rs, :], hf_ref[0, rs, :], hi_ref[0, :, rs] if v_t else hi_ref[0, rs, :]

    def write(gi, h, o):
        o_ref[0, gi * c:(gi + 1) * c, HG_VAL_DIM * h:HG_VAL_DIM * (h + 1)] = o

    states = _hgrn_group(read, write, lb, g, pre, lvl, [st_scr[h] for h in range(HG_HEADS)],
                         c=c, group=group, v_t=v_t)
    for h in range(HG_HEADS):
        st_scr[h] = states[h]

    @pl.when(ci == n_steps - 1)
    def _():
        for h in range(HG_HEADS):
            sout_ref[0, h] = st_scr[h].T


def _hgrn(hq, hf, hi, lb_raw, norm_g, s0, *, layer, c, group, v_t):
    b, t, w = hq.shape
    span = c * group
    assert w == HG_WIDTH and t % span == 0 and c % HG_BLOCK == 0
    n_steps = t // span
    pre, level = _hgrn_consts(c)
    tok = pl.BlockSpec((1, span, w), lambda bi, ci: (bi, ci, 0))
    tok_t = pl.BlockSpec((1, w, span), lambda bi, ci: (bi, 0, ci))
    const2 = lambda shape: pl.BlockSpec(shape, lambda bi, ci: (0, 0))
    st_spec = pl.BlockSpec((1, HG_HEADS, HG_KEY_DIM, HG_VAL_DIM), lambda bi, ci: (bi, 0, 0, 0))
    in_specs = [tok, tok, tok_t if v_t else tok, const2(lb_raw.shape), const2((1, w)),
                const2(pre.shape), const2(level.shape)]
    args = [hq, hf, hi, lb_raw, norm_g.reshape(1, w), pre, level]
    if s0 is not None:
        in_specs.append(st_spec)
        args.append(s0)
    return pl.pallas_call(
        functools.partial(_hgrn_kernel, c=c, group=group, layer=layer, has_s0=s0 is not None, v_t=v_t,
                          n_steps=n_steps),
        grid=(b, n_steps),
        in_specs=in_specs,
        out_specs=(tok, st_spec),
        out_shape=(jax.ShapeDtypeStruct((b, t, w), F32),
                   jax.ShapeDtypeStruct((b, HG_HEADS, HG_KEY_DIM, HG_VAL_DIM), F32)),
        scratch_shapes=[pltpu.VMEM((HG_HEADS, HG_VAL_DIM, HG_KEY_DIM), F32)],
        compiler_params=pltpu.CompilerParams(
            dimension_semantics=("arbitrary", "arbitrary"), vmem_limit_bytes=V7X_VMEM_LIMIT),
        name="hgrn2",
    )(*args)


def _out_kernel(x_ref, osb_ref, ohg_ref, gpre_ref, wg_ref, bg_ref, wa_ref, wb_ref, wo_ref, gp_ref, y_ref):
    d_model = x_ref.shape[-1]
    x = x_ref[...]
    xn = _rms_norm_bf16(x, gpre_ref[...])
    off = np.cumsum([0, SB_WIDTH, HG_WIDTH, d_model, d_model])
    z_a, hz, ga, gb = (_dot(xn, wg_ref[:, off[i]:off[i + 1]]) for i in range(4))
    u_a = _dot((osb_ref[...] * jax.nn.silu(z_a)).astype(BF16), wa_ref[...])
    u_b = _dot((ohg_ref[...] * jax.nn.silu(hz)).astype(BF16), wb_ref[...])
    bg = bg_ref[...]
    merged = (jax.nn.sigmoid(ga + bg[:, :d_model]) * u_a + jax.nn.sigmoid(gb + bg[:, d_model:]) * u_b)
    m = _dot(merged.astype(BF16), wo_ref[...])
    ms = jnp.mean(m * m, axis=-1, keepdims=True)
    y_ref[...] = x + m * lax.rsqrt(ms + NORM_EPS) * gp_ref[...]


def _out_stage(x2d, o_sb, o_hg, g_pre, w_in_bf16, b_gate, wa, wb, wo, g_post, *, tm):
    n, d = x2d.shape
    assert n % tm == 0
    _, offsets = _in_widths(d)
    w_gate = jnp.concatenate([w_in_bf16[:, offsets[i]:offsets[i + 1]] for i in IN_GATE_GROUPS], axis=1)
    row = lambda w: pl.BlockSpec((tm, w), lambda i: (i, 0))
    const = lambda shape: pl.BlockSpec(shape, lambda i: (0, 0))
    return pl.pallas_call(
        _out_kernel,
        grid=(n // tm,),
        in_specs=[row(d), row(SB_WIDTH), row(HG_WIDTH), const((1, d)), const(w_gate.shape),
                  const((1, 2 * d)), const(wa.shape), const(wb.shape), const(wo.shape), const((1, d))],
        out_specs=row(d),
        out_shape=jax.ShapeDtypeStruct((n, d), F32),
        compiler_params=pltpu.CompilerParams(
            dimension_semantics=("arbitrary",), vmem_limit_bytes=V7X_VMEM_LIMIT),
        name="out_stage",
    )(x2d, o_sb, o_hg, g_pre.reshape(1, d), w_gate, b_gate.reshape(1, 2 * d), wa, wb, wo, g_post.reshape(1, d))


def _largest_tile(n, cap):
    t = cap
    while n % t:
        t //= 2
    return t


def _trunk_layer(x, cache_k, cache_v, s0, layer, g_pre, w_in, b_gate, lb_raw, hg_norm_g,
                 w_a, w_b, w_out, g_post):
    b, t, d = x.shape
    n = b * t
    heads = (SB_HEADS, SB_HEAD_DIM)
    if cache_k is None:
        q, kt, vt, hq, hf, hi = _in_proj(x, g_pre, w_in, tm=_largest_tile(t, 512), t_groups=(IN_K, IN_V, IN_HG_V))
        o_sb = _sb_prompt(q, kt, vt, tq=_largest_tile(t, 256))
        k_out, v_out = (a.reshape(b, *heads, t).transpose(0, 3, 1, 2) for a in (kt, vt))
    else:
        tm = _largest_tile(n, 256)
        outs = _in_proj(x.reshape(1, n, d), g_pre, w_in, tm=tm, t_groups=())
        q, k, v, hq, hf, hi = (a.reshape(b, t, a.shape[-1]) for a in outs)
        past = cache_k.shape[1]
        cache_kt, cache_vt = (a.transpose(0, 2, 3, 1).reshape(b, SB_WIDTH, past) for a in (cache_k, cache_v))
        o_sb = _sb_cached(q, k, v, cache_kt, cache_vt, kn=_largest_tile(past, 256))
        k_out, v_out = k.reshape(b, t, *heads), v.reshape(b, t, *heads)
    c = _largest_tile(t, 128)
    o_hg, s_new = _hgrn(hq, hf, hi, lb_raw, hg_norm_g.reshape(-1), s0, layer=layer, c=c,
                        group=_largest_tile(t // c, 4), v_t=cache_k is None)
    flat = lambda a: a.reshape(n, a.shape[-1])
    y = _out_stage(flat(x), flat(o_sb), flat(o_hg), g_pre, w_in, b_gate, w_a, w_b, w_out, g_post,
                   tm=_largest_tile(n, 512))
    return y.reshape(b, t, d), k_out, v_out, s_new


def kernel(x_prompt, x_sample, cache_sb_k, cache_sb_v, state_hgrn, g_pre, w_in, b_gate, hg_lb_raw,
           hg_norm_g, w_branch_a, w_branch_b, w_out, g_post):
    depth = w_in.shape[0]
    y_p, y_s = x_prompt, x_sample
    outs = [[] for _ in range(6)]
    for l in range(depth):
        weights = (g_pre[l], w_in[l].astype(BF16), b_gate[l], hg_lb_raw, hg_norm_g[l],
                   w_branch_a[l].astype(BF16), w_branch_b[l].astype(BF16), w_out[l].astype(BF16), g_post[l])
        y_p, kp, vp, sp = _trunk_layer(y_p, None, None, None, l, *weights)
        y_s, ks, vs, ss = _trunk_layer(y_s, cache_sb_k[l], cache_sb_v[l], state_hgrn[l], l, *weights)
        for lst, a in zip(outs, (kp, vp, sp, ks, vs, ss)):
            lst.append(a)
    return (y_p, y_s) + tuple(jnp.stack(lst) for lst in outs)
```

```python
import functools
import math

import numpy as np
import jax
import jax.numpy as jnp
from jax import lax
from jax.experimental import pallas as pl
from jax.experimental.pallas import tpu as pltpu

SB_HEADS = 8
SB_HEAD_DIM = 64
SB_WIDTH = SB_HEADS * SB_HEAD_DIM
HG_HEADS = 4
HG_KEY_DIM = 128
HG_VAL_DIM = 128
HG_WIDTH = HG_HEADS * HG_KEY_DIM
HG_BLOCK = 16
NORM_EPS = 1e-6

V7X_LANES = 128
V7X_VMEM_LIMIT = 56 * 1024 * 1024
SB_PAIRS = SB_WIDTH // V7X_LANES

SB_LOG2_UNDERFLOW = 150.5
LOG2_E = math.log2(math.e)

F32 = jnp.float32
BF16 = jnp.bfloat16

_NT = (((1,), (1,)), ((), ()))
_TN = (((0,), (0,)), ((), ()))


def _dot(a, b):
    return jnp.dot(a, b, preferred_element_type=F32)


def _dot_nt(a, b):
    return lax.dot_general(a, b, _NT, preferred_element_type=F32)


def _dot_tn(a, b):
    return lax.dot_general(a, b, _TN, preferred_element_type=F32)


def _split_bf16(x):
    hi = x.astype(BF16)
    lo = (x - hi.astype(F32)).astype(BF16)
    return hi, lo


IN_Q, IN_K, IN_V, IN_ZA, IN_HG_Q, IN_HG_F, IN_HG_V, IN_HG_Z, IN_GATE_A, IN_GATE_B = range(10)
IN_MIXER_GROUPS = (IN_Q, IN_K, IN_V, IN_HG_Q, IN_HG_F, IN_HG_V)
IN_GATE_GROUPS = (IN_ZA, IN_HG_Z, IN_GATE_A, IN_GATE_B)


def _in_widths(d_model):
    widths = (SB_WIDTH,) * 4 + (HG_WIDTH,) * 4 + (d_model, d_model)
    return widths, np.concatenate([[0], np.cumsum(widths)])


def _rms_norm_bf16(x, g):
    ms = jnp.mean(x * x, axis=-1, keepdims=True)
    return (x * lax.rsqrt(ms + NORM_EPS) * g).astype(BF16)


def _in_proj_kernel(*refs, groups, t_groups, widths, offsets, q_scale):
    if t_groups:
        x_ref, g_ref, w_ref, wt_ref = refs[:4]
        out_refs = refs[4:]
    else:
        x_ref, g_ref, w_ref = refs[:3]
        out_refs = refs[3:]
    xn = _rms_norm_bf16(x_ref[0], g_ref[...])
    if t_groups:
        ht = _dot_nt(wt_ref[...], xn)
        off = 0
        for idx in t_groups:
            out_refs[groups.index(idx)][0] = ht[off:off + widths[idx]]
            off += widths[idx]
    for ref, idx in zip(out_refs, groups):
        if idx not in t_groups:
            h = _dot(xn, w_ref[:, offsets[idx]:offsets[idx + 1]])
            if idx == IN_Q:
                h = h * q_scale
            ref[0] = h.astype(ref.dtype)


def _in_proj(x, g_pre, w_in_bf16, *, tm, t_groups):
    b, t, d = x.shape
    widths, offsets = _in_widths(d)
    offsets = tuple(int(o) for o in offsets)
    assert offsets[-1] == w_in_bf16.shape[1] and t % tm == 0
    out_shape, out_specs = [], []
    for idx in IN_MIXER_GROUPS:
        w = widths[idx]
        dt = BF16 if idx == IN_Q else F32
        if idx in t_groups:
            out_shape.append(jax.ShapeDtypeStruct((b, w, t), dt))
            out_specs.append(pl.BlockSpec((1, w, tm), lambda bi, ti: (bi, 0, ti)))
        else:
            out_shape.append(jax.ShapeDtypeStruct((b, t, w), dt))
            out_specs.append(pl.BlockSpec((1, tm, w), lambda bi, ti: (bi, ti, 0)))
    in_specs = [
        pl.BlockSpec((1, tm, d), lambda bi, ti: (bi, ti, 0)),
        pl.BlockSpec((1, d), lambda bi, ti: (0, 0)),
        pl.BlockSpec(w_in_bf16.shape, lambda bi, ti: (0, 0)),
    ]
    args = [x, g_pre.reshape(1, d), w_in_bf16]
    if t_groups:
        w_t = jnp.concatenate([w_in_bf16[:, offsets[i]:offsets[i + 1]] for i in t_groups], axis=1).T
        in_specs.append(pl.BlockSpec(w_t.shape, lambda bi, ti: (0, 0)))
        args.append(w_t)
    return pl.pallas_call(
        functools.partial(_in_proj_kernel, groups=IN_MIXER_GROUPS, t_groups=t_groups, widths=widths,
                          offsets=offsets, q_scale=SB_HEAD_DIM ** -0.5 * LOG2_E),
        grid=(b, t // tm),
        in_specs=in_specs,
        out_specs=tuple(out_specs),
        out_shape=tuple(out_shape),
        compiler_params=pltpu.CompilerParams(
            dimension_semantics=("arbitrary", "arbitrary"), vmem_limit_bytes=V7X_VMEM_LIMIT),
        name="in_proj",
    )(*args)


def _suffix_ones(n, copies):
    r = np.arange(n)
    u = (r[:, None] >= r[None, :]).astype(np.float32)
    return jnp.asarray(np.concatenate([u] * copies, axis=0), dtype=BF16)


def _sb_weights(z, u2, carry, mask):
    sp = jnp.maximum(z, 0.0) + jnp.log(1.0 + jnp.exp2(-jnp.abs(z))) * LOG2_E
    if mask is not None:
        sp = jnp.where(mask, sp, 0.0)
    hi, lo = _split_bf16(sp)
    tot = _dot(jnp.concatenate([hi, lo], axis=1), u2) + carry
    a = jnp.exp2(z - tot)
    if mask is not None:
        a = jnp.where(mask, a, 0.0)
    return a.astype(BF16), tot[:, 0:1]


def _head_split(x2, first):
    zero = jnp.zeros_like(x2)
    return jnp.where(first, x2, zero), jnp.where(first, zero, x2)


def _sb_prompt_kernel(q_ref, kt_ref, vt_ref, u_ref, o_ref, kt_scr, ve_scr, vo_scr, acc_scr, car_scr, *, tq):
    i = pl.program_id(1)
    n_chunks = kt_scr.shape[0]

    @pl.when(i == 0)
    def _():
        row = lax.broadcasted_iota(jnp.int32, (SB_WIDTH, tq), 0)
        even = (row // SB_HEAD_DIM) % 2 == 0
        for n in range(n_chunks):
            kt_scr[n] = kt_ref[0, :, n * tq:(n + 1) * tq].astype(BF16)
            v = vt_ref[0, :, n * tq:(n + 1) * tq]
            ve_scr[n] = jnp.where(even, v, 0.0).astype(BF16)
            vo_scr[n] = jnp.where(even, 0.0, v).astype(BF16)

    late = tq - tq // 4

    def first(rows):
        return lax.broadcasted_iota(jnp.int32, (rows, V7X_LANES), 1) < SB_HEAD_DIM

    def sweep(jobs):
        qs, z, sp16, tot, wts, pv = {}, {}, {}, {}, {}, {}
        masks = []
        for j, (n, (r0, r1), (k0, k1), offset, fresh) in enumerate(jobs):
            if offset is None:
                masks.append(None)
            else:
                r = lax.broadcasted_iota(jnp.int32, (r1 - r0, k1 - k0), 0)
                c = lax.broadcasted_iota(jnp.int32, (r1 - r0, k1 - k0), 1)
                masks.append(c < r + offset)
            for p in range(SB_PAIRS):
                q2 = q_ref[0, r0:r1, V7X_LANES * p:V7X_LANES * (p + 1)]
                qs[j, 2 * p], qs[j, 2 * p + 1] = _head_split(q2, first(r1 - r0))

        def pair_rows(h):
            return slice(V7X_LANES * (h // 2), V7X_LANES * (h // 2 + 1))

        def scores(j, h):
            n, _, (k0, k1), _, _ = jobs[j]
            z[j, h] = _dot(qs.pop((j, h)), kt_scr[n, pair_rows(h), k0:k1])

        def softplus(j, h):
            zz = z[j, h]
            sp = jnp.maximum(zz, 0.0) + jnp.log(1.0 + jnp.exp2(-jnp.abs(zz))) * LOG2_E
            if masks[j] is not None:
                sp = jnp.where(masks[j], sp, 0.0)
            sp16[j, h] = sp.astype(BF16)

        def suffix_sums(j, h):
            _, (r0, r1), (k0, k1), _, fresh = jobs[j]
            carry = jnp.zeros((r1 - r0, 1), F32) if fresh else car_scr[h, r0:r1]
            tot[j, h] = _dot(sp16.pop((j, h)), u_ref[:k1 - k0, :k1 - k0]) + carry

        def weights(j, h):
            a = jnp.exp2(z.pop((j, h)) - tot[j, h])
            if masks[j] is not None:
                a = jnp.where(masks[j], a, 0.0)
            wts[j, h] = a.astype(BF16)
            tot[j, h] = tot[j, h][:, 0:1]

        def weighted_values(j, h):
            n, _, (k0, k1), _, _ = jobs[j]
            v_scr = vo_scr if h % 2 else ve_scr
            pv[j, h] = _dot_nt(wts.pop((j, h)), v_scr[n, pair_rows(h), k0:k1])

        stages = (scores, softplus, suffix_sums, weights, weighted_values)
        n_items = len(jobs) * SB_HEADS
        for step in range(n_items + len(stages) - 1):
            for k, stage in enumerate(stages):
                if 0 <= step - k < n_items:
                    stage(*divmod(step - k, SB_HEADS))

        for j, (n, (r0, r1), _, _, fresh) in enumerate(jobs):
            for p in range(SB_PAIRS):
                new = pv[j, 2 * p] + pv[j, 2 * p + 1]
                acc_scr[p, r0:r1] = new if fresh else acc_scr[p, r0:r1] + new
            for h in range(SB_HEADS):
                car_scr[h, r0:r1] = tot[j, h]

    def lowest_total(r0, r1):
        low = car_scr[0, r0:r1]
        for h in range(1, SB_HEADS):
            low = jnp.minimum(low, car_scr[h, r0:r1])
        return jnp.min(low)

    sweep([(i, (0, tq), (0, tq), 0, True)])

    def cond(st):
        n, low_early, low_late = st
        return jnp.logical_and(n >= 0, jnp.minimum(low_early, low_late) <= SB_LOG2_UNDERFLOW)

    def older(n, r0, r1, low):
        def run():
            sweep([(n, (r0, r1), (0, tq), None, False)])
            return lowest_total(r0, r1)
        return lax.cond(low <= SB_LOG2_UNDERFLOW, run, lambda: low)

    def body(st):
        n, low_early, low_late = st
        return n - 1, older(n, 0, late, low_early), older(n, late, tq, low_late)

    lax.while_loop(cond, body, (i - 1, lowest_total(0, late), lowest_total(late, tq)))
    for p in range(SB_PAIRS):
        o_ref[0, :, V7X_LANES * p:V7X_LANES * (p + 1)] = acc_scr[p]


def _sb_prompt(q, kt, vt, *, tq):
    b, t, w = q.shape
    assert w == SB_WIDTH and t % tq == 0 and kt.shape == (b, w, t)
    n_chunks = t // tq
    kv_spec = pl.BlockSpec((1, w, t), lambda bi, i: (bi, 0, 0))
    q_spec = pl.BlockSpec((1, tq, w), lambda bi, i: (bi, i, 0))
    return pl.pallas_call(
        functools.partial(_sb_prompt_kernel, tq=tq),
        grid=(b, n_chunks),
        in_specs=[q_spec, kv_spec, kv_spec, pl.BlockSpec((tq, tq), lambda bi, i: (0, 0))],
        out_specs=q_spec,
        out_shape=jax.ShapeDtypeStruct((b, t, w), F32),
        scratch_shapes=[pltpu.VMEM((n_chunks, w, tq), BF16)] * 3
        + [pltpu.VMEM((SB_PAIRS, tq, V7X_LANES), F32), pltpu.VMEM((SB_HEADS, tq, 1), F32)],
        compiler_params=pltpu.CompilerParams(
            dimension_semantics=("arbitrary", "arbitrary"), vmem_limit_bytes=V7X_VMEM_LIMIT),
        name="sb_prompt",
    )(q, kt, vt, _suffix_ones(tq, 1))


def _sb_cached_kernel(q_ref, k_ref, v_ref, ckt_ref, cvt_ref, ud_ref, uc_ref, o_ref, *, tq, kn, n_chunks):
    first = lax.broadcasted_iota(jnp.int32, (tq, V7X_LANES), 1) < SB_HEAD_DIM
    r = lax.broadcasted_iota(jnp.int32, (tq, tq), 0)
    c = lax.broadcasted_iota(jnp.int32, (tq, tq), 1)
    causal = c < r
    even = (lax.broadcasted_iota(jnp.int32, (V7X_LANES, kn), 0) // SB_HEAD_DIM) % 2 == 0
    ud = ud_ref[...]
    uc = uc_ref[...]
    for p in range(SB_PAIRS):
        rows = slice(V7X_LANES * p, V7X_LANES * (p + 1))
        qs = _head_split(q_ref[0, :, rows], first)
        kd = k_ref[0, :, rows].astype(BF16)
        vs = _head_split(v_ref[0, :, rows], first)
        acc = jnp.zeros((tq, V7X_LANES), F32)
        carries = []
        for h in range(2):
            a, carry = _sb_weights(_dot_nt(qs[h], kd), ud, jnp.zeros((tq, 1), F32), causal)
            acc = acc + _dot(a, vs[h].astype(BF16))
            carries.append(carry)
        for n in range(n_chunks - 1, -1, -1):
            k2 = ckt_ref[0, rows, n * kn:(n + 1) * kn].astype(BF16)
            v = cvt_ref[0, rows, n * kn:(n + 1) * kn]
            vv = jnp.concatenate([jnp.where(even, v, 0.0), jnp.where(even, 0.0, v)], axis=1).astype(BF16)
            parts = []
            for h in range(2):
                a, carries[h] = _sb_weights(_dot(qs[h], k2), uc, carries[h], None)
                parts.append(a)
            acc = acc + _dot_nt(jnp.concatenate(parts, axis=1), vv)
        o_ref[0, :, rows] = acc


def _sb_cached(q, k, v, cache_kt, cache_vt, *, kn):
    b, tq, w = q.shape
    past = cache_kt.shape[2]
    assert w == SB_WIDTH and past % kn == 0 and cache_kt.shape == (b, w, past)
    new_spec = pl.BlockSpec((1, tq, w), lambda bi: (bi, 0, 0))
    cache_spec = pl.BlockSpec((1, w, past), lambda bi: (bi, 0, 0))
    return pl.pallas_call(
        functools.partial(_sb_cached_kernel, tq=tq, kn=kn, n_chunks=past // kn),
        grid=(b,),
        in_specs=[new_spec, new_spec, new_spec, cache_spec, cache_spec,
                  pl.BlockSpec((2 * tq, tq), lambda bi: (0, 0)),
                  pl.BlockSpec((2 * kn, kn), lambda bi: (0, 0))],
        out_specs=new_spec,
        out_shape=jax.ShapeDtypeStruct((b, tq, w), F32),
        compiler_params=pltpu.CompilerParams(
            dimension_semantics=("arbitrary",), vmem_limit_bytes=V7X_VMEM_LIMIT),
        name="sb_cached",
    )(q, k, v, cache_kt, cache_vt, _suffix_ones(tq, 2), _suffix_ones(kn, 2))


def _hgrn_consts(c):
    t = np.arange(c)
    sub = t // HG_BLOCK
    same = sub[:, None] == sub[None, :]
    tri = (same & (t[None, :] <= t[:, None])).astype(np.float32)
    ones = same.astype(np.float32)
    x = sub[:, None] ^ sub[None, :]
    level = np.where(x == 0, 0, np.floor(np.log2(np.maximum(x, 1))).astype(np.int64) + 1)
    level = np.where(t[None, :] <= t[:, None], level, -1).astype(np.int32)
    pre = np.block([[tri, tri], [ones, ones]])
    return jnp.asarray(pre, dtype=BF16), jnp.asarray(level)


def _prod(vecs):
    out = None
    for v in vecs:
        out = v if out is None else out * v
    return out


def _rows(vecs, width):
    one = jnp.ones((HG_BLOCK, width), F32)
    return jnp.concatenate(
        [one if v is None else jnp.broadcast_to(v, (HG_BLOCK, width)) for v in vecs], axis=0)


def _hgrn_group(read, write, lb, g, pre, lvl, states, *, c, group, v_t):
    width = HG_WIDTH
    nsub = c // HG_BLOCK
    levels = int(round(math.log2(nsub)))
    states = list(states)
    ch = [dict() for _ in range(group)]
    item = {}

    def head(h):
        return slice(HG_KEY_DIM * h, HG_KEY_DIM * (h + 1))

    def gates(gi):
        hq, hf, v = read(gi)
        f = lb + (1.0 - lb) * jax.nn.sigmoid(hf)
        ch[gi].update(hq=hq, kb=1.0 - f,
                      v_bf=v.astype(BF16), split=jnp.concatenate(_split_bf16(jnp.log(f)), axis=0))

    def prefix(gi):
        ch[gi]["sums"] = _dot(pre, ch[gi].pop("split"))

    def decays(gi):
        sums, hq, kb = (ch[gi].pop(k) for k in ("sums", "hq", "kb"))
        b16 = sums[:c]
        bend = sums[c:]
        q_dec = hq * jnp.exp(b16)
        k_inv = kb * jnp.exp(-b16)
        k_end = kb * jnp.exp(bend - b16)
        dmat = jnp.exp(bend)
        d = [dmat[HG_BLOCK * i:HG_BLOCK * i + 1, :] for i in range(nsub)]
        q_lv = [q_dec.astype(BF16)]
        k_lv = [k_inv.astype(BF16)]
        if levels >= 1:
            q_lv.append(q_lv[0])
            k_lv.append(k_end.astype(BF16))
        for l in range(2, levels + 1):
            half = 2 ** (l - 1)
            alpha, beta = [], []
            for i in range(nsub):
                mid = (i // (2 * half)) * 2 * half + half
                alpha.append(_prod(d[mid:i]) if i > mid else None)
                beta.append(_prod(d[i + 1:mid]) if i + 1 < mid else None)
            q_lv.append((q_dec * _rows(alpha, width)).astype(BF16))
            k_lv.append((k_end * _rows(beta, width)).astype(BF16))
        if nsub > 1:
            alpha_c = [_prod(d[:i]) if i > 0 else None for i in range(nsub)]
            beta_c = [_prod(d[i + 1:]) if i + 1 < nsub else None for i in range(nsub)]
            q_c = (q_dec * _rows(alpha_c, width)).astype(BF16)
            k_c = (k_end * _rows(beta_c, width)).astype(BF16)
        else:
            q_c = q_lv[0]
            k_c = k_end.astype(BF16)
        ch[gi].update(q_lv=q_lv, k_lv=k_lv, q_c=q_c, k_c=k_c, d_all=_prod(d))

    def level_scores(gi, h):
        q_lv, k_lv = ch[gi]["q_lv"], ch[gi]["k_lv"]
        item[gi, h] = [_dot_nt(q_lv[l][:, head(h)], k_lv[l][:, head(h)]) for l in range(levels + 1)]

    def combine(gi, h):
        att = jnp.zeros((c, c), F32)
        for l, scores in enumerate(item[gi, h]):
            att = jnp.where(lvl == l, scores, att)
        item[gi, h] = att.astype(BF16)

    def outputs(gi, h):
        att = item[gi, h]
        v_bf, q_c, k_c = ch[gi]["v_bf"], ch[gi]["q_c"], ch[gi]["k_c"]
        if v_t:
            vt_h = v_bf[head(h), :]
            o_intra = _dot_nt(att, vt_h)
            st_add = _dot(vt_h, k_c[:, head(h)])
        else:
            o_intra = _dot(att, v_bf[:, head(h)])
            st_add = _dot_tn(v_bf[:, head(h)], k_c[:, head(h)])
        item[gi, h] = (o_intra + _dot_nt(q_c[:, head(h)], states[h].astype(BF16)), st_add)

    def finish(gi, h):
        o, st_add = item.pop((gi, h))
        ms = jnp.mean(o * o, axis=-1, keepdims=True)
        write(gi, h, o * lax.rsqrt(ms + NORM_EPS) * g[:, head(h)])
        states[h] = states[h] * ch[gi]["d_all"][:, head(h)] + st_add

    chunk_stages = (gates, prefix, decays)
    item_stages = (level_scores, combine, outputs, finish)
    n_items = group * HG_HEADS
    for step in range(-len(chunk_stages), n_items + len(item_stages) - 1):
        for gi in range(group):
            for k, stage in enumerate(chunk_stages):
                if step == HG_HEADS * gi - len(chunk_stages) + k:
                    stage(gi)
        for k, stage in enumerate(item_stages):
            if 0 <= step - k < n_items:
                stage(*divmod(step - k, HG_HEADS))
    return states


def _hgrn_kernel(*refs, c, group, layer, has_s0, v_t, n_steps):
    if has_s0:
        (hq_ref, hf_ref, hi_ref, lbraw_ref, g_ref, pre_ref, lvl_ref, s0_ref,
         o_ref, sout_ref, st_scr) = refs
    else:
        (hq_ref, hf_ref, hi_ref, lbraw_ref, g_ref, pre_ref, lvl_ref,
         o_ref, sout_ref, st_scr) = refs
        s0_ref = None
    ci = pl.program_id(1)

    @pl.when(ci == 0)
    def _():
        for h in range(HG_HEADS):
            if has_s0:
                st_scr[h] = s0_ref[0, h].T
            else:
                st_scr[h] = jnp.zeros((HG_VAL_DIM, HG_KEY_DIM), F32)

    raw = lbraw_ref[...]
    e = jnp.exp(raw - jnp.max(raw, axis=0, keepdims=True))
    lb = jnp.sum(e[:layer + 1], axis=0, keepdims=True) / jnp.sum(e, axis=0, keepdims=True)

    pre = pre_ref[...]
    lvl = lvl_ref[...]
    g = g_ref[...]
    def read(gi):
        rs = slice(gi * c, (gi + 1) * c)
        return hq_ref[0, rs, :], hf_ref[0, rs, :], hi_ref[0, :, rs] if v_t else hi_ref[0, rs, :]

    def write(gi, h, o):
        o_ref[0, gi * c:(gi + 1) * c, HG_VAL_DIM * h:HG_VAL_DIM * (h + 1)] = o

    states = _hgrn_group(read, write, lb, g, pre, lvl, [st_scr[h] for h in range(HG_HEADS)],
                         c=c, group=group, v_t=v_t)
    for h in range(HG_HEADS):
        st_scr[h] = states[h]

    @pl.when(ci == n_steps - 1)
    def _():
        for h in range(HG_HEADS):
            sout_ref[0, h] = st_scr[h].T


def _hgrn(hq, hf, hi, lb_raw, norm_g, s0, *, layer, c, group, v_t):
    b, t, w = hq.shape
    span = c * group
    assert w == HG_WIDTH and t % span == 0 and c % HG_BLOCK == 0
    n_steps = t // span
    pre, level = _hgrn_consts(c)
    tok = pl.BlockSpec((1, span, w), lambda bi, ci: (bi, ci, 0))
    tok_t = pl.BlockSpec((1, w, span), lambda bi, ci: (bi, 0, ci))
    const2 = lambda shape: pl.BlockSpec(shape, lambda bi, ci: (0, 0))
    st_spec = pl.BlockSpec((1, HG_HEADS, HG_KEY_DIM, HG_VAL_DIM), lambda bi, ci: (bi, 0, 0, 0))
    in_specs = [tok, tok, tok_t if v_t else tok, const2(lb_raw.shape), const2((1, w)),
                const2(pre.shape), const2(level.shape)]
    args = [hq, hf, hi, lb_raw, norm_g.reshape(1, w), pre, level]
    if s0 is not None:
        in_specs.append(st_spec)
        args.append(s0)
    return pl.pallas_call(
        functools.partial(_hgrn_kernel, c=c, group=group, layer=layer, has_s0=s0 is not None, v_t=v_t,
                          n_steps=n_steps),
        grid=(b, n_steps),
        in_specs=in_specs,
        out_specs=(tok, st_spec),
        out_shape=(jax.ShapeDtypeStruct((b, t, w), F32),
                   jax.ShapeDtypeStruct((b, HG_HEADS, HG_KEY_DIM, HG_VAL_DIM), F32)),
        scratch_shapes=[pltpu.VMEM((HG_HEADS, HG_VAL_DIM, HG_KEY_DIM), F32)],
        compiler_params=pltpu.CompilerParams(
            dimension_semantics=("arbitrary", "arbitrary"), vmem_limit_bytes=V7X_VMEM_LIMIT),
        name="hgrn2",
    )(*args)


def _out_kernel(x_ref, osb_ref, ohg_ref, gpre_ref, wg_ref, bg_ref, wa_ref, wb_ref, wo_ref, gp_ref, y_ref):
    d_model = x_ref.shape[-1]
    x = x_ref[...]
    xn = _rms_norm_bf16(x, gpre_ref[...])
    off = np.cumsum([0, SB_WIDTH, HG_WIDTH, d_model, d_model])
    z_a, hz, ga, gb = (_dot(xn, wg_ref[:, off[i]:off[i + 1]]) for i in range(4))
    u_a = _dot((osb_ref[...] * jax.nn.silu(z_a)).astype(BF16), wa_ref[...])
    u_b = _dot((ohg_ref[...] * jax.nn.silu(hz)).astype(BF16), wb_ref[...])
    bg = bg_ref[...]
    merged = (jax.nn.sigmoid(ga + bg[:, :d_model]) * u_a + jax.nn.sigmoid(gb + bg[:, d_model:]) * u_b)
    m = _dot(merged.astype(BF16), wo_ref[...])
    ms = jnp.mean(m * m, axis=-1, keepdims=True)
    y_ref[...] = x + m * lax.rsqrt(ms + NORM_EPS) * gp_ref[...]


def _out_stage(x2d, o_sb, o_hg, g_pre, w_in_bf16, b_gate, wa, wb, wo, g_post, *, tm):
    n, d = x2d.shape
    assert n % tm == 0
    _, offsets = _in_widths(d)
    w_gate = jnp.concatenate([w_in_bf16[:, offsets[i]:offsets[i + 1]] for i in IN_GATE_GROUPS], axis=1)
    row = lambda w: pl.BlockSpec((tm, w), lambda i: (i, 0))
    const = lambda shape: pl.BlockSpec(shape, lambda i: (0, 0))
    return pl.pallas_call(
        _out_kernel,
        grid=(n // tm,),
        in_specs=[row(d), row(SB_WIDTH), row(HG_WIDTH), const((1, d)), const(w_gate.shape),
                  const((1, 2 * d)), const(wa.shape), const(wb.shape), const(wo.shape), const((1, d))],
        out_specs=row(d),
        out_shape=jax.ShapeDtypeStruct((n, d), F32),
        compiler_params=pltpu.CompilerParams(
            dimension_semantics=("arbitrary",), vmem_limit_bytes=V7X_VMEM_LIMIT),
        name="out_stage",
    )(x2d, o_sb, o_hg, g_pre.reshape(1, d), w_gate, b_gate.reshape(1, 2 * d), wa, wb, wo, g_post.reshape(1, d))


def _largest_tile(n, cap):
    t = cap
    while n % t:
        t //= 2
    return t


def _trunk_layer(x, cache_k, cache_v, s0, layer, g_pre, w_in, b_gate, lb_raw, hg_norm_g,
                 w_a, w_b, w_out, g_post):
    b, t, d = x.shape
    n = b * t
    heads = (SB_HEADS, SB_HEAD_DIM)
    if cache_k is None:
        q, kt, vt, hq, hf, hi = _in_proj(x, g_pre, w_in, tm=_largest_tile(t, 512), t_groups=(IN_K, IN_V, IN_HG_V))
        o_sb = _sb_prompt(q, kt, vt, tq=_largest_tile(t, 256))
        k_out, v_out = (a.reshape(b, *heads, t).transpose(0, 3, 1, 2) for a in (kt, vt))
    else:
        tm = _largest_tile(n, 256)
        outs = _in_proj(x.reshape(1, n, d), g_pre, w_in, tm=tm, t_groups=())
        q, k, v, hq, hf, hi = (a.reshape(b, t, a.shape[-1]) for a in outs)
        past = cache_k.shape[1]
        cache_kt, cache_vt = (a.transpose(0, 2, 3, 1).reshape(b, SB_WIDTH, past) for a in (cache_k, cache_v))
        o_sb = _sb_cached(q, k, v, cache_kt, cache_vt, kn=_largest_tile(past, 256))
        k_out, v_out = k.reshape(b, t, *heads), v.reshape(b, t, *heads)
    c = _largest_tile(t, 128)
    o_hg, s_new = _hgrn(hq, hf, hi, lb_raw, hg_norm_g.reshape(-1), s0, layer=layer, c=c,
                        group=_largest_tile(t // c, 4), v_t=cache_k is None)
    flat = lambda a: a.reshape(n, a.shape[-1])
    y = _out_stage(flat(x), flat(o_sb), flat(o_hg), g_pre, w_in, b_gate, w_a, w_b, w_out, g_post,
                   tm=_largest_tile(n, 256))
    return y.reshape(b, t, d), k_out, v_out, s_new


def kernel(x_prompt, x_sample, cache_sb_k, cache_sb_v, state_hgrn, g_pre, w_in, b_gate, hg_lb_raw,
           hg_norm_g, w_branch_a, w_branch_b, w_out, g_post):
    depth = w_in.shape[0]
    y_p, y_s = x_prompt, x_sample
    outs = [[] for _ in range(6)]
    for l in range(depth):
        weights = (g_pre[l], w_in[l].astype(BF16), b_gate[l], hg_lb_raw, hg_norm_g[l],
                   w_branch_a[l].astype(BF16), w_branch_b[l].astype(BF16), w_out[l].astype(BF16), g_post[l])
        y_p, kp, vp, sp = _trunk_layer(y_p, None, None, None, l, *weights)
        y_s, ks, vs, ss = _trunk_layer(y_s, cache_sb_k[l], cache_sb_v[l], state_hgrn[l], l, *weights)
        for lst, a in zip(outs, (kp, vp, sp, ks, vs, ss)):
            lst.append(a)
    return (y_p, y_s) + tuple(jnp.stack(lst) for lst in outs)
```

```python
import functools
import math

import numpy as np
import jax
import jax.numpy as jnp
from jax import lax
from jax.experimental import pallas as pl
from jax.experimental.pallas import tpu as pltpu

SB_HEADS = 8
SB_HEAD_DIM = 64
SB_WIDTH = SB_HEADS * SB_HEAD_DIM
HG_HEADS = 4
HG_KEY_DIM = 128
HG_VAL_DIM = 128
HG_WIDTH = HG_HEADS * HG_KEY_DIM
HG_BLOCK = 16
NORM_EPS = 1e-6

V7X_LANES = 128
V7X_VMEM_LIMIT = 56 * 1024 * 1024
SB_PAIRS = SB_WIDTH // V7X_LANES

SB_LOG2_UNDERFLOW = 150.5
LOG2_E = math.log2(math.e)

F32 = jnp.float32
BF16 = jnp.bfloat16

_NT = (((1,), (1,)), ((), ()))
_TN = (((0,), (0,)), ((), ()))


def _dot(a, b):
    return jnp.dot(a, b, preferred_element_type=F32)


def _dot_nt(a, b):
    return lax.dot_general(a, b, _NT, preferred_element_type=F32)


def _dot_tn(a, b):
    return lax.dot_general(a, b, _TN, preferred_element_type=F32)


def _split_bf16(x):
    hi = x.astype(BF16)
    lo = (x - hi.astype(F32)).astype(BF16)
    return hi, lo


IN_Q, IN_K, IN_V, IN_ZA, IN_HG_Q, IN_HG_F, IN_HG_V, IN_HG_Z, IN_GATE_A, IN_GATE_B = range(10)
IN_MIXER_GROUPS = (IN_Q, IN_K, IN_V, IN_HG_Q, IN_HG_F, IN_HG_V)
IN_GATE_GROUPS = (IN_ZA, IN_HG_Z, IN_GATE_A, IN_GATE_B)


def _in_widths(d_model):
    widths = (SB_WIDTH,) * 4 + (HG_WIDTH,) * 4 + (d_model, d_model)
    return widths, np.concatenate([[0], np.cumsum(widths)])


def _rms_norm_bf16(x, g):
    ms = jnp.mean(x * x, axis=-1, keepdims=True)
    return (x * lax.rsqrt(ms + NORM_EPS) * g).astype(BF16)


def _in_proj_kernel(*refs, groups, t_groups, widths, offsets, q_scale):
    if t_groups:
        x_ref, g_ref, w_ref, wt_ref = refs[:4]
        out_refs = refs[4:]
    else:
        x_ref, g_ref, w_ref = refs[:3]
        out_refs = refs[3:]
    xn = _rms_norm_bf16(x_ref[0], g_ref[...])
    if t_groups:
        ht = _dot_nt(wt_ref[...], xn)
        off = 0
        for idx in t_groups:
            out_refs[groups.index(idx)][0] = ht[off:off + widths[idx]]
            off += widths[idx]
    for ref, idx in zip(out_refs, groups):
        if idx not in t_groups:
            h = _dot(xn, w_ref[:, offsets[idx]:offsets[idx + 1]])
            if idx == IN_Q:
                h = h * q_scale
            ref[0] = h.astype(ref.dtype)


def _in_proj(x, g_pre, w_in_bf16, *, tm, t_groups):
    b, t, d = x.shape
    widths, offsets = _in_widths(d)
    offsets = tuple(int(o) for o in offsets)
    assert offsets[-1] == w_in_bf16.shape[1] and t % tm == 0
    out_shape, out_specs = [], []
    for idx in IN_MIXER_GROUPS:
        w = widths[idx]
        dt = BF16 if idx == IN_Q else F32
        if idx in t_groups:
            out_shape.append(jax.ShapeDtypeStruct((b, w, t), dt))
            out_specs.append(pl.BlockSpec((1, w, tm), lambda bi, ti: (bi, 0, ti)))
        else:
            out_shape.append(jax.ShapeDtypeStruct((b, t, w), dt))
            out_specs.append(pl.BlockSpec((1, tm, w), lambda bi, ti: (bi, ti, 0)))
    in_specs = [
        pl.BlockSpec((1, tm, d), lambda bi, ti: (bi, ti, 0)),
        pl.BlockSpec((1, d), lambda bi, ti: (0, 0)),
        pl.BlockSpec(w_in_bf16.shape, lambda bi, ti: (0, 0)),
    ]
    args = [x, g_pre.reshape(1, d), w_in_bf16]
    if t_groups:
        w_t = jnp.concatenate([w_in_bf16[:, offsets[i]:offsets[i + 1]] for i in t_groups], axis=1).T
        in_specs.append(pl.BlockSpec(w_t.shape, lambda bi, ti: (0, 0)))
        args.append(w_t)
    return pl.pallas_call(
        functools.partial(_in_proj_kernel, groups=IN_MIXER_GROUPS, t_groups=t_groups, widths=widths,
                          offsets=offsets, q_scale=SB_HEAD_DIM ** -0.5 * LOG2_E),
        grid=(b, t // tm),
        in_specs=in_specs,
        out_specs=tuple(out_specs),
        out_shape=tuple(out_shape),
        compiler_params=pltpu.CompilerParams(
            dimension_semantics=("arbitrary", "arbitrary"), vmem_limit_bytes=V7X_VMEM_LIMIT),
        name="in_proj",
    )(*args)


def _suffix_ones(n, copies):
    r = np.arange(n)
    u = (r[:, None] >= r[None, :]).astype(np.float32)
    return jnp.asarray(np.concatenate([u] * copies, axis=0), dtype=BF16)


def _sb_weights(z, u2, carry, mask):
    sp = jnp.maximum(z, 0.0) + jnp.log(1.0 + jnp.exp2(-jnp.abs(z))) * LOG2_E
    if mask is not None:
        sp = jnp.where(mask, sp, 0.0)
    hi, lo = _split_bf16(sp)
    tot = _dot(jnp.concatenate([hi, lo], axis=1), u2) + carry
    a = jnp.exp2(z - tot)
    if mask is not None:
        a = jnp.where(mask, a, 0.0)
    return a.astype(BF16), tot[:, 0:1]


def _head_split(x2, first):
    zero = jnp.zeros_like(x2)
    return jnp.where(first, x2, zero), jnp.where(first, zero, x2)


def _sb_prompt_kernel(q_ref, kt_ref, vt_ref, u_ref, o_ref, kt_scr, ve_scr, vo_scr, acc_scr, car_scr, *, tq):
    i = pl.program_id(1)
    n_chunks = kt_scr.shape[0]

    @pl.when(i == 0)
    def _():
        row = lax.broadcasted_iota(jnp.int32, (SB_WIDTH, tq), 0)
        even = (row // SB_HEAD_DIM) % 2 == 0
        for n in range(n_chunks):
            kt_scr[n] = kt_ref[0, :, n * tq:(n + 1) * tq].astype(BF16)
            v = vt_ref[0, :, n * tq:(n + 1) * tq]
            ve_scr[n] = jnp.where(even, v, 0.0).astype(BF16)
            vo_scr[n] = jnp.where(even, 0.0, v).astype(BF16)

    late = tq - tq // 4

    def first(rows):
        return lax.broadcasted_iota(jnp.int32, (rows, V7X_LANES), 1) < SB_HEAD_DIM

    def sweep(jobs):
        qs, z, sp16, tot, wts, pv = {}, {}, {}, {}, {}, {}
        masks = []
        for j, (n, (r0, r1), (k0, k1), offset, fresh) in enumerate(jobs):
            if offset is None:
                masks.append(None)
            else:
                r = lax.broadcasted_iota(jnp.int32, (r1 - r0, k1 - k0), 0)
                c = lax.broadcasted_iota(jnp.int32, (r1 - r0, k1 - k0), 1)
                masks.append(c < r + offset)
            for p in range(SB_PAIRS):
                q2 = q_ref[0, r0:r1, V7X_LANES * p:V7X_LANES * (p + 1)]
                qs[j, 2 * p], qs[j, 2 * p + 1] = _head_split(q2, first(r1 - r0))

        def pair_rows(h):
            return slice(V7X_LANES * (h // 2), V7X_LANES * (h // 2 + 1))

        def scores(j, h):
            n, _, (k0, k1), _, _ = jobs[j]
            z[j, h] = _dot(qs.pop((j, h)), kt_scr[n, pair_rows(h), k0:k1])

        def softplus(j, h):
            zz = z[j, h]
            sp = jnp.maximum(zz, 0.0) + jnp.log(1.0 + jnp.exp2(-jnp.abs(zz))) * LOG2_E
            if masks[j] is not None:
                sp = jnp.where(masks[j], sp, 0.0)
            sp16[j, h] = sp.astype(BF16)

        def suffix_sums(j, h):
            _, (r0, r1), (k0, k1), _, fresh = jobs[j]
            carry = jnp.zeros((r1 - r0, 1), F32) if fresh else car_scr[h, r0:r1]
            tot[j, h] = _dot(sp16.pop((j, h)), u_ref[:k1 - k0, :k1 - k0]) + carry

        def weights(j, h):
            a = jnp.exp2(z.pop((j, h)) - tot[j, h])
            if masks[j] is not None:
                a = jnp.where(masks[j], a, 0.0)
            wts[j, h] = a.astype(BF16)
            tot[j, h] = tot[j, h][:, 0:1]

        def weighted_values(j, h):
            n, _, (k0, k1), _, _ = jobs[j]
            v_scr = vo_scr if h % 2 else ve_scr
            pv[j, h] = _dot_nt(wts.pop((j, h)), v_scr[n, pair_rows(h), k0:k1])

        stages = (scores, softplus, suffix_sums, weights, weighted_values)
        n_items = len(jobs) * SB_HEADS
        for step in range(n_items + len(stages) - 1):
            for k, stage in enumerate(stages):
                if 0 <= step - k < n_items:
                    stage(*divmod(step - k, SB_HEADS))

        for j, (n, (r0, r1), _, _, fresh) in enumerate(jobs):
            for p in range(SB_PAIRS):
                new = pv[j, 2 * p] + pv[j, 2 * p + 1]
                acc_scr[p, r0:r1] = new if fresh else acc_scr[p, r0:r1] + new
            for h in range(SB_HEADS):
                car_scr[h, r0:r1] = tot[j, h]

    def lowest_total(r0, r1):
        low = car_scr[0, r0:r1]
        for h in range(1, SB_HEADS):
            low = jnp.minimum(low, car_scr[h, r0:r1])
        return jnp.min(low)

    sweep([(i, (0, tq), (0, tq), 0, True)])

    def cond(st):
        n, low_early, low_late = st
        return jnp.logical_and(n >= 0, jnp.minimum(low_early, low_late) <= SB_LOG2_UNDERFLOW)

    def older(n, r0, r1, low):
        def run():
            sweep([(n, (r0, r1), (0, tq), None, False)])
            return lowest_total(r0, r1)
        return lax.cond(low <= SB_LOG2_UNDERFLOW, run, lambda: low)

    def body(st):
        n, low_early, low_late = st
        return n - 1, older(n, 0, late, low_early), older(n, late, tq, low_late)

    lax.while_loop(cond, body, (i - 1, lowest_total(0, late), lowest_total(late, tq)))
    for p in range(SB_PAIRS):
        o_ref[0, :, V7X_LANES * p:V7X_LANES * (p + 1)] = acc_scr[p]


def _sb_prompt(q, kt, vt, *, tq):
    b, t, w = q.shape
    assert w == SB_WIDTH and t % tq == 0 and kt.shape == (b, w, t)
    n_chunks = t // tq
    kv_spec = pl.BlockSpec((1, w, t), lambda bi, i: (bi, 0, 0))
    q_spec = pl.BlockSpec((1, tq, w), lambda bi, i: (bi, i, 0))
    return pl.pallas_call(
        functools.partial(_sb_prompt_kernel, tq=tq),
        grid=(b, n_chunks),
        in_specs=[q_spec, kv_spec, kv_spec, pl.BlockSpec((tq, tq), lambda bi, i: (0, 0))],
        out_specs=q_spec,
        out_shape=jax.ShapeDtypeStruct((b, t, w), F32),
        scratch_shapes=[pltpu.VMEM((n_chunks, w, tq), BF16)] * 3
        + [pltpu.VMEM((SB_PAIRS, tq, V7X_LANES), F32), pltpu.VMEM((SB_HEADS, tq, 1), F32)],
        compiler_params=pltpu.CompilerParams(
            dimension_semantics=("arbitrary", "arbitrary"), vmem_limit_bytes=V7X_VMEM_LIMIT),
        name="sb_prompt",
    )(q, kt, vt, _suffix_ones(tq, 1))


def _sb_cached_kernel(q_ref, k_ref, v_ref, ckt_ref, cvt_ref, ud_ref, uc_ref, o_ref, acc_scr, car_scr,
                      *, tq, kn, n_chunks):
    first = lax.broadcasted_iota(jnp.int32, (tq, V7X_LANES), 1) < SB_HEAD_DIM
    r = lax.broadcasted_iota(jnp.int32, (tq, tq), 0)
    c = lax.broadcasted_iota(jnp.int32, (tq, tq), 1)
    causal = c < r
    even = (lax.broadcasted_iota(jnp.int32, (V7X_LANES, kn), 0) // SB_HEAD_DIM) % 2 == 0
    qs = []
    for p in range(SB_PAIRS):
        qs.extend(_head_split(q_ref[0, :, V7X_LANES * p:V7X_LANES * (p + 1)], first))

    def pair_rows(h):
        return slice(V7X_LANES * (h // 2), V7X_LANES * (h // 2 + 1))

    def sweep(n):
        z, wts, tot, pv = {}, {}, {}, {}

        def scores(h):
            if n is None:
                z[h] = _dot_nt(qs[h], k_ref[0, :, pair_rows(h)].astype(BF16))
            else:
                z[h] = _dot(qs[h], ckt_ref[0, pair_rows(h), n * kn:(n + 1) * kn].astype(BF16))

        def weights(h):
            if n is None:
                wts[h], tot[h] = _sb_weights(z.pop(h), ud_ref[...], jnp.zeros((tq, 1), F32), causal)
            else:
                wts[h], tot[h] = _sb_weights(z.pop(h), uc_ref[...], car_scr[h], None)

        def weighted_values(h):
            if n is None:
                v = _head_split(v_ref[0, :, pair_rows(h)], first)[h % 2]
                pv[h] = _dot(wts.pop(h), v.astype(BF16))
            else:
                v = cvt_ref[0, pair_rows(h), n * kn:(n + 1) * kn]
                v = jnp.where(even, 0.0, v) if h % 2 else jnp.where(even, v, 0.0)
                pv[h] = _dot_nt(wts.pop(h), v.astype(BF16))

        stages = (scores, weights, weighted_values)
        for step in range(SB_HEADS + len(stages) - 1):
            for k, stage in enumerate(stages):
                if 0 <= step - k < SB_HEADS:
                    stage(step - k)
        low = None
        for p in range(SB_PAIRS):
            new = pv[2 * p] + pv[2 * p + 1]
            acc_scr[p] = new if n is None else acc_scr[p] + new
        for h in range(SB_HEADS):
            car_scr[h] = tot[h]
            low = tot[h] if low is None else jnp.minimum(low, tot[h])
        return jnp.min(low)

    lowest = sweep(None)
    for n in range(n_chunks - 1, -1, -1):
        lowest = lax.cond(lowest <= SB_LOG2_UNDERFLOW, functools.partial(sweep, n), lambda low=lowest: low)
    for p in range(SB_PAIRS):
        o_ref[0, :, V7X_LANES * p:V7X_LANES * (p + 1)] = acc_scr[p]


def _sb_cached(q, k, v, cache_kt, cache_vt, *, kn):
    b, tq, w = q.shape
    past = cache_kt.shape[2]
    assert w == SB_WIDTH and past % kn == 0 and cache_kt.shape == (b, w, past)
    new_spec = pl.BlockSpec((1, tq, w), lambda bi: (bi, 0, 0))
    cache_spec = pl.BlockSpec((1, w, past), lambda bi: (bi, 0, 0))
    return pl.pallas_call(
        functools.partial(_sb_cached_kernel, tq=tq, kn=kn, n_chunks=past // kn),
        grid=(b,),
        in_specs=[new_spec, new_spec, new_spec, cache_spec, cache_spec,
                  pl.BlockSpec((2 * tq, tq), lambda bi: (0, 0)),
                  pl.BlockSpec((2 * kn, kn), lambda bi: (0, 0))],
        out_specs=new_spec,
        out_shape=jax.ShapeDtypeStruct((b, tq, w), F32),
        scratch_shapes=[pltpu.VMEM((SB_PAIRS, tq, V7X_LANES), F32), pltpu.VMEM((SB_HEADS, tq, 1), F32)],
        compiler_params=pltpu.CompilerParams(
            dimension_semantics=("arbitrary",), vmem_limit_bytes=V7X_VMEM_LIMIT),
        name="sb_cached",
    )(q, k, v, cache_kt, cache_vt, _suffix_ones(tq, 2), _suffix_ones(kn, 2))


def _hgrn_consts(c):
    t = np.arange(c)
    sub = t // HG_BLOCK
    same = sub[:, None] == sub[None, :]
    tri = (same & (t[None, :] <= t[:, None])).astype(np.float32)
    ones = same.astype(np.float32)
    x = sub[:, None] ^ sub[None, :]
    level = np.where(x == 0, 0, np.floor(np.log2(np.maximum(x, 1))).astype(np.int64) + 1)
    level = np.where(t[None, :] <= t[:, None], level, -1).astype(np.int32)
    pre = np.block([[tri, tri], [ones, ones]])
    return jnp.asarray(pre, dtype=BF16), jnp.asarray(level)


def _prod(vecs):
    out = None
    for v in vecs:
        out = v if out is None else out * v
    return out


def _rows(vecs, width):
    one = jnp.ones((HG_BLOCK, width), F32)
    return jnp.concatenate(
        [one if v is None else jnp.broadcast_to(v, (HG_BLOCK, width)) for v in vecs], axis=0)


def _hgrn_group(read, write, lb, g, pre, lvl, states, *, c, group, v_t):
    width = HG_WIDTH
    nsub = c // HG_BLOCK
    levels = int(round(math.log2(nsub)))
    states = list(states)
    level_masks = [lvl == l for l in range(levels + 1)]
    ch = [dict() for _ in range(group)]
    item = {}

    def head(h):
        return slice(HG_KEY_DIM * h, HG_KEY_DIM * (h + 1))

    def gates(gi):
        hq, hf, v = read(gi)
        f = lb + (1.0 - lb) * jax.nn.sigmoid(hf)
        ch[gi].update(hq=hq, kb=1.0 - f,
                      v_bf=v.astype(BF16), split=jnp.concatenate(_split_bf16(jnp.log(f)), axis=0))

    def prefix(gi):
        ch[gi]["sums"] = _dot(pre, ch[gi].pop("split"))

    def decays(gi):
        sums, hq, kb = (ch[gi].pop(k) for k in ("sums", "hq", "kb"))
        b16 = sums[:c]
        bend = sums[c:]
        q_dec = hq * jnp.exp(b16)
        dmat = jnp.exp(bend)
        k_inv = kb * jnp.exp(-b16)
        k_end = k_inv * dmat
        d = [dmat[HG_BLOCK * i:HG_BLOCK * i + 1, :] for i in range(nsub)]
        q_lv = [q_dec.astype(BF16)]
        k_lv = [k_inv.astype(BF16)]
        if levels >= 1:
            q_lv.append(q_lv[0])
            k_lv.append(k_end.astype(BF16))
        for l in range(2, levels + 1):
            half = 2 ** (l - 1)
            alpha, beta = [], []
            for i in range(nsub):
                mid = (i // (2 * half)) * 2 * half + half
                alpha.append(_prod(d[mid:i]) if i > mid else None)
                beta.append(_prod(d[i + 1:mid]) if i + 1 < mid else None)
            q_lv.append((q_dec * _rows(alpha, width)).astype(BF16))
            k_lv.append((k_end * _rows(beta, width)).astype(BF16))
        if nsub > 1:
            alpha_c = [_prod(d[:i]) if i > 0 else None for i in range(nsub)]
            beta_c = [_prod(d[i + 1:]) if i + 1 < nsub else None for i in range(nsub)]
            q_c = (q_dec * _rows(alpha_c, width)).astype(BF16)
            k_c = (k_end * _rows(beta_c, width)).astype(BF16)
        else:
            q_c = q_lv[0]
            k_c = k_end.astype(BF16)
        ch[gi].update(q_lv=q_lv, k_lv=k_lv, q_c=q_c, k_c=k_c, d_all=_prod(d))

    def level_scores(gi, h):
        q_lv, k_lv = ch[gi]["q_lv"], ch[gi]["k_lv"]
        item[gi, h] = [_dot_nt(q_lv[l][:, head(h)], k_lv[l][:, head(h)]) for l in range(levels + 1)]

    def combine(gi, h):
        att = jnp.zeros((c, c), F32)
        for l, scores in enumerate(item[gi, h]):
            att = jnp.where(level_masks[l], scores, att)
        item[gi, h] = att.astype(BF16)

    def outputs(gi, h):
        att = item[gi, h]
        v_bf, q_c, k_c = ch[gi]["v_bf"], ch[gi]["q_c"], ch[gi]["k_c"]
        if v_t:
            vt_h = v_bf[head(h), :]
            o_intra = _dot_nt(att, vt_h)
            st_add = _dot(vt_h, k_c[:, head(h)])
        else:
            o_intra = _dot(att, v_bf[:, head(h)])
            st_add = _dot_tn(v_bf[:, head(h)], k_c[:, head(h)])
        item[gi, h] = (o_intra + _dot_nt(q_c[:, head(h)], states[h].astype(BF16)), st_add)

    def finish(gi, h):
        o, st_add = item.pop((gi, h))
        ms = jnp.mean(o * o, axis=-1, keepdims=True)
        write(gi, h, o * lax.rsqrt(ms + NORM_EPS) * g[:, head(h)])
        states[h] = states[h] * ch[gi]["d_all"][:, head(h)] + st_add

    chunk_stages = (gates, prefix, decays)
    item_stages = (level_scores, combine, outputs, finish)
    n_items = group * HG_HEADS
    for step in range(-len(chunk_stages), n_items + len(item_stages) - 1):
        for gi in range(group):
            for k, stage in enumerate(chunk_stages):
                if step == HG_HEADS * gi - len(chunk_stages) + k:
                    stage(gi)
        for k, stage in enumerate(item_stages):
            if 0 <= step - k < n_items:
                stage(*divmod(step - k, HG_HEADS))
    return states


def _hgrn_kernel(*refs, c, group, layer, has_s0, v_t, n_steps):
    if has_s0:
        (hq_ref, hf_ref, hi_ref, lbraw_ref, g_ref, pre_ref, lvl_ref, s0_ref,
         o_ref, sout_ref, st_scr) = refs
    else:
        (hq_ref, hf_ref, hi_ref, lbraw_ref, g_ref, pre_ref, lvl_ref,
         o_ref, sout_ref, st_scr) = refs
        s0_ref = None
    ci = pl.program_id(1)

    @pl.when(ci == 0)
    def _():
        for h in range(HG_HEADS):
            if has_s0:
                st_scr[h] = s0_ref[0, h].T
            else:
                st_scr[h] = jnp.zeros((HG_VAL_DIM, HG_KEY_DIM), F32)

    raw = lbraw_ref[...]
    e = jnp.exp(raw - jnp.max(raw, axis=0, keepdims=True))
    lb = jnp.sum(e[:layer + 1], axis=0, keepdims=True) / jnp.sum(e, axis=0, keepdims=True)

    pre = pre_ref[...]
    lvl = lvl_ref[...]
    g = g_ref[...]
    def read(gi):
        rs = slice(gi * c, (gi + 1) * c)
        return hq_ref[0, rs, :], hf_ref[0, rs, :], hi_ref[0, :, rs] if v_t else hi_ref[0, rs, :]

    def write(gi, h, o):
        o_ref[0, gi * c:(gi + 1) * c, HG_VAL_DIM * h:HG_VAL_DIM * (h + 1)] = o

    states = _hgrn_group(read, write, lb, g, pre, lvl, [st_scr[h] for h in range(HG_HEADS)],
                         c=c, group=group, v_t=v_t)
    for h in range(HG_HEADS):
        st_scr[h] = states[h]

    @pl.when(ci == n_steps - 1)
    def _():
        for h in range(HG_HEADS):
            sout_ref[0, h] = st_scr[h].T


def _hgrn(hq, hf, hi, lb_raw, norm_g, s0, *, layer, c, group, v_t):
    b, t, w = hq.shape
    span = c * group
    assert w == HG_WIDTH and t % span == 0 and c % HG_BLOCK == 0
    n_steps = t // span
    pre, level = _hgrn_consts(c)
    tok = pl.BlockSpec((1, span, w), lambda bi, ci: (bi, ci, 0))
    tok_t = pl.BlockSpec((1, w, span), lambda bi, ci: (bi, 0, ci))
    const2 = lambda shape: pl.BlockSpec(shape, lambda bi, ci: (0, 0))
    st_spec = pl.BlockSpec((1, HG_HEADS, HG_KEY_DIM, HG_VAL_DIM), lambda bi, ci: (bi, 0, 0, 0))
    in_specs = [tok, tok, tok_t if v_t else tok, const2(lb_raw.shape), const2((1, w)),
                const2(pre.shape), const2(level.shape)]
    args = [hq, hf, hi, lb_raw, norm_g.reshape(1, w), pre, level]
    if s0 is not None:
        in_specs.append(st_spec)
        args.append(s0)
    return pl.pallas_call(
        functools.partial(_hgrn_kernel, c=c, group=group, layer=layer, has_s0=s0 is not None, v_t=v_t,
                          n_steps=n_steps),
        grid=(b, n_steps),
        in_specs=in_specs,
        out_specs=(tok, st_spec),
        out_shape=(jax.ShapeDtypeStruct((b, t, w), F32),
                   jax.ShapeDtypeStruct((b, HG_HEADS, HG_KEY_DIM, HG_VAL_DIM), F32)),
        scratch_shapes=[pltpu.VMEM((HG_HEADS, HG_VAL_DIM, HG_KEY_DIM), F32)],
        compiler_params=pltpu.CompilerParams(
            dimension_semantics=("arbitrary", "arbitrary"), vmem_limit_bytes=V7X_VMEM_LIMIT),
        name="hgrn2",
    )(*args)


def _out_kernel(x_ref, osb_ref, ohg_ref, gpre_ref, wg_ref, bg_ref, wa_ref, wb_ref, wo_ref, gp_ref, y_ref):
    d_model = x_ref.shape[-1]
    x = x_ref[...]
    xn = _rms_norm_bf16(x, gpre_ref[...])
    off = np.cumsum([0, SB_WIDTH, HG_WIDTH, d_model, d_model])
    z_a, hz, ga, gb = (_dot(xn, wg_ref[:, off[i]:off[i + 1]]) for i in range(4))
    u_a = _dot((osb_ref[...] * jax.nn.silu(z_a)).astype(BF16), wa_ref[...])
    u_b = _dot((ohg_ref[...] * jax.nn.silu(hz)).astype(BF16), wb_ref[...])
    bg = bg_ref[...]
    merged = (jax.nn.sigmoid(ga + bg[:, :d_model]) * u_a + jax.nn.sigmoid(gb + bg[:, d_model:]) * u_b)
    m = _dot(merged.astype(BF16), wo_ref[...])
    ms = jnp.mean(m * m, axis=-1, keepdims=True)
    y_ref[...] = x + m * lax.rsqrt(ms + NORM_EPS) * gp_ref[...]


def _out_stage(x2d, o_sb, o_hg, g_pre, w_in_bf16, b_gate, wa, wb, wo, g_post, *, tm):
    n, d = x2d.shape
    assert n % tm == 0
    _, offsets = _in_widths(d)
    w_gate = jnp.concatenate([w_in_bf16[:, offsets[i]:offsets[i + 1]] for i in IN_GATE_GROUPS], axis=1)
    row = lambda w: pl.BlockSpec((tm, w), lambda i: (i, 0))
    const = lambda shape: pl.BlockSpec(shape, lambda i: (0, 0))
    return pl.pallas_call(
        _out_kernel,
        grid=(n // tm,),
        in_specs=[row(d), row(SB_WIDTH), row(HG_WIDTH), const((1, d)), const(w_gate.shape),
                  const((1, 2 * d)), const(wa.shape), const(wb.shape), const(wo.shape), const((1, d))],
        out_specs=row(d),
        out_shape=jax.ShapeDtypeStruct((n, d), F32),
        compiler_params=pltpu.CompilerParams(
            dimension_semantics=("arbitrary",), vmem_limit_bytes=V7X_VMEM_LIMIT),
        name="out_stage",
    )(x2d, o_sb, o_hg, g_pre.reshape(1, d), w_gate, b_gate.reshape(1, 2 * d), wa, wb, wo, g_post.reshape(1, d))


def _largest_tile(n, cap):
    t = cap
    while n % t:
        t //= 2
    return t


def _trunk_layer(x, cache_k, cache_v, s0, layer, g_pre, w_in, b_gate, lb_raw, hg_norm_g,
                 w_a, w_b, w_out, g_post):
    b, t, d = x.shape
    n = b * t
    heads = (SB_HEADS, SB_HEAD_DIM)
    if cache_k is None:
        q, kt, vt, hq, hf, hi = _in_proj(x, g_pre, w_in, tm=_largest_tile(t, 512), t_groups=(IN_K, IN_V, IN_HG_V))
        o_sb = _sb_prompt(q, kt, vt, tq=_largest_tile(t, 256))
        k_out, v_out = (a.reshape(b, *heads, t).transpose(0, 3, 1, 2) for a in (kt, vt))
    else:
        tm = _largest_tile(n, 256)
        outs = _in_proj(x.reshape(1, n, d), g_pre, w_in, tm=tm, t_groups=())
        q, k, v, hq, hf, hi = (a.reshape(b, t, a.shape[-1]) for a in outs)
        past = cache_k.shape[1]
        cache_kt, cache_vt = (a.transpose(0, 2, 3, 1).reshape(b, SB_WIDTH, past) for a in (cache_k, cache_v))
        o_sb = _sb_cached(q, k, v, cache_kt, cache_vt, kn=_largest_tile(past, 256))
        k_out, v_out = k.reshape(b, t, *heads), v.reshape(b, t, *heads)
    c = _largest_tile(t, 128)
    o_hg, s_new = _hgrn(hq, hf, hi, lb_raw, hg_norm_g.reshape(-1), s0, layer=layer, c=c,
                        group=_largest_tile(t // c, 4), v_t=cache_k is None)
    flat = lambda a: a.reshape(n, a.shape[-1])
    y = _out_stage(flat(x), flat(o_sb), flat(o_hg), g_pre, w_in, b_gate, w_a, w_b, w_out, g_post,
                   tm=_largest_tile(n, 256))
    return y.reshape(b, t, d), k_out, v_out, s_new


def kernel(x_prompt, x_sample, cache_sb_k, cache_sb_v, state_hgrn, g_pre, w_in, b_gate, hg_lb_raw,
           hg_norm_g, w_branch_a, w_branch_b, w_out, g_post):
    depth = w_in.shape[0]
    y_p, y_s = x_prompt, x_sample
    outs = [[] for _ in range(6)]
    for l in range(depth):
        weights = (g_pre[l], w_in[l].astype(BF16), b_gate[l], hg_lb_raw, hg_norm_g[l],
                   w_branch_a[l].astype(BF16), w_branch_b[l].astype(BF16), w_out[l].astype(BF16), g_post[l])
        y_p, kp, vp, sp = _trunk_layer(y_p, None, None, None, l, *weights)
        y_s, ks, vs, ss = _trunk_layer(y_s, cache_sb_k[l], cache_sb_v[l], state_hgrn[l], l, *weights)
        for lst, a in zip(outs, (kp, vp, sp, ks, vs, ss)):
            lst.append(a)
    return (y_p, y_s) + tuple(jnp.stack(lst) for lst in outs)
```

```python
import functools
import math

import numpy as np
import jax
import jax.numpy as jnp
from jax import lax
from jax.experimental import pallas as pl
from jax.experimental.pallas import tpu as pltpu

SB_HEADS = 8
SB_HEAD_DIM = 64
SB_WIDTH = SB_HEADS * SB_HEAD_DIM
HG_HEADS = 4
HG_KEY_DIM = 128
HG_VAL_DIM = 128
HG_WIDTH = HG_HEADS * HG_KEY_DIM
HG_BLOCK = 16
NORM_EPS = 1e-6

V7X_LANES = 128
V7X_VMEM_LIMIT = 56 * 1024 * 1024
SB_PAIRS = SB_WIDTH // V7X_LANES

SB_LOG2_UNDERFLOW = 150.5
LOG2_E = math.log2(math.e)

F32 = jnp.float32
BF16 = jnp.bfloat16

_NT = (((1,), (1,)), ((), ()))
_TN = (((0,), (0,)), ((), ()))


def _dot(a, b):
    return jnp.dot(a, b, preferred_element_type=F32)


def _dot_nt(a, b):
    return lax.dot_general(a, b, _NT, preferred_element_type=F32)


def _dot_tn(a, b):
    return lax.dot_general(a, b, _TN, preferred_element_type=F32)


def _split_bf16(x):
    hi = x.astype(BF16)
    lo = (x - hi.astype(F32)).astype(BF16)
    return hi, lo


IN_Q, IN_K, IN_V, IN_ZA, IN_HG_Q, IN_HG_F, IN_HG_V, IN_HG_Z, IN_GATE_A, IN_GATE_B = range(10)
IN_MIXER_GROUPS = (IN_Q, IN_K, IN_V, IN_HG_Q, IN_HG_F, IN_HG_V)
IN_GATE_GROUPS = (IN_ZA, IN_HG_Z, IN_GATE_A, IN_GATE_B)


def _in_widths(d_model):
    widths = (SB_WIDTH,) * 4 + (HG_WIDTH,) * 4 + (d_model, d_model)
    return widths, np.concatenate([[0], np.cumsum(widths)])


def _rms_norm_bf16(x, g):
    ms = jnp.mean(x * x, axis=-1, keepdims=True)
    return (x * lax.rsqrt(ms + NORM_EPS) * g).astype(BF16)


def _in_proj_kernel(*refs, groups, t_groups, widths, columns, q_scale):
    if t_groups:
        x_ref, g_ref, w_ref, wt_ref = refs[:4]
        out_refs = refs[4:]
    else:
        x_ref, g_ref, w_ref = refs[:3]
        out_refs = refs[3:]
    xn = _rms_norm_bf16(x_ref[0], g_ref[...])
    if t_groups:
        ht = _dot_nt(wt_ref[...], xn)
        off = 0
        for idx in t_groups:
            out_refs[groups.index(idx)][0] = ht[off:off + widths[idx]]
            off += widths[idx]
    for ref, idx in zip(out_refs, groups):
        if idx not in t_groups:
            h = _dot(xn, w_ref[:, columns[idx][0]:columns[idx][1]])
            if idx == IN_Q:
                h = h * q_scale
            ref[0] = h.astype(ref.dtype)


def _in_proj(x, g_pre, w_in_bf16, *, tm, t_groups):
    b, t, d = x.shape
    widths, offsets = _in_widths(d)
    offsets = tuple(int(o) for o in offsets)
    assert offsets[-1] == w_in_bf16.shape[1] and t % tm == 0
    out_shape, out_specs = [], []
    for idx in IN_MIXER_GROUPS:
        w = widths[idx]
        dt = BF16 if idx == IN_Q else F32
        if idx in t_groups:
            out_shape.append(jax.ShapeDtypeStruct((b, w, t), dt))
            out_specs.append(pl.BlockSpec((1, w, tm), lambda bi, ti: (bi, 0, ti)))
        else:
            out_shape.append(jax.ShapeDtypeStruct((b, t, w), dt))
            out_specs.append(pl.BlockSpec((1, tm, w), lambda bi, ti: (bi, ti, 0)))
    columns, parts, start = {}, [], 0
    for idx in IN_MIXER_GROUPS:
        if idx not in t_groups:
            parts.append(w_in_bf16[:, offsets[idx]:offsets[idx + 1]])
            columns[idx] = (start, start + widths[idx])
            start += widths[idx]
    w_used = jnp.concatenate(parts, axis=1)
    in_specs = [
        pl.BlockSpec((1, tm, d), lambda bi, ti: (bi, ti, 0)),
        pl.BlockSpec((1, d), lambda bi, ti: (0, 0)),
        pl.BlockSpec(w_used.shape, lambda bi, ti: (0, 0)),
    ]
    args = [x, g_pre.reshape(1, d), w_used]
    if t_groups:
        w_t = jnp.concatenate([w_in_bf16[:, offsets[i]:offsets[i + 1]] for i in t_groups], axis=1).T
        in_specs.append(pl.BlockSpec(w_t.shape, lambda bi, ti: (0, 0)))
        args.append(w_t)
    return pl.pallas_call(
        functools.partial(_in_proj_kernel, groups=IN_MIXER_GROUPS, t_groups=t_groups, widths=widths,
                          columns=columns, q_scale=SB_HEAD_DIM ** -0.5 * LOG2_E),
        grid=(b, t // tm),
        in_specs=in_specs,
        out_specs=tuple(out_specs),
        out_shape=tuple(out_shape),
        compiler_params=pltpu.CompilerParams(
            dimension_semantics=("arbitrary", "arbitrary"), vmem_limit_bytes=V7X_VMEM_LIMIT),
        name="in_proj",
    )(*args)


def _suffix_ones(n, copies):
    r = np.arange(n)
    u = (r[:, None] >= r[None, :]).astype(np.float32)
    return jnp.asarray(np.concatenate([u] * copies, axis=0), dtype=BF16)


def _sb_weights(z, u2, carry, mask):
    sp = jnp.maximum(z, 0.0) + jnp.log(1.0 + jnp.exp2(-jnp.abs(z))) * LOG2_E
    if mask is not None:
        sp = jnp.where(mask, sp, 0.0)
    hi, lo = _split_bf16(sp)
    tot = _dot(jnp.concatenate([hi, lo], axis=1), u2) + carry
    a = jnp.exp2(z - tot)
    if mask is not None:
        a = jnp.where(mask, a, 0.0)
    return a.astype(BF16), tot[:, 0:1]


def _head_split(x2, first):
    zero = jnp.zeros_like(x2)
    return jnp.where(first, x2, zero), jnp.where(first, zero, x2)


def _sb_prompt_kernel(q_ref, kt_ref, vt_ref, u_ref, o_ref, kt_scr, ve_scr, vo_scr, acc_scr, car_scr,
                      *, tq, blocks):
    step = pl.program_id(1)
    n_chunks = kt_scr.shape[0]

    @pl.when(step == 0)
    def _():
        row = lax.broadcasted_iota(jnp.int32, (SB_WIDTH, tq), 0)
        even = (row // SB_HEAD_DIM) % 2 == 0
        for n in range(n_chunks):
            kt_scr[n] = kt_ref[0, :, n * tq:(n + 1) * tq].astype(BF16)
            v = vt_ref[0, :, n * tq:(n + 1) * tq]
            ve_scr[n] = jnp.where(even, v, 0.0).astype(BF16)
            vo_scr[n] = jnp.where(even, 0.0, v).astype(BF16)

    late = tq - tq // 4

    def first(rows):
        return lax.broadcasted_iota(jnp.int32, (rows, V7X_LANES), 1) < SB_HEAD_DIM

    def sweep(jobs, row0):
        qs, z, sp16, tot, wts, pv = {}, {}, {}, {}, {}, {}
        masks = []
        for j, (n, (r0, r1), (k0, k1), offset, fresh) in enumerate(jobs):
            if offset is None:
                masks.append(None)
            else:
                r = lax.broadcasted_iota(jnp.int32, (r1 - r0, k1 - k0), 0)
                c = lax.broadcasted_iota(jnp.int32, (r1 - r0, k1 - k0), 1)
                masks.append(c < r + offset)
            for p in range(SB_PAIRS):
                q2 = q_ref[0, row0 + r0:row0 + r1, V7X_LANES * p:V7X_LANES * (p + 1)]
                qs[j, 2 * p], qs[j, 2 * p + 1] = _head_split(q2, first(r1 - r0))

        def pair_rows(h):
            return slice(V7X_LANES * (h // 2), V7X_LANES * (h // 2 + 1))

        def scores(j, h):
            n, _, (k0, k1), _, _ = jobs[j]
            z[j, h] = _dot(qs.pop((j, h)), kt_scr[n, pair_rows(h), k0:k1])

        def softplus(j, h):
            zz = z[j, h]
            sp = jnp.maximum(zz, 0.0) + jnp.log(1.0 + jnp.exp2(-jnp.abs(zz))) * LOG2_E
            if masks[j] is not None:
                sp = jnp.where(masks[j], sp, 0.0)
            sp16[j, h] = sp.astype(BF16)

        def suffix_sums(j, h):
            _, (r0, r1), (k0, k1), _, fresh = jobs[j]
            carry = jnp.zeros((r1 - r0, 1), F32) if fresh else car_scr[h, r0:r1]
            tot[j, h] = _dot(sp16.pop((j, h)), u_ref[:k1 - k0, :k1 - k0]) + carry

        def weights(j, h):
            a = jnp.exp2(z.pop((j, h)) - tot[j, h])
            if masks[j] is not None:
                a = jnp.where(masks[j], a, 0.0)
            wts[j, h] = a.astype(BF16)
            tot[j, h] = tot[j, h][:, 0:1]

        def weighted_values(j, h):
            n, _, (k0, k1), _, _ = jobs[j]
            v_scr = vo_scr if h % 2 else ve_scr
            pv[j, h] = _dot_nt(wts.pop((j, h)), v_scr[n, pair_rows(h), k0:k1])

        stages = (scores, softplus, suffix_sums, weights, weighted_values)
        n_items = len(jobs) * SB_HEADS
        for step in range(n_items + len(stages) - 1):
            for k, stage in enumerate(stages):
                if 0 <= step - k < n_items:
                    stage(*divmod(step - k, SB_HEADS))

        for j, (n, (r0, r1), _, _, fresh) in enumerate(jobs):
            for p in range(SB_PAIRS):
                new = pv[j, 2 * p] + pv[j, 2 * p + 1]
                acc_scr[p, r0:r1] = new if fresh else acc_scr[p, r0:r1] + new
            for h in range(SB_HEADS):
                car_scr[h, r0:r1] = tot[j, h]

    def lowest_total(r0, r1):
        low = car_scr[0, r0:r1]
        for h in range(1, SB_HEADS):
            low = jnp.minimum(low, car_scr[h, r0:r1])
        return jnp.min(low)

    def cond(st):
        n, low_early, low_late = st
        return jnp.logical_and(n >= 0, jnp.minimum(low_early, low_late) <= SB_LOG2_UNDERFLOW)

    for blk in range(blocks):
        i = step * blocks + blk
        row0 = blk * tq
        sweep([(i, (0, tq), (0, tq), 0, True)], row0)

        def older(n, r0, r1, low, row0=row0):
            def run():
                sweep([(n, (r0, r1), (0, tq), None, False)], row0)
                return lowest_total(r0, r1)
            return lax.cond(low <= SB_LOG2_UNDERFLOW, run, lambda: low)

        def body(st, older=older):
            n, low_early, low_late = st
            return n - 1, older(n, 0, late, low_early), older(n, late, tq, low_late)

        lax.while_loop(cond, body, (i - 1, lowest_total(0, late), lowest_total(late, tq)))
        for p in range(SB_PAIRS):
            o_ref[0, row0:row0 + tq, V7X_LANES * p:V7X_LANES * (p + 1)] = acc_scr[p]


def _sb_prompt(q, kt, vt, *, tq, blocks):
    b, t, w = q.shape
    assert w == SB_WIDTH and t % (tq * blocks) == 0 and kt.shape == (b, w, t)
    n_chunks = t // tq
    kv_spec = pl.BlockSpec((1, w, t), lambda bi, i: (bi, 0, 0))
    q_spec = pl.BlockSpec((1, tq * blocks, w), lambda bi, i: (bi, i, 0))
    return pl.pallas_call(
        functools.partial(_sb_prompt_kernel, tq=tq, blocks=blocks),
        grid=(b, n_chunks // blocks),
        in_specs=[q_spec, kv_spec, kv_spec, pl.BlockSpec((tq, tq), lambda bi, i: (0, 0))],
        out_specs=q_spec,
        out_shape=jax.ShapeDtypeStruct((b, t, w), F32),
        scratch_shapes=[pltpu.VMEM((n_chunks, w, tq), BF16)] * 3
        + [pltpu.VMEM((SB_PAIRS, tq, V7X_LANES), F32), pltpu.VMEM((SB_HEADS, tq, 1), F32)],
        compiler_params=pltpu.CompilerParams(
            dimension_semantics=("arbitrary", "arbitrary"), vmem_limit_bytes=V7X_VMEM_LIMIT),
        name="sb_prompt",
    )(q, kt, vt, _suffix_ones(tq, 1))


def _sb_cached_kernel(q_ref, k_ref, v_ref, ckt_ref, cvt_ref, ud_ref, uc_ref, o_ref, acc_scr, car_scr,
                      *, tq, kn, n_chunks):
    first = lax.broadcasted_iota(jnp.int32, (tq, V7X_LANES), 1) < SB_HEAD_DIM
    r = lax.broadcasted_iota(jnp.int32, (tq, tq), 0)
    c = lax.broadcasted_iota(jnp.int32, (tq, tq), 1)
    causal = c < r
    even = (lax.broadcasted_iota(jnp.int32, (V7X_LANES, kn), 0) // SB_HEAD_DIM) % 2 == 0
    qs = []
    for p in range(SB_PAIRS):
        qs.extend(_head_split(q_ref[0, :, V7X_LANES * p:V7X_LANES * (p + 1)], first))

    def pair_rows(h):
        return slice(V7X_LANES * (h // 2), V7X_LANES * (h // 2 + 1))

    def sweep(n):
        z, wts, tot, pv = {}, {}, {}, {}

        def scores(h):
            if n is None:
                z[h] = _dot_nt(qs[h], k_ref[0, :, pair_rows(h)].astype(BF16))
            else:
                z[h] = _dot(qs[h], ckt_ref[0, pair_rows(h), n * kn:(n + 1) * kn].astype(BF16))

        def weights(h):
            if n is None:
                wts[h], tot[h] = _sb_weights(z.pop(h), ud_ref[...], jnp.zeros((tq, 1), F32), causal)
            else:
                wts[h], tot[h] = _sb_weights(z.pop(h), uc_ref[...], car_scr[h], None)

        def weighted_values(h):
            if n is None:
                v = _head_split(v_ref[0, :, pair_rows(h)], first)[h % 2]
                pv[h] = _dot(wts.pop(h), v.astype(BF16))
            else:
                v = cvt_ref[0, pair_rows(h), n * kn:(n + 1) * kn]
                v = jnp.where(even, 0.0, v) if h % 2 else jnp.where(even, v, 0.0)
                pv[h] = _dot_nt(wts.pop(h), v.astype(BF16))

        stages = (scores, weights, weighted_values)
        for step in range(SB_HEADS + len(stages) - 1):
            for k, stage in enumerate(stages):
                if 0 <= step - k < SB_HEADS:
                    stage(step - k)
        low = None
        for p in range(SB_PAIRS):
            new = pv[2 * p] + pv[2 * p + 1]
            acc_scr[p] = new if n is None else acc_scr[p] + new
        for h in range(SB_HEADS):
            car_scr[h] = tot[h]
            low = tot[h] if low is None else jnp.minimum(low, tot[h])
        return jnp.min(low)

    lowest = sweep(None)
    for n in range(n_chunks - 1, -1, -1):
        lowest = lax.cond(lowest <= SB_LOG2_UNDERFLOW, functools.partial(sweep, n), lambda low=lowest: low)
    for p in range(SB_PAIRS):
        o_ref[0, :, V7X_LANES * p:V7X_LANES * (p + 1)] = acc_scr[p]


def _sb_cached(q, k, v, cache_kt, cache_vt, *, kn):
    b, tq, w = q.shape
    past = cache_kt.shape[2]
    assert w == SB_WIDTH and past % kn == 0 and cache_kt.shape == (b, w, past)
    new_spec = pl.BlockSpec((1, tq, w), lambda bi: (bi, 0, 0))
    cache_spec = pl.BlockSpec((1, w, past), lambda bi: (bi, 0, 0))
    return pl.pallas_call(
        functools.partial(_sb_cached_kernel, tq=tq, kn=kn, n_chunks=past // kn),
        grid=(b,),
        in_specs=[new_spec, new_spec, new_spec, cache_spec, cache_spec,
                  pl.BlockSpec((2 * tq, tq), lambda bi: (0, 0)),
                  pl.BlockSpec((2 * kn, kn), lambda bi: (0, 0))],
        out_specs=new_spec,
        out_shape=jax.ShapeDtypeStruct((b, tq, w), F32),
        scratch_shapes=[pltpu.VMEM((SB_PAIRS, tq, V7X_LANES), F32), pltpu.VMEM((SB_HEADS, tq, 1), F32)],
        compiler_params=pltpu.CompilerParams(
            dimension_semantics=("arbitrary",), vmem_limit_bytes=V7X_VMEM_LIMIT),
        name="sb_cached",
    )(q, k, v, cache_kt, cache_vt, _suffix_ones(tq, 2), _suffix_ones(kn, 2))


def _hgrn_consts(c):
    t = np.arange(c)
    sub = t // HG_BLOCK
    same = sub[:, None] == sub[None, :]
    tri = (same & (t[None, :] <= t[:, None])).astype(np.float32)
    ones = same.astype(np.float32)
    x = sub[:, None] ^ sub[None, :]
    level = np.where(x == 0, 0, np.floor(np.log2(np.maximum(x, 1))).astype(np.int64) + 1)
    level = np.where(t[None, :] <= t[:, None], level, -1).astype(np.int32)
    pre = np.block([[tri, tri], [ones, ones]])
    return jnp.asarray(pre, dtype=BF16), jnp.asarray(level)


def _prod(vecs):
    out = None
    for v in vecs:
        out = v if out is None else out * v
    return out


def _rows(vecs, width):
    one = jnp.ones((HG_BLOCK, width), F32)
    return jnp.concatenate(
        [one if v is None else jnp.broadcast_to(v, (HG_BLOCK, width)) for v in vecs], axis=0)


def _hgrn_group(read, write, lb, g, pre, lvl, states, *, c, group, v_t):
    width = HG_WIDTH
    nsub = c // HG_BLOCK
    levels = int(round(math.log2(nsub)))
    states = list(states)
    level_masks = [lvl == l for l in range(levels + 1)]
    ch = [dict() for _ in range(group)]
    item = {}

    def head(h):
        return slice(HG_KEY_DIM * h, HG_KEY_DIM * (h + 1))

    def gates(gi):
        hq, hf, v = read(gi)
        f = lb + (1.0 - lb) * jax.nn.sigmoid(hf)
        ch[gi].update(hq=hq, kb=1.0 - f,
                      v_bf=v.astype(BF16), split=jnp.concatenate(_split_bf16(jnp.log(f)), axis=0))

    def prefix(gi):
        ch[gi]["sums"] = _dot(pre, ch[gi].pop("split"))

    def decays(gi):
        sums, hq, kb = (ch[gi].pop(k) for k in ("sums", "hq", "kb"))
        b16 = sums[:c]
        bend = sums[c:]
        q_dec = hq * jnp.exp(b16)
        dmat = jnp.exp(bend)
        k_inv = kb * jnp.exp(-b16)
        k_end = k_inv * dmat
        d = [dmat[HG_BLOCK * i:HG_BLOCK * i + 1, :] for i in range(nsub)]
        q_lv = [q_dec.astype(BF16)]
        k_lv = [k_inv.astype(BF16)]
        if levels >= 1:
            q_lv.append(q_lv[0])
            k_lv.append(k_end.astype(BF16))
        for l in range(2, levels + 1):
            half = 2 ** (l - 1)
            alpha, beta = [], []
            for i in range(nsub):
                mid = (i // (2 * half)) * 2 * half + half
                alpha.append(_prod(d[mid:i]) if i > mid else None)
                beta.append(_prod(d[i + 1:mid]) if i + 1 < mid else None)
            q_lv.append((q_dec * _rows(alpha, width)).astype(BF16))
            k_lv.append((k_end * _rows(beta, width)).astype(BF16))
        if nsub > 1:
            alpha_c = [_prod(d[:i]) if i > 0 else None for i in range(nsub)]
            beta_c = [_prod(d[i + 1:]) if i + 1 < nsub else None for i in range(nsub)]
            q_c = (q_dec * _rows(alpha_c, width)).astype(BF16)
            k_c = (k_end * _rows(beta_c, width)).astype(BF16)
        else:
            q_c = q_lv[0]
            k_c = k_end.astype(BF16)
        ch[gi].update(q_lv=q_lv, k_lv=k_lv, q_c=q_c, k_c=k_c, d_all=_prod(d))

    def level_scores(gi, h):
        q_lv, k_lv = ch[gi]["q_lv"], ch[gi]["k_lv"]
        item[gi, h] = [_dot_nt(q_lv[l][:, head(h)], k_lv[l][:, head(h)]) for l in range(levels + 1)]

    def combine(gi, h):
        att = jnp.zeros((c, c), F32)
        for l, scores in enumerate(item[gi, h]):
            att = jnp.where(level_masks[l], scores, att)
        item[gi, h] = att.astype(BF16)

    def outputs(gi, h):
        att = item[gi, h]
        v_bf, q_c, k_c = ch[gi]["v_bf"], ch[gi]["q_c"], ch[gi]["k_c"]
        if v_t:
            vt_h = v_bf[head(h), :]
            o_intra = _dot_nt(att, vt_h)
            st_add = _dot(vt_h, k_c[:, head(h)])
        else:
            o_intra = _dot(att, v_bf[:, head(h)])
            st_add = _dot_tn(v_bf[:, head(h)], k_c[:, head(h)])
        item[gi, h] = (o_intra + _dot_nt(q_c[:, head(h)], states[h].astype(BF16)), st_add)

    def finish(gi, h):
        o, st_add = item.pop((gi, h))
        ms = jnp.mean(o * o, axis=-1, keepdims=True)
        write(gi, h, o * lax.rsqrt(ms + NORM_EPS) * g[:, head(h)])
        states[h] = states[h] * ch[gi]["d_all"][:, head(h)] + st_add

    chunk_stages = (gates, prefix, decays)
    item_stages = (level_scores, combine, outputs, finish)
    n_items = group * HG_HEADS
    for step in range(-len(chunk_stages), n_items + len(item_stages) - 1):
        for gi in range(group):
            for k, stage in enumerate(chunk_stages):
                if step == HG_HEADS * gi - len(chunk_stages) + k:
                    stage(gi)
        for k, stage in enumerate(item_stages):
            if 0 <= step - k < n_items:
                stage(*divmod(step - k, HG_HEADS))
    return states


def _hgrn_kernel(*refs, c, group, layer, has_s0, v_t, n_steps):
    if has_s0:
        (hq_ref, hf_ref, hi_ref, lbraw_ref, g_ref, pre_ref, lvl_ref, s0_ref,
         o_ref, sout_ref, st_scr) = refs
    else:
        (hq_ref, hf_ref, hi_ref, lbraw_ref, g_ref, pre_ref, lvl_ref,
         o_ref, sout_ref, st_scr) = refs
        s0_ref = None
    ci = pl.program_id(1)

    @pl.when(ci == 0)
    def _():
        for h in range(HG_HEADS):
            if has_s0:
                st_scr[h] = s0_ref[0, h].T
            else:
                st_scr[h] = jnp.zeros((HG_VAL_DIM, HG_KEY_DIM), F32)

    raw = lbraw_ref[...]
    e = jnp.exp(raw - jnp.max(raw, axis=0, keepdims=True))
    lb = jnp.sum(e[:layer + 1], axis=0, keepdims=True) / jnp.sum(e, axis=0, keepdims=True)

    pre = pre_ref[...]
    lvl = lvl_ref[...]
    g = g_ref[...]
    def read(gi):
        rs = slice(gi * c, (gi + 1) * c)
        return hq_ref[0, rs, :], hf_ref[0, rs, :], hi_ref[0, :, rs] if v_t else hi_ref[0, rs, :]

    def write(gi, h, o):
        o_ref[0, gi * c:(gi + 1) * c, HG_VAL_DIM * h:HG_VAL_DIM * (h + 1)] = o

    states = _hgrn_group(read, write, lb, g, pre, lvl, [st_scr[h] for h in range(HG_HEADS)],
                         c=c, group=group, v_t=v_t)
    for h in range(HG_HEADS):
        st_scr[h] = states[h]

    @pl.when(ci == n_steps - 1)
    def _():
        for h in range(HG_HEADS):
            sout_ref[0, h] = st_scr[h].T


def _hgrn(hq, hf, hi, lb_raw, norm_g, s0, *, layer, c, group, v_t):
    b, t, w = hq.shape
    span = c * group
    assert w == HG_WIDTH and t % span == 0 and c % HG_BLOCK == 0
    n_steps = t // span
    pre, level = _hgrn_consts(c)
    tok = pl.BlockSpec((1, span, w), lambda bi, ci: (bi, ci, 0))
    tok_t = pl.BlockSpec((1, w, span), lambda bi, ci: (bi, 0, ci))
    const2 = lambda shape: pl.BlockSpec(shape, lambda bi, ci: (0, 0))
    st_spec = pl.BlockSpec((1, HG_HEADS, HG_KEY_DIM, HG_VAL_DIM), lambda bi, ci: (bi, 0, 0, 0))
    in_specs = [tok, tok, tok_t if v_t else tok, const2(lb_raw.shape), const2((1, w)),
                const2(pre.shape), const2(level.shape)]
    args = [hq, hf, hi, lb_raw, norm_g.reshape(1, w), pre, level]
    if s0 is not None:
        in_specs.append(st_spec)
        args.append(s0)
    return pl.pallas_call(
        functools.partial(_hgrn_kernel, c=c, group=group, layer=layer, has_s0=s0 is not None, v_t=v_t,
                          n_steps=n_steps),
        grid=(b, n_steps),
        in_specs=in_specs,
        out_specs=(tok, st_spec),
        out_shape=(jax.ShapeDtypeStruct((b, t, w), F32),
                   jax.ShapeDtypeStruct((b, HG_HEADS, HG_KEY_DIM, HG_VAL_DIM), F32)),
        scratch_shapes=[pltpu.VMEM((HG_HEADS, HG_VAL_DIM, HG_KEY_DIM), F32)],
        compiler_params=pltpu.CompilerParams(
            dimension_semantics=("arbitrary", "arbitrary"), vmem_limit_bytes=V7X_VMEM_LIMIT),
        name="hgrn2",
    )(*args)


def _out_kernel(x_ref, osb_ref, ohg_ref, gpre_ref, wg_ref, bg_ref, wa_ref, wb_ref, wo_ref, gp_ref, y_ref):
    d_model = x_ref.shape[-1]
    x = x_ref[...]
    xn = _rms_norm_bf16(x, gpre_ref[...])
    off = np.cumsum([0, SB_WIDTH, HG_WIDTH, d_model, d_model])
    z_a, hz, ga, gb = (_dot(xn, wg_ref[:, off[i]:off[i + 1]]) for i in range(4))
    u_a = _dot((osb_ref[...] * jax.nn.silu(z_a)).astype(BF16), wa_ref[...])
    u_b = _dot((ohg_ref[...] * jax.nn.silu(hz)).astype(BF16), wb_ref[...])
    bg = bg_ref[...]
    merged = (jax.nn.sigmoid(ga + bg[:, :d_model]) * u_a + jax.nn.sigmoid(gb + bg[:, d_model:]) * u_b)
    m = _dot(merged.astype(BF16), wo_ref[...])
    ms = jnp.mean(m * m, axis=-1, keepdims=True)
    y_ref[...] = x + m * lax.rsqrt(ms + NORM_EPS) * gp_ref[...]


def _out_stage(x2d, o_sb, o_hg, g_pre, w_in_bf16, b_gate, wa, wb, wo, g_post, *, tm):
    n, d = x2d.shape
    assert n % tm == 0
    _, offsets = _in_widths(d)
    w_gate = jnp.concatenate([w_in_bf16[:, offsets[i]:offsets[i + 1]] for i in IN_GATE_GROUPS], axis=1)
    row = lambda w: pl.BlockSpec((tm, w), lambda i: (i, 0))
    const = lambda shape: pl.BlockSpec(shape, lambda i: (0, 0))
    return pl.pallas_call(
        _out_kernel,
        grid=(n // tm,),
        in_specs=[row(d), row(SB_WIDTH), row(HG_WIDTH), const((1, d)), const(w_gate.shape),
                  const((1, 2 * d)), const(wa.shape), const(wb.shape), const(wo.shape), const((1, d))],
        out_specs=row(d),
        out_shape=jax.ShapeDtypeStruct((n, d), F32),
        compiler_params=pltpu.CompilerParams(
            dimension_semantics=("arbitrary",), vmem_limit_bytes=V7X_VMEM_LIMIT),
        name="out_stage",
    )(x2d, o_sb, o_hg, g_pre.reshape(1, d), w_gate, b_gate.reshape(1, 2 * d), wa, wb, wo, g_post.reshape(1, d))


def _largest_tile(n, cap):
    t = cap
    while n % t:
        t //= 2
    return t


def _trunk_layer(x, cache_k, cache_v, s0, layer, g_pre, w_in, b_gate, lb_raw, hg_norm_g,
                 w_a, w_b, w_out, g_post):
    b, t, d = x.shape
    n = b * t
    heads = (SB_HEADS, SB_HEAD_DIM)
    if cache_k is None:
        q, kt, vt, hq, hf, hi = _in_proj(x, g_pre, w_in, tm=_largest_tile(t, 1024), t_groups=(IN_K, IN_V, IN_HG_V))
        tq = _largest_tile(t, 256)
        o_sb = _sb_prompt(q, kt, vt, tq=tq, blocks=_largest_tile(t // tq, 2))
        k_out, v_out = (a.reshape(b, *heads, t).transpose(0, 3, 1, 2) for a in (kt, vt))
    else:
        tm = _largest_tile(n, 256)
        outs = _in_proj(x.reshape(1, n, d), g_pre, w_in, tm=tm, t_groups=())
        q, k, v, hq, hf, hi = (a.reshape(b, t, a.shape[-1]) for a in outs)
        past = cache_k.shape[1]
        cache_kt, cache_vt = (a.transpose(0, 2, 3, 1).reshape(b, SB_WIDTH, past) for a in (cache_k, cache_v))
        o_sb = _sb_cached(q, k, v, cache_kt, cache_vt, kn=_largest_tile(past, 256))
        k_out, v_out = k.reshape(b, t, *heads), v.reshape(b, t, *heads)
    c = _largest_tile(t, 128)
    o_hg, s_new = _hgrn(hq, hf, hi, lb_raw, hg_norm_g.reshape(-1), s0, layer=layer, c=c,
                        group=_largest_tile(t // c, 8), v_t=cache_k is None)
    flat = lambda a: a.reshape(n, a.shape[-1])
    y = _out_stage(flat(x), flat(o_sb), flat(o_hg), g_pre, w_in, b_gate, w_a, w_b, w_out, g_post,
                   tm=_largest_tile(n, 256))
    return y.reshape(b, t, d), k_out, v_out, s_new


def kernel(x_prompt, x_sample, cache_sb_k, cache_sb_v, state_hgrn, g_pre, w_in, b_gate, hg_lb_raw,
           hg_norm_g, w_branch_a, w_branch_b, w_out, g_post):
    depth = w_in.shape[0]
    y_p, y_s = x_prompt, x_sample
    outs = [[] for _ in range(6)]
    for l in range(depth):
        weights = (g_pre[l], w_in[l].astype(BF16), b_gate[l], hg_lb_raw, hg_norm_g[l],
                   w_branch_a[l].astype(BF16), w_branch_b[l].astype(BF16), w_out[l].astype(BF16), g_post[l])
        y_p, kp, vp, sp = _trunk_layer(y_p, None, None, None, l, *weights)
        y_s, ks, vs, ss = _trunk_layer(y_s, cache_sb_k[l], cache_sb_v[l], state_hgrn[l], l, *weights)
        for lst, a in zip(outs, (kp, vp, sp, ks, vs, ss)):
            lst.append(a)
    return (y_p, y_s) + tuple(jnp.stack(lst) for lst in outs)
```

```python
import functools
import math

import numpy as np
import jax
import jax.numpy as jnp
from jax import lax
from jax.experimental import pallas as pl
from jax.experimental.pallas import tpu as pltpu

SB_HEADS = 8
SB_HEAD_DIM = 64
SB_WIDTH = SB_HEADS * SB_HEAD_DIM
HG_HEADS = 4
HG_KEY_DIM = 128
HG_VAL_DIM = 128
HG_WIDTH = HG_HEADS * HG_KEY_DIM
HG_BLOCK = 16
NORM_EPS = 1e-6

V7X_LANES = 128
V7X_VMEM_LIMIT = 56 * 1024 * 1024
SB_PAIRS = SB_WIDTH // V7X_LANES

SB_LOG2_UNDERFLOW = 150.5
LOG2_E = math.log2(math.e)

F32 = jnp.float32
BF16 = jnp.bfloat16

_NT = (((1,), (1,)), ((), ()))
_TN = (((0,), (0,)), ((), ()))


def _dot(a, b):
    return jnp.dot(a, b, preferred_element_type=F32)


def _dot_nt(a, b):
    return lax.dot_general(a, b, _NT, preferred_element_type=F32)


def _dot_tn(a, b):
    return lax.dot_general(a, b, _TN, preferred_element_type=F32)


def _split_bf16(x):
    hi = x.astype(BF16)
    lo = (x - hi.astype(F32)).astype(BF16)
    return hi, lo


IN_Q, IN_K, IN_V, IN_ZA, IN_HG_Q, IN_HG_F, IN_HG_V, IN_HG_Z, IN_GATE_A, IN_GATE_B = range(10)
IN_MIXER_GROUPS = (IN_Q, IN_K, IN_V, IN_HG_Q, IN_HG_F, IN_HG_V)
IN_GATE_GROUPS = (IN_ZA, IN_HG_Z, IN_GATE_A, IN_GATE_B)


def _in_widths(d_model):
    widths = (SB_WIDTH,) * 4 + (HG_WIDTH,) * 4 + (d_model, d_model)
    return widths, np.concatenate([[0], np.cumsum(widths)])


def _rms_norm_bf16(x, g):
    ms = jnp.mean(x * x, axis=-1, keepdims=True)
    return (x * lax.rsqrt(ms + NORM_EPS) * g).astype(BF16)


def _in_proj_kernel(*refs, groups, t_groups, widths, columns, q_scale):
    if t_groups:
        x_ref, g_ref, w_ref, wt_ref = refs[:4]
        out_refs = refs[4:]
    else:
        x_ref, g_ref, w_ref = refs[:3]
        out_refs = refs[3:]
    xn = _rms_norm_bf16(x_ref[0], g_ref[...])
    if t_groups:
        ht = _dot_nt(wt_ref[...], xn)
        off = 0
        for idx in t_groups:
            out_refs[groups.index(idx)][0] = ht[off:off + widths[idx]]
            off += widths[idx]
    for ref, idx in zip(out_refs, groups):
        if idx not in t_groups:
            h = _dot(xn, w_ref[:, columns[idx][0]:columns[idx][1]])
            if idx == IN_Q:
                h = h * q_scale
            ref[0] = h.astype(ref.dtype)


def _in_proj(x, g_pre, w_in_bf16, *, tm, t_groups):
    b, t, d = x.shape
    widths, offsets = _in_widths(d)
    offsets = tuple(int(o) for o in offsets)
    assert offsets[-1] == w_in_bf16.shape[1] and t % tm == 0
    out_shape, out_specs = [], []
    for idx in IN_MIXER_GROUPS:
        w = widths[idx]
        dt = BF16 if idx == IN_Q else F32
        if idx in t_groups:
            out_shape.append(jax.ShapeDtypeStruct((b, w, t), dt))
            out_specs.append(pl.BlockSpec((1, w, tm), lambda bi, ti: (bi, 0, ti)))
        else:
            out_shape.append(jax.ShapeDtypeStruct((b, t, w), dt))
            out_specs.append(pl.BlockSpec((1, tm, w), lambda bi, ti: (bi, ti, 0)))
    columns, parts, start = {}, [], 0
    for idx in IN_MIXER_GROUPS:
        if idx not in t_groups:
            parts.append(w_in_bf16[:, offsets[idx]:offsets[idx + 1]])
            columns[idx] = (start, start + widths[idx])
            start += widths[idx]
    w_used = jnp.concatenate(parts, axis=1)
    in_specs = [
        pl.BlockSpec((1, tm, d), lambda bi, ti: (bi, ti, 0)),
        pl.BlockSpec((1, d), lambda bi, ti: (0, 0)),
        pl.BlockSpec(w_used.shape, lambda bi, ti: (0, 0)),
    ]
    args = [x, g_pre.reshape(1, d), w_used]
    if t_groups:
        w_t = jnp.concatenate([w_in_bf16[:, offsets[i]:offsets[i + 1]] for i in t_groups], axis=1).T
        in_specs.append(pl.BlockSpec(w_t.shape, lambda bi, ti: (0, 0)))
        args.append(w_t)
    return pl.pallas_call(
        functools.partial(_in_proj_kernel, groups=IN_MIXER_GROUPS, t_groups=t_groups, widths=widths,
                          columns=columns, q_scale=SB_HEAD_DIM ** -0.5 * LOG2_E),
        grid=(b, t // tm),
        in_specs=in_specs,
        out_specs=tuple(out_specs),
        out_shape=tuple(out_shape),
        compiler_params=pltpu.CompilerParams(
            dimension_semantics=("arbitrary", "arbitrary"), vmem_limit_bytes=V7X_VMEM_LIMIT),
        name="in_proj",
    )(*args)


def _suffix_ones(n, copies):
    r = np.arange(n)
    u = (r[:, None] >= r[None, :]).astype(np.float32)
    return jnp.asarray(np.concatenate([u] * copies, axis=0), dtype=BF16)


def _sb_weights(z, u2, carry, mask):
    sp = jnp.maximum(z, 0.0) + jnp.log(1.0 + jnp.exp2(-jnp.abs(z))) * LOG2_E
    if mask is not None:
        sp = jnp.where(mask, sp, 0.0)
    hi, lo = _split_bf16(sp)
    tot = _dot(jnp.concatenate([hi, lo], axis=1), u2) + carry
    a = jnp.exp2(z - tot)
    if mask is not None:
        a = jnp.where(mask, a, 0.0)
    return a.astype(BF16), tot[:, 0:1]


def _head_split(x2, first):
    zero = jnp.zeros_like(x2)
    return jnp.where(first, x2, zero), jnp.where(first, zero, x2)


def _sb_prompt_kernel(q_ref, kt_ref, vt_ref, u_ref, o_ref, kt_scr, ve_scr, vo_scr, acc_scr, car_scr,
                      *, tq, blocks):
    step = pl.program_id(1)
    n_chunks = kt_scr.shape[0]

    @pl.when(step == 0)
    def _():
        row = lax.broadcasted_iota(jnp.int32, (SB_WIDTH, tq), 0)
        even = (row // SB_HEAD_DIM) % 2 == 0
        for n in range(n_chunks):
            kt_scr[n] = kt_ref[0, :, n * tq:(n + 1) * tq].astype(BF16)
            v = vt_ref[0, :, n * tq:(n + 1) * tq]
            ve_scr[n] = jnp.where(even, v, 0.0).astype(BF16)
            vo_scr[n] = jnp.where(even, 0.0, v).astype(BF16)

    late = tq - tq // 4

    def first(rows):
        return lax.broadcasted_iota(jnp.int32, (rows, V7X_LANES), 1) < SB_HEAD_DIM

    def sweep(jobs, row0):
        qs, z, sp16, tot, wts, pv = {}, {}, {}, {}, {}, {}
        masks = []
        for j, (n, (r0, r1), (k0, k1), offset, fresh) in enumerate(jobs):
            if offset is None:
                masks.append(None)
            else:
                r = lax.broadcasted_iota(jnp.int32, (r1 - r0, k1 - k0), 0)
                c = lax.broadcasted_iota(jnp.int32, (r1 - r0, k1 - k0), 1)
                masks.append(c < r + offset)
            for p in range(SB_PAIRS):
                q2 = q_ref[0, row0 + r0:row0 + r1, V7X_LANES * p:V7X_LANES * (p + 1)]
                qs[j, 2 * p], qs[j, 2 * p + 1] = _head_split(q2, first(r1 - r0))

        def pair_rows(h):
            return slice(V7X_LANES * (h // 2), V7X_LANES * (h // 2 + 1))

        def scores(j, h):
            n, _, (k0, k1), _, _ = jobs[j]
            z[j, h] = _dot(qs.pop((j, h)), kt_scr[n, pair_rows(h), k0:k1])

        def softplus(j, h):
            zz = z[j, h]
            sp = jnp.maximum(zz, 0.0) + jnp.log(1.0 + jnp.exp2(-jnp.abs(zz))) * LOG2_E
            if masks[j] is not None:
                sp = jnp.where(masks[j], sp, 0.0)
            sp16[j, h] = sp.astype(BF16)

        def suffix_sums(j, h):
            _, (r0, r1), (k0, k1), _, fresh = jobs[j]
            carry = jnp.zeros((r1 - r0, 1), F32) if fresh else car_scr[h, r0:r1]
            tot[j, h] = _dot(sp16.pop((j, h)), u_ref[:k1 - k0, :k1 - k0]) + carry

        def weights(j, h):
            a = jnp.exp2(z.pop((j, h)) - tot[j, h])
            if masks[j] is not None:
                a = jnp.where(masks[j], a, 0.0)
            wts[j, h] = a.astype(BF16)
            tot[j, h] = tot[j, h][:, 0:1]

        def weighted_values(j, h):
            n, _, (k0, k1), _, _ = jobs[j]
            v_scr = vo_scr if h % 2 else ve_scr
            pv[j, h] = _dot_nt(wts.pop((j, h)), v_scr[n, pair_rows(h), k0:k1])

        stages = (scores, softplus, suffix_sums, weights, weighted_values)
        n_items = len(jobs) * SB_HEADS
        for step in range(n_items + len(stages) - 1):
            for k, stage in enumerate(stages):
                if 0 <= step - k < n_items:
                    stage(*divmod(step - k, SB_HEADS))

        for j, (n, (r0, r1), _, _, fresh) in enumerate(jobs):
            for p in range(SB_PAIRS):
                new = pv[j, 2 * p] + pv[j, 2 * p + 1]
                acc_scr[p, r0:r1] = new if fresh else acc_scr[p, r0:r1] + new
            for h in range(SB_HEADS):
                car_scr[h, r0:r1] = tot[j, h]

    def lowest_total(r0, r1):
        low = car_scr[0, r0:r1]
        for h in range(1, SB_HEADS):
            low = jnp.minimum(low, car_scr[h, r0:r1])
        return jnp.min(low)

    def cond(st):
        n, low_early, low_late = st
        return jnp.logical_and(n >= 0, jnp.minimum(low_early, low_late) <= SB_LOG2_UNDERFLOW)

    for blk in range(blocks):
        i = step * blocks + blk
        row0 = blk * tq
        sweep([(i, (0, tq), (0, tq), 0, True)], row0)

        def older(n, r0, r1, low, row0=row0):
            def run():
                sweep([(n, (r0, r1), (0, tq), None, False)], row0)
                return lowest_total(r0, r1)
            return lax.cond(low <= SB_LOG2_UNDERFLOW, run, lambda: low)

        def body(st, older=older):
            n, low_early, low_late = st
            return n - 1, older(n, 0, late, low_early), older(n, late, tq, low_late)

        lax.while_loop(cond, body, (i - 1, lowest_total(0, late), lowest_total(late, tq)))
        for p in range(SB_PAIRS):
            o_ref[0, row0:row0 + tq, V7X_LANES * p:V7X_LANES * (p + 1)] = acc_scr[p]


def _sb_prompt(q, kt, vt, *, tq, blocks):
    b, t, w = q.shape
    assert w == SB_WIDTH and t % (tq * blocks) == 0 and kt.shape == (b, w, t)
    n_chunks = t // tq
    kv_spec = pl.BlockSpec((1, w, t), lambda bi, i: (bi, 0, 0))
    q_spec = pl.BlockSpec((1, tq * blocks, w), lambda bi, i: (bi, i, 0))
    return pl.pallas_call(
        functools.partial(_sb_prompt_kernel, tq=tq, blocks=blocks),
        grid=(b, n_chunks // blocks),
        in_specs=[q_spec, kv_spec, kv_spec, pl.BlockSpec((tq, tq), lambda bi, i: (0, 0))],
        out_specs=q_spec,
        out_shape=jax.ShapeDtypeStruct((b, t, w), F32),
        scratch_shapes=[pltpu.VMEM((n_chunks, w, tq), BF16)] * 3
        + [pltpu.VMEM((SB_PAIRS, tq, V7X_LANES), F32), pltpu.VMEM((SB_HEADS, tq, 1), F32)],
        compiler_params=pltpu.CompilerParams(
            dimension_semantics=("arbitrary", "arbitrary"), vmem_limit_bytes=V7X_VMEM_LIMIT),
        name="sb_prompt",
    )(q, kt, vt, _suffix_ones(tq, 1))


def _sb_cached_kernel(q_ref, k_ref, v_ref, ckt_ref, cvt_ref, ud_ref, uc_ref, o_ref, acc_scr, car_scr,
                      *, tq, kn, n_chunks):
    first = lax.broadcasted_iota(jnp.int32, (tq, V7X_LANES), 1) < SB_HEAD_DIM
    r = lax.broadcasted_iota(jnp.int32, (tq, tq), 0)
    c = lax.broadcasted_iota(jnp.int32, (tq, tq), 1)
    causal = c < r
    even = (lax.broadcasted_iota(jnp.int32, (V7X_LANES, kn), 0) // SB_HEAD_DIM) % 2 == 0
    qs = []
    for p in range(SB_PAIRS):
        qs.extend(_head_split(q_ref[0, :, V7X_LANES * p:V7X_LANES * (p + 1)], first))

    def pair_rows(h):
        return slice(V7X_LANES * (h // 2), V7X_LANES * (h // 2 + 1))

    def sweep(n):
        z, wts, tot, pv = {}, {}, {}, {}

        def scores(h):
            if n is None:
                z[h] = _dot_nt(qs[h], k_ref[0, :, pair_rows(h)].astype(BF16))
            else:
                z[h] = _dot(qs[h], ckt_ref[0, pair_rows(h), n * kn:(n + 1) * kn].astype(BF16))

        def weights(h):
            if n is None:
                wts[h], tot[h] = _sb_weights(z.pop(h), ud_ref[...], jnp.zeros((tq, 1), F32), causal)
            else:
                wts[h], tot[h] = _sb_weights(z.pop(h), uc_ref[...], car_scr[h], None)

        def weighted_values(h):
            if n is None:
                v = _head_split(v_ref[0, :, pair_rows(h)], first)[h % 2]
                pv[h] = _dot(wts.pop(h), v.astype(BF16))
            else:
                v = cvt_ref[0, pair_rows(h), n * kn:(n + 1) * kn]
                v = jnp.where(even, 0.0, v) if h % 2 else jnp.where(even, v, 0.0)
                pv[h] = _dot_nt(wts.pop(h), v.astype(BF16))

        stages = (scores, weights, weighted_values)
        for step in range(SB_HEADS + len(stages) - 1):
            for k, stage in enumerate(stages):
                if 0 <= step - k < SB_HEADS:
                    stage(step - k)
        low = None
        for p in range(SB_PAIRS):
            new = pv[2 * p] + pv[2 * p + 1]
            acc_scr[p] = new if n is None else acc_scr[p] + new
        for h in range(SB_HEADS):
            car_scr[h] = tot[h]
            low = tot[h] if low is None else jnp.minimum(low, tot[h])
        return jnp.min(low)

    lowest = sweep(None)
    for n in range(n_chunks - 1, -1, -1):
        lowest = lax.cond(lowest <= SB_LOG2_UNDERFLOW, functools.partial(sweep, n), lambda low=lowest: low)
    for p in range(SB_PAIRS):
        o_ref[0, :, V7X_LANES * p:V7X_LANES * (p + 1)] = acc_scr[p]


def _sb_cached(q, k, v, cache_kt, cache_vt, *, kn):
    b, tq, w = q.shape
    past = cache_kt.shape[2]
    assert w == SB_WIDTH and past % kn == 0 and cache_kt.shape == (b, w, past)
    new_spec = pl.BlockSpec((1, tq, w), lambda bi: (bi, 0, 0))
    cache_spec = pl.BlockSpec((1, w, past), lambda bi: (bi, 0, 0))
    return pl.pallas_call(
        functools.partial(_sb_cached_kernel, tq=tq, kn=kn, n_chunks=past // kn),
        grid=(b,),
        in_specs=[new_spec, new_spec, new_spec, cache_spec, cache_spec,
                  pl.BlockSpec((2 * tq, tq), lambda bi: (0, 0)),
                  pl.BlockSpec((2 * kn, kn), lambda bi: (0, 0))],
        out_specs=new_spec,
        out_shape=jax.ShapeDtypeStruct((b, tq, w), F32),
        scratch_shapes=[pltpu.VMEM((SB_PAIRS, tq, V7X_LANES), F32), pltpu.VMEM((SB_HEADS, tq, 1), F32)],
        compiler_params=pltpu.CompilerParams(
            dimension_semantics=("arbitrary",), vmem_limit_bytes=V7X_VMEM_LIMIT),
        name="sb_cached",
    )(q, k, v, cache_kt, cache_vt, _suffix_ones(tq, 2), _suffix_ones(kn, 2))


def _hgrn_consts(c):
    t = np.arange(c)
    sub = t // HG_BLOCK
    same = sub[:, None] == sub[None, :]
    tri = (same & (t[None, :] <= t[:, None])).astype(np.float32)
    ones = same.astype(np.float32)
    x = sub[:, None] ^ sub[None, :]
    level = np.where(x == 0, 0, np.floor(np.log2(np.maximum(x, 1))).astype(np.int64) + 1)
    level = np.where(t[None, :] <= t[:, None], level, -1).astype(np.int32)
    pre = np.block([[tri, tri], [ones, ones]])
    return jnp.asarray(pre, dtype=BF16), jnp.asarray(level)


def _prod(vecs):
    out = None
    for v in vecs:
        out = v if out is None else out * v
    return out


def _rows(vecs, width):
    one = jnp.ones((HG_BLOCK, width), F32)
    return jnp.concatenate(
        [one if v is None else jnp.broadcast_to(v, (HG_BLOCK, width)) for v in vecs], axis=0)


def _hgrn_group(read, write, lb, g, pre, lvl, states, *, c, group, v_t):
    width = HG_WIDTH
    nsub = c // HG_BLOCK
    levels = int(round(math.log2(nsub)))
    states = list(states)
    level_masks = [lvl == l for l in range(levels + 1)]
    ch = [dict() for _ in range(group)]
    item = {}

    def head(h):
        return slice(HG_KEY_DIM * h, HG_KEY_DIM * (h + 1))

    def gates(gi):
        hq, hf, v = read(gi)
        f = lb + (1.0 - lb) * jax.nn.sigmoid(hf)
        ch[gi].update(hq=hq, kb=1.0 - f,
                      v_bf=v.astype(BF16), split=jnp.concatenate(_split_bf16(jnp.log(f)), axis=0))

    def prefix(gi):
        ch[gi]["sums"] = _dot(pre, ch[gi].pop("split"))

    def decays(gi):
        sums, hq, kb = (ch[gi].pop(k) for k in ("sums", "hq", "kb"))
        b16 = sums[:c]
        bend = sums[c:]
        q_dec = hq * jnp.exp(b16)
        dmat = jnp.exp(bend)
        k_inv = kb * jnp.exp(-b16)
        k_end = k_inv * dmat
        d = [dmat[HG_BLOCK * i:HG_BLOCK * i + 1, :] for i in range(nsub)]
        q_lv = [q_dec.astype(BF16)]
        k_lv = [k_inv.astype(BF16)]
        if levels >= 1:
            q_lv.append(q_lv[0])
            k_lv.append(k_end.astype(BF16))
        for l in range(2, levels + 1):
            half = 2 ** (l - 1)
            alpha, beta = [], []
            for i in range(nsub):
                mid = (i // (2 * half)) * 2 * half + half
                alpha.append(_prod(d[mid:i]) if i > mid else None)
                beta.append(_prod(d[i + 1:mid]) if i + 1 < mid else None)
            q_lv.append((q_dec * _rows(alpha, width)).astype(BF16))
            k_lv.append((k_end * _rows(beta, width)).astype(BF16))
        if nsub > 1:
            alpha_c = [_prod(d[:i]) if i > 0 else None for i in range(nsub)]
            beta_c = [_prod(d[i + 1:]) if i + 1 < nsub else None for i in range(nsub)]
            q_c = (q_dec * _rows(alpha_c, width)).astype(BF16)
            k_c = (k_end * _rows(beta_c, width)).astype(BF16)
        else:
            q_c = q_lv[0]
            k_c = k_end.astype(BF16)
        ch[gi].update(q_lv=q_lv, k_lv=k_lv, q_c=q_c, k_c=k_c, d_all=_prod(d))

    def level_scores(gi, h):
        q_lv, k_lv = ch[gi]["q_lv"], ch[gi]["k_lv"]
        item[gi, h] = [_dot_nt(q_lv[l][:, head(h)], k_lv[l][:, head(h)]) for l in range(levels + 1)]

    def combine(gi, h):
        att = jnp.zeros((c, c), F32)
        for l, scores in enumerate(item[gi, h]):
            att = jnp.where(level_masks[l], scores, att)
        item[gi, h] = att.astype(BF16)

    def outputs(gi, h):
        att = item[gi, h]
        v_bf, q_c, k_c = ch[gi]["v_bf"], ch[gi]["q_c"], ch[gi]["k_c"]
        if v_t:
            vt_h = v_bf[head(h), :]
            o_intra = _dot_nt(att, vt_h)
            st_add = _dot(vt_h, k_c[:, head(h)])
        else:
            o_intra = _dot(att, v_bf[:, head(h)])
            st_add = _dot_tn(v_bf[:, head(h)], k_c[:, head(h)])
        item[gi, h] = (o_intra + _dot_nt(q_c[:, head(h)], states[h].astype(BF16)), st_add)

    def finish(gi, h):
        o, st_add = item.pop((gi, h))
        ms = jnp.mean(o * o, axis=-1, keepdims=True)
        write(gi, h, o * lax.rsqrt(ms + NORM_EPS) * g[:, head(h)])
        states[h] = states[h] * ch[gi]["d_all"][:, head(h)] + st_add

    chunk_stages = (gates, prefix, decays)
    item_stages = (level_scores, combine, outputs, finish)
    n_items = group * HG_HEADS
    for step in range(-len(chunk_stages), n_items + len(item_stages) - 1):
        for gi in range(group):
            for k, stage in enumerate(chunk_stages):
                if step == HG_HEADS * gi - len(chunk_stages) + k:
                    stage(gi)
        for k, stage in enumerate(item_stages):
            if 0 <= step - k < n_items:
                stage(*divmod(step - k, HG_HEADS))
    return states


def _hgrn_kernel(*refs, c, group, layer, has_s0, v_t, n_steps):
    if has_s0:
        (hq_ref, hf_ref, hi_ref, lbraw_ref, g_ref, pre_ref, lvl_ref, s0_ref,
         o_ref, sout_ref, st_scr) = refs
    else:
        (hq_ref, hf_ref, hi_ref, lbraw_ref, g_ref, pre_ref, lvl_ref,
         o_ref, sout_ref, st_scr) = refs
        s0_ref = None
    ci = pl.program_id(1)

    @pl.when(ci == 0)
    def _():
        for h in range(HG_HEADS):
            if has_s0:
                st_scr[h] = s0_ref[0, h].T
            else:
                st_scr[h] = jnp.zeros((HG_VAL_DIM, HG_KEY_DIM), F32)

    raw = lbraw_ref[...]
    e = jnp.exp(raw - jnp.max(raw, axis=0, keepdims=True))
    lb = jnp.sum(e[:layer + 1], axis=0, keepdims=True) / jnp.sum(e, axis=0, keepdims=True)

    pre = pre_ref[...]
    lvl = lvl_ref[...]
    g = g_ref[...]
    def read(gi):
        rs = slice(gi * c, (gi + 1) * c)
        return hq_ref[0, rs, :], hf_ref[0, rs, :], hi_ref[0, :, rs] if v_t else hi_ref[0, rs, :]

    def write(gi, h, o):
        o_ref[0, gi * c:(gi + 1) * c, HG_VAL_DIM * h:HG_VAL_DIM * (h + 1)] = o

    states = _hgrn_group(read, write, lb, g, pre, lvl, [st_scr[h] for h in range(HG_HEADS)],
                         c=c, group=group, v_t=v_t)
    for h in range(HG_HEADS):
        st_scr[h] = states[h]

    @pl.when(ci == n_steps - 1)
    def _():
        for h in range(HG_HEADS):
            sout_ref[0, h] = st_scr[h].T


def _hgrn(hq, hf, hi, lb_raw, norm_g, s0, *, layer, c, group, v_t):
    b, t, w = hq.shape
    span = c * group
    assert w == HG_WIDTH and t % span == 0 and c % HG_BLOCK == 0
    n_steps = t // span
    pre, level = _hgrn_consts(c)
    tok = pl.BlockSpec((1, span, w), lambda bi, ci: (bi, ci, 0))
    tok_t = pl.BlockSpec((1, w, span), lambda bi, ci: (bi, 0, ci))
    const2 = lambda shape: pl.BlockSpec(shape, lambda bi, ci: (0, 0))
    st_spec = pl.BlockSpec((1, HG_HEADS, HG_KEY_DIM, HG_VAL_DIM), lambda bi, ci: (bi, 0, 0, 0))
    in_specs = [tok, tok, tok_t if v_t else tok, const2(lb_raw.shape), const2((1, w)),
                const2(pre.shape), const2(level.shape)]
    args = [hq, hf, hi, lb_raw, norm_g.reshape(1, w), pre, level]
    if s0 is not None:
        in_specs.append(st_spec)
        args.append(s0)
    return pl.pallas_call(
        functools.partial(_hgrn_kernel, c=c, group=group, layer=layer, has_s0=s0 is not None, v_t=v_t,
                          n_steps=n_steps),
        grid=(b, n_steps),
        in_specs=in_specs,
        out_specs=(tok, st_spec),
        out_shape=(jax.ShapeDtypeStruct((b, t, w), F32),
                   jax.ShapeDtypeStruct((b, HG_HEADS, HG_KEY_DIM, HG_VAL_DIM), F32)),
        scratch_shapes=[pltpu.VMEM((HG_HEADS, HG_VAL_DIM, HG_KEY_DIM), F32)],
        compiler_params=pltpu.CompilerParams(
            dimension_semantics=("arbitrary", "arbitrary"), vmem_limit_bytes=V7X_VMEM_LIMIT),
        name="hgrn2",
    )(*args)


def _out_kernel(x_ref, osb_ref, ohg_ref, gpre_ref, wg_ref, bg_ref, wa_ref, wb_ref, wo_ref, gp_ref, y_ref,
                *, tm, tiles):
    d_model = x_ref.shape[-1]
    off = np.cumsum([0, SB_WIDTH, HG_WIDTH, d_model, d_model])
    bg = bg_ref[...]
    for t in range(tiles):
        rows = slice(t * tm, (t + 1) * tm)
        x = x_ref[rows, :]
        xn = _rms_norm_bf16(x, gpre_ref[...])
        z_a, hz, ga, gb = (_dot(xn, wg_ref[:, off[i]:off[i + 1]]) for i in range(4))
        u_a = _dot((osb_ref[rows, :] * jax.nn.silu(z_a)).astype(BF16), wa_ref[...])
        u_b = _dot((ohg_ref[rows, :] * jax.nn.silu(hz)).astype(BF16), wb_ref[...])
        merged = (jax.nn.sigmoid(ga + bg[:, :d_model]) * u_a + jax.nn.sigmoid(gb + bg[:, d_model:]) * u_b)
        m = _dot(merged.astype(BF16), wo_ref[...])
        ms = jnp.mean(m * m, axis=-1, keepdims=True)
        y_ref[rows, :] = x + m * lax.rsqrt(ms + NORM_EPS) * gp_ref[...]


def _out_stage(x2d, o_sb, o_hg, g_pre, w_in_bf16, b_gate, wa, wb, wo, g_post, *, tm, tiles):
    n, d = x2d.shape
    span = tm * tiles
    assert n % span == 0
    _, offsets = _in_widths(d)
    w_gate = jnp.concatenate([w_in_bf16[:, offsets[i]:offsets[i + 1]] for i in IN_GATE_GROUPS], axis=1)
    row = lambda w: pl.BlockSpec((span, w), lambda i: (i, 0))
    const = lambda shape: pl.BlockSpec(shape, lambda i: (0, 0))
    return pl.pallas_call(
        functools.partial(_out_kernel, tm=tm, tiles=tiles),
        grid=(n // span,),
        in_specs=[row(d), row(SB_WIDTH), row(HG_WIDTH), const((1, d)), const(w_gate.shape),
                  const((1, 2 * d)), const(wa.shape), const(wb.shape), const(wo.shape), const((1, d))],
        out_specs=row(d),
        out_shape=jax.ShapeDtypeStruct((n, d), F32),
        compiler_params=pltpu.CompilerParams(
            dimension_semantics=("arbitrary",), vmem_limit_bytes=V7X_VMEM_LIMIT),
        name="out_stage",
    )(x2d, o_sb, o_hg, g_pre.reshape(1, d), w_gate, b_gate.reshape(1, 2 * d), wa, wb, wo, g_post.reshape(1, d))


def _largest_tile(n, cap):
    t = cap
    while n % t:
        t //= 2
    return t


def _trunk_layer(x, cache_k, cache_v, s0, layer, g_pre, w_in, b_gate, lb_raw, hg_norm_g,
                 w_a, w_b, w_out, g_post):
    b, t, d = x.shape
    n = b * t
    heads = (SB_HEADS, SB_HEAD_DIM)
    if cache_k is None:
        q, kt, vt, hq, hf, hi = _in_proj(x, g_pre, w_in, tm=_largest_tile(t, 1024), t_groups=(IN_K, IN_V, IN_HG_V))
        tq = _largest_tile(t, 256)
        o_sb = _sb_prompt(q, kt, vt, tq=tq, blocks=_largest_tile(t // tq, 8))
        k_out, v_out = (a.reshape(b, *heads, t).transpose(0, 3, 1, 2) for a in (kt, vt))
    else:
        tm = _largest_tile(n, 256)
        outs = _in_proj(x.reshape(1, n, d), g_pre, w_in, tm=tm, t_groups=())
        q, k, v, hq, hf, hi = (a.reshape(b, t, a.shape[-1]) for a in outs)
        past = cache_k.shape[1]
        cache_kt, cache_vt = (a.transpose(0, 2, 3, 1).reshape(b, SB_WIDTH, past) for a in (cache_k, cache_v))
        o_sb = _sb_cached(q, k, v, cache_kt, cache_vt, kn=_largest_tile(past, 256))
        k_out, v_out = k.reshape(b, t, *heads), v.reshape(b, t, *heads)
    c = _largest_tile(t, 128)
    o_hg, s_new = _hgrn(hq, hf, hi, lb_raw, hg_norm_g.reshape(-1), s0, layer=layer, c=c,
                        group=_largest_tile(t // c, 16), v_t=cache_k is None)
    flat = lambda a: a.reshape(n, a.shape[-1])
    tm = _largest_tile(n, 256)
    y = _out_stage(flat(x), flat(o_sb), flat(o_hg), g_pre, w_in, b_gate, w_a, w_b, w_out, g_post,
                   tm=tm, tiles=_largest_tile(n // tm, 4))
    return y.reshape(b, t, d), k_out, v_out, s_new


def kernel(x_prompt, x_sample, cache_sb_k, cache_sb_v, state_hgrn, g_pre, w_in, b_gate, hg_lb_raw,
           hg_norm_g, w_branch_a, w_branch_b, w_out, g_post):
    depth = w_in.shape[0]
    y_p, y_s = x_prompt, x_sample
    outs = [[] for _ in range(6)]
    for l in range(depth):
        weights = (g_pre[l], w_in[l].astype(BF16), b_gate[l], hg_lb_raw, hg_norm_g[l],
                   w_branch_a[l].astype(BF16), w_branch_b[l].astype(BF16), w_out[l].astype(BF16), g_post[l])
        y_p, kp, vp, sp = _trunk_layer(y_p, None, None, None, l, *weights)
        y_s, ks, vs, ss = _trunk_layer(y_s, cache_sb_k[l], cache_sb_v[l], state_hgrn[l], l, *weights)
        for lst, a in zip(outs, (kp, vp, sp, ks, vs, ss)):
            lst.append(a)
    return (y_p, y_s) + tuple(jnp.stack(lst) for lst in outs)
```

```python
import functools
import math

import numpy as np
import jax
import jax.numpy as jnp
from jax import lax
from jax.experimental import pallas as pl
from jax.experimental.pallas import tpu as pltpu

SB_HEADS = 8
SB_HEAD_DIM = 64
SB_WIDTH = SB_HEADS * SB_HEAD_DIM
HG_HEADS = 4
HG_KEY_DIM = 128
HG_VAL_DIM = 128
HG_WIDTH = HG_HEADS * HG_KEY_DIM
HG_BLOCK = 16
NORM_EPS = 1e-6

V7X_LANES = 128
V7X_VMEM_LIMIT = 56 * 1024 * 1024
SB_PAIRS = SB_WIDTH // V7X_LANES

SB_LOG2_UNDERFLOW = 150.5
LOG2_E = math.log2(math.e)

F32 = jnp.float32
BF16 = jnp.bfloat16

_NT = (((1,), (1,)), ((), ()))
_TN = (((0,), (0,)), ((), ()))


def _dot(a, b):
    return jnp.dot(a, b, preferred_element_type=F32)


def _dot_nt(a, b):
    return lax.dot_general(a, b, _NT, preferred_element_type=F32)


def _dot_tn(a, b):
    return lax.dot_general(a, b, _TN, preferred_element_type=F32)


def _split_bf16(x):
    hi = x.astype(BF16)
    lo = (x - hi.astype(F32)).astype(BF16)
    return hi, lo


IN_Q, IN_K, IN_V, IN_ZA, IN_HG_Q, IN_HG_F, IN_HG_V, IN_HG_Z, IN_GATE_A, IN_GATE_B = range(10)
IN_MIXER_GROUPS = (IN_Q, IN_K, IN_V, IN_HG_Q, IN_HG_F, IN_HG_V)
IN_GATE_GROUPS = (IN_ZA, IN_HG_Z, IN_GATE_A, IN_GATE_B)


def _in_widths(d_model):
    widths = (SB_WIDTH,) * 4 + (HG_WIDTH,) * 4 + (d_model, d_model)
    return widths, np.concatenate([[0], np.cumsum(widths)])


def _rms_norm_bf16(x, g):
    ms = jnp.mean(x * x, axis=-1, keepdims=True)
    return (x * lax.rsqrt(ms + NORM_EPS) * g).astype(BF16)


def _in_proj_kernel(*refs, groups, t_groups, widths, columns, q_scale):
    if t_groups:
        x_ref, g_ref, w_ref, wt_ref = refs[:4]
        out_refs = refs[4:]
    else:
        x_ref, g_ref, w_ref = refs[:3]
        out_refs = refs[3:]
    xn = _rms_norm_bf16(x_ref[0], g_ref[...])
    if t_groups:
        ht = _dot_nt(wt_ref[...], xn)
        off = 0
        for idx in t_groups:
            out_refs[groups.index(idx)][0] = ht[off:off + widths[idx]]
            off += widths[idx]
    for ref, idx in zip(out_refs, groups):
        if idx not in t_groups:
            h = _dot(xn, w_ref[:, columns[idx][0]:columns[idx][1]])
            if idx == IN_Q:
                h = h * q_scale
            ref[0] = h.astype(ref.dtype)


def _in_proj(x, g_pre, w_in_bf16, *, tm, t_groups):
    b, t, d = x.shape
    widths, offsets = _in_widths(d)
    offsets = tuple(int(o) for o in offsets)
    assert offsets[-1] == w_in_bf16.shape[1] and t % tm == 0
    out_shape, out_specs = [], []
    for idx in IN_MIXER_GROUPS:
        w = widths[idx]
        dt = BF16 if idx == IN_Q else F32
        if idx in t_groups:
            out_shape.append(jax.ShapeDtypeStruct((b, w, t), dt))
            out_specs.append(pl.BlockSpec((1, w, tm), lambda bi, ti: (bi, 0, ti)))
        else:
            out_shape.append(jax.ShapeDtypeStruct((b, t, w), dt))
            out_specs.append(pl.BlockSpec((1, tm, w), lambda bi, ti: (bi, ti, 0)))
    columns, parts, start = {}, [], 0
    for idx in IN_MIXER_GROUPS:
        if idx not in t_groups:
            parts.append(w_in_bf16[:, offsets[idx]:offsets[idx + 1]])
            columns[idx] = (start, start + widths[idx])
            start += widths[idx]
    w_used = jnp.concatenate(parts, axis=1)
    in_specs = [
        pl.BlockSpec((1, tm, d), lambda bi, ti: (bi, ti, 0)),
        pl.BlockSpec((1, d), lambda bi, ti: (0, 0)),
        pl.BlockSpec(w_used.shape, lambda bi, ti: (0, 0)),
    ]
    args = [x, g_pre.reshape(1, d), w_used]
    if t_groups:
        w_t = jnp.concatenate([w_in_bf16[:, offsets[i]:offsets[i + 1]] for i in t_groups], axis=1).T
        in_specs.append(pl.BlockSpec(w_t.shape, lambda bi, ti: (0, 0)))
        args.append(w_t)
    return pl.pallas_call(
        functools.partial(_in_proj_kernel, groups=IN_MIXER_GROUPS, t_groups=t_groups, widths=widths,
                          columns=columns, q_scale=SB_HEAD_DIM ** -0.5 * LOG2_E),
        grid=(b, t // tm),
        in_specs=in_specs,
        out_specs=tuple(out_specs),
        out_shape=tuple(out_shape),
        compiler_params=pltpu.CompilerParams(
            dimension_semantics=("arbitrary", "arbitrary"), vmem_limit_bytes=V7X_VMEM_LIMIT),
        name="in_proj",
    )(*args)


def _suffix_ones(n, copies):
    r = np.arange(n)
    u = (r[:, None] >= r[None, :]).astype(np.float32)
    return jnp.asarray(np.concatenate([u] * copies, axis=0), dtype=BF16)


def _sb_weights(z, u2, carry, mask):
    sp = jnp.maximum(z, 0.0) + jnp.log(1.0 + jnp.exp2(-jnp.abs(z))) * LOG2_E
    if mask is not None:
        sp = jnp.where(mask, sp, 0.0)
    hi, lo = _split_bf16(sp)
    tot = _dot(jnp.concatenate([hi, lo], axis=1), u2) + carry
    a = jnp.exp2(z - tot)
    if mask is not None:
        a = jnp.where(mask, a, 0.0)
    return a.astype(BF16), tot[:, 0:1]


def _head_split(x2, first):
    zero = jnp.zeros_like(x2)
    return jnp.where(first, x2, zero), jnp.where(first, zero, x2)


def _sb_prompt_kernel(q_ref, kt_ref, vt_ref, u_ref, o_ref, kt_scr, ve_scr, vo_scr, acc_scr, car_scr,
                      *, tq, blocks):
    step = pl.program_id(1)
    n_chunks = kt_scr.shape[0]

    @pl.when(step == 0)
    def _():
        row = lax.broadcasted_iota(jnp.int32, (SB_WIDTH, tq), 0)
        even = (row // SB_HEAD_DIM) % 2 == 0
        for n in range(n_chunks):
            kt_scr[n] = kt_ref[0, :, n * tq:(n + 1) * tq].astype(BF16)
            v = vt_ref[0, :, n * tq:(n + 1) * tq]
            ve_scr[n] = jnp.where(even, v, 0.0).astype(BF16)
            vo_scr[n] = jnp.where(even, 0.0, v).astype(BF16)

    late = tq - tq // 4

    def first(rows):
        return lax.broadcasted_iota(jnp.int32, (rows, V7X_LANES), 1) < SB_HEAD_DIM

    def sweep(jobs, row0):
        qs, z, sp16, tot, wts, pv = {}, {}, {}, {}, {}, {}
        masks = []
        for j, (n, (r0, r1), diag, start) in enumerate(jobs):
            if diag:
                r = lax.broadcasted_iota(jnp.int32, (r1 - r0, tq), 0)
                c = lax.broadcasted_iota(jnp.int32, (r1 - r0, tq), 1)
                masks.append(c < r + r0)
            else:
                masks.append(None)
            for p in range(SB_PAIRS):
                q2 = q_ref[0, row0 + r0:row0 + r1, V7X_LANES * p:V7X_LANES * (p + 1)]
                qs[j, 2 * p], qs[j, 2 * p + 1] = _head_split(q2, first(r1 - r0))

        def pair_rows(h):
            return slice(V7X_LANES * (h // 2), V7X_LANES * (h // 2 + 1))

        def scores(j, h):
            z[j, h] = _dot(qs.pop((j, h)), kt_scr[jobs[j][0], pair_rows(h), :])

        def softplus(j, h):
            zz = z[j, h]
            sp = jnp.maximum(zz, 0.0) + jnp.log(1.0 + jnp.exp2(-jnp.abs(zz))) * LOG2_E
            if masks[j] is not None:
                sp = jnp.where(masks[j], sp, 0.0)
            sp16[j, h] = sp.astype(BF16)

        def suffix_sums(j, h):
            _, (r0, r1), _, start = jobs[j]
            if start == "zero":
                carry = jnp.zeros((r1 - r0, 1), F32)
            elif start == "scratch":
                carry = car_scr[h, r0:r1]
            else:
                carry = tot[start, h]
            tot[j, h] = _dot(sp16.pop((j, h)), u_ref[...]) + carry

        def weights(j, h):
            a = jnp.exp2(z.pop((j, h)) - tot[j, h])
            if masks[j] is not None:
                a = jnp.where(masks[j], a, 0.0)
            wts[j, h] = a.astype(BF16)
            tot[j, h] = tot[j, h][:, 0:1]

        def weighted_values(j, h):
            v_scr = vo_scr if h % 2 else ve_scr
            pv[j, h] = _dot_nt(wts.pop((j, h)), v_scr[jobs[j][0], pair_rows(h), :])

        stages = (scores, softplus, suffix_sums, weights, weighted_values)
        n_items = len(jobs) * SB_HEADS
        for step in range(n_items + len(stages) - 1):
            for k, stage in enumerate(stages):
                if 0 <= step - k < n_items:
                    stage(*divmod(step - k, SB_HEADS))

        continued = {start for _, _, _, start in jobs if isinstance(start, int)}
        for j, (n, (r0, r1), _, start) in enumerate(jobs):
            if j in continued:
                continue
            chain, k = [j], start
            while isinstance(k, int):
                chain.append(k)
                k = jobs[k][3]
            for p in range(SB_PAIRS):
                new = sum(pv[c, 2 * p] + pv[c, 2 * p + 1] for c in chain)
                acc_scr[p, r0:r1] = new if k == "zero" else acc_scr[p, r0:r1] + new
            for h in range(SB_HEADS):
                car_scr[h, r0:r1] = tot[j, h]

    def lowest_total(r0, r1):
        low = car_scr[0, r0:r1]
        for h in range(1, SB_HEADS):
            low = jnp.minimum(low, car_scr[h, r0:r1])
        return jnp.min(low)

    def cond(st):
        n, low_early, low_late = st
        return jnp.logical_and(n >= 0, jnp.minimum(low_early, low_late) <= SB_LOG2_UNDERFLOW)

    for blk in range(blocks):
        i = blk if blocks == n_chunks else step * blocks + blk
        row0 = blk * tq
        own = [(i, (0, late), True, "zero"), (i, (late, tq), True, "zero")]
        with_previous = own + [(i - 1, (0, late), False, 0)]
        if blocks == n_chunks:
            sweep(with_previous if i > 0 else own, row0)
        else:
            pl.when(i == 0)(functools.partial(sweep, own, row0))
            pl.when(i > 0)(functools.partial(sweep, with_previous, row0))

        def older(n, r0, r1, low, row0=row0):
            def run():
                sweep([(n, (r0, r1), False, "scratch")], row0)
                return lowest_total(r0, r1)
            return lax.cond(low <= SB_LOG2_UNDERFLOW, run, lambda: low)

        def body(st, older=older, i=i):
            n, low_early, low_late = st
            low_early = lax.cond(n == i - 1, lambda: low_early, lambda: older(n, 0, late, low_early))
            return n - 1, low_early, older(n, late, tq, low_late)

        lax.while_loop(cond, body, (i - 1, lowest_total(0, late), lowest_total(late, tq)))
        for p in range(SB_PAIRS):
            o_ref[0, row0:row0 + tq, V7X_LANES * p:V7X_LANES * (p + 1)] = acc_scr[p]


def _sb_prompt(q, kt, vt, *, tq, blocks):
    b, t, w = q.shape
    assert w == SB_WIDTH and t % (tq * blocks) == 0 and kt.shape == (b, w, t)
    n_chunks = t // tq
    kv_spec = pl.BlockSpec((1, w, t), lambda bi, i: (bi, 0, 0))
    q_spec = pl.BlockSpec((1, tq * blocks, w), lambda bi, i: (bi, i, 0))
    return pl.pallas_call(
        functools.partial(_sb_prompt_kernel, tq=tq, blocks=blocks),
        grid=(b, n_chunks // blocks),
        in_specs=[q_spec, kv_spec, kv_spec, pl.BlockSpec((tq, tq), lambda bi, i: (0, 0))],
        out_specs=q_spec,
        out_shape=jax.ShapeDtypeStruct((b, t, w), F32),
        scratch_shapes=[pltpu.VMEM((n_chunks, w, tq), BF16)] * 3
        + [pltpu.VMEM((SB_PAIRS, tq, V7X_LANES), F32), pltpu.VMEM((SB_HEADS, tq, 1), F32)],
        compiler_params=pltpu.CompilerParams(
            dimension_semantics=("arbitrary", "arbitrary"), vmem_limit_bytes=V7X_VMEM_LIMIT),
        name="sb_prompt",
    )(q, kt, vt, _suffix_ones(tq, 1))


def _sb_cached_kernel(q_ref, k_ref, v_ref, ckt_ref, cvt_ref, ud_ref, uc_ref, o_ref, acc_scr, car_scr,
                      *, tq, kn, n_chunks):
    first = lax.broadcasted_iota(jnp.int32, (tq, V7X_LANES), 1) < SB_HEAD_DIM
    r = lax.broadcasted_iota(jnp.int32, (tq, tq), 0)
    c = lax.broadcasted_iota(jnp.int32, (tq, tq), 1)
    causal = c < r
    even = (lax.broadcasted_iota(jnp.int32, (V7X_LANES, kn), 0) // SB_HEAD_DIM) % 2 == 0
    qs = []
    for p in range(SB_PAIRS):
        qs.extend(_head_split(q_ref[0, :, V7X_LANES * p:V7X_LANES * (p + 1)], first))

    def pair_rows(h):
        return slice(V7X_LANES * (h // 2), V7X_LANES * (h // 2 + 1))

    def sweep(n):
        z, wts, tot, pv = {}, {}, {}, {}

        def scores(h):
            if n is None:
                z[h] = _dot_nt(qs[h], k_ref[0, :, pair_rows(h)].astype(BF16))
            else:
                z[h] = _dot(qs[h], ckt_ref[0, pair_rows(h), n * kn:(n + 1) * kn].astype(BF16))

        def weights(h):
            if n is None:
                wts[h], tot[h] = _sb_weights(z.pop(h), ud_ref[...], jnp.zeros((tq, 1), F32), causal)
            else:
                wts[h], tot[h] = _sb_weights(z.pop(h), uc_ref[...], car_scr[h], None)

        def weighted_values(h):
            if n is None:
                v = _head_split(v_ref[0, :, pair_rows(h)], first)[h % 2]
                pv[h] = _dot(wts.pop(h), v.astype(BF16))
            else:
                v = cvt_ref[0, pair_rows(h), n * kn:(n + 1) * kn]
                v = jnp.where(even, 0.0, v) if h % 2 else jnp.where(even, v, 0.0)
                pv[h] = _dot_nt(wts.pop(h), v.astype(BF16))

        stages = (scores, weights, weighted_values)
        for step in range(SB_HEADS + len(stages) - 1):
            for k, stage in enumerate(stages):
                if 0 <= step - k < SB_HEADS:
                    stage(step - k)
        low = None
        for p in range(SB_PAIRS):
            new = pv[2 * p] + pv[2 * p + 1]
            acc_scr[p] = new if n is None else acc_scr[p] + new
        for h in range(SB_HEADS):
            car_scr[h] = tot[h]
            low = tot[h] if low is None else jnp.minimum(low, tot[h])
        return jnp.min(low)

    lowest = sweep(None)
    for n in range(n_chunks - 1, -1, -1):
        lowest = lax.cond(lowest <= SB_LOG2_UNDERFLOW, functools.partial(sweep, n), lambda low=lowest: low)
    for p in range(SB_PAIRS):
        o_ref[0, :, V7X_LANES * p:V7X_LANES * (p + 1)] = acc_scr[p]


def _sb_cached(q, k, v, cache_kt, cache_vt, *, kn):
    b, tq, w = q.shape
    past = cache_kt.shape[2]
    assert w == SB_WIDTH and past % kn == 0 and cache_kt.shape == (b, w, past)
    new_spec = pl.BlockSpec((1, tq, w), lambda bi: (bi, 0, 0))
    cache_spec = pl.BlockSpec((1, w, past), lambda bi: (bi, 0, 0))
    return pl.pallas_call(
        functools.partial(_sb_cached_kernel, tq=tq, kn=kn, n_chunks=past // kn),
        grid=(b,),
        in_specs=[new_spec, new_spec, new_spec, cache_spec, cache_spec,
                  pl.BlockSpec((2 * tq, tq), lambda bi: (0, 0)),
                  pl.BlockSpec((2 * kn, kn), lambda bi: (0, 0))],
        out_specs=new_spec,
        out_shape=jax.ShapeDtypeStruct((b, tq, w), F32),
        scratch_shapes=[pltpu.VMEM((SB_PAIRS, tq, V7X_LANES), F32), pltpu.VMEM((SB_HEADS, tq, 1), F32)],
        compiler_params=pltpu.CompilerParams(
            dimension_semantics=("arbitrary",), vmem_limit_bytes=V7X_VMEM_LIMIT),
        name="sb_cached",
    )(q, k, v, cache_kt, cache_vt, _suffix_ones(tq, 2), _suffix_ones(kn, 2))


def _hgrn_consts(c):
    t = np.arange(c)
    sub = t // HG_BLOCK
    same = sub[:, None] == sub[None, :]
    tri = (same & (t[None, :] <= t[:, None])).astype(np.float32)
    ones = same.astype(np.float32)
    x = sub[:, None] ^ sub[None, :]
    level = np.where(x == 0, 0, np.floor(np.log2(np.maximum(x, 1))).astype(np.int64) + 1)
    level = np.where(t[None, :] <= t[:, None], level, -1).astype(np.int32)
    pre = np.block([[tri, tri], [ones, ones]])
    return jnp.asarray(pre, dtype=BF16), jnp.asarray(level)


def _prod(vecs):
    out = None
    for v in vecs:
        out = v if out is None else out * v
    return out


def _rows(vecs, width):
    one = jnp.ones((HG_BLOCK, width), F32)
    return jnp.concatenate(
        [one if v is None else jnp.broadcast_to(v, (HG_BLOCK, width)) for v in vecs], axis=0)


def _hgrn_group(read, write, lb, g, pre, lvl, states, *, c, group, v_t):
    width = HG_WIDTH
    nsub = c // HG_BLOCK
    levels = int(round(math.log2(nsub)))
    states = list(states)
    level_masks = [lvl == l for l in range(levels + 1)]
    ch = [dict() for _ in range(group)]
    item = {}

    def head(h):
        return slice(HG_KEY_DIM * h, HG_KEY_DIM * (h + 1))

    def gates(gi):
        hq, hf, v = read(gi)
        f = lb + (1.0 - lb) * jax.nn.sigmoid(hf)
        ch[gi].update(hq=hq, kb=1.0 - f,
                      v_bf=v.astype(BF16), split=jnp.concatenate(_split_bf16(jnp.log(f)), axis=0))

    def prefix(gi):
        ch[gi]["sums"] = _dot(pre, ch[gi].pop("split"))

    def decays(gi):
        sums, hq, kb = (ch[gi].pop(k) for k in ("sums", "hq", "kb"))
        b16 = sums[:c]
        bend = sums[c:]
        q_dec = hq * jnp.exp(b16)
        dmat = jnp.exp(bend)
        k_inv = kb * jnp.exp(-b16)
        k_end = k_inv * dmat
        d = [dmat[HG_BLOCK * i:HG_BLOCK * i + 1, :] for i in range(nsub)]
        q_lv = [q_dec.astype(BF16)]
        k_lv = [k_inv.astype(BF16)]
        if levels >= 1:
            q_lv.append(q_lv[0])
            k_lv.append(k_end.astype(BF16))
        for l in range(2, levels + 1):
            half = 2 ** (l - 1)
            alpha, beta = [], []
            for i in range(nsub):
                mid = (i // (2 * half)) * 2 * half + half
                alpha.append(_prod(d[mid:i]) if i > mid else None)
                beta.append(_prod(d[i + 1:mid]) if i + 1 < mid else None)
            q_lv.append((q_dec * _rows(alpha, width)).astype(BF16))
            k_lv.append((k_end * _rows(beta, width)).astype(BF16))
        if nsub > 1:
            alpha_c = [_prod(d[:i]) if i > 0 else None for i in range(nsub)]
            beta_c = [_prod(d[i + 1:]) if i + 1 < nsub else None for i in range(nsub)]
            q_c = (q_dec * _rows(alpha_c, width)).astype(BF16)
            k_c = (k_end * _rows(beta_c, width)).astype(BF16)
        else:
            q_c = q_lv[0]
            k_c = k_end.astype(BF16)
        ch[gi].update(q_lv=q_lv, k_lv=k_lv, q_c=q_c, k_c=k_c, d_all=_prod(d))

    def level_scores(gi, h):
        q_lv, k_lv = ch[gi]["q_lv"], ch[gi]["k_lv"]
        item[gi, h] = [_dot_nt(q_lv[l][:, head(h)], k_lv[l][:, head(h)]) for l in range(levels + 1)]

    def combine(gi, h):
        att = jnp.zeros((c, c), F32)
        for l, scores in enumerate(item[gi, h]):
            att = jnp.where(level_masks[l], scores, att)
        item[gi, h] = att.astype(BF16)

    def outputs(gi, h):
        att = item[gi, h]
        v_bf, q_c, k_c = ch[gi]["v_bf"], ch[gi]["q_c"], ch[gi]["k_c"]
        if v_t:
            vt_h = v_bf[head(h), :]
            o_intra = _dot_nt(att, vt_h)
            st_add = _dot(vt_h, k_c[:, head(h)])
        else:
            o_intra = _dot(att, v_bf[:, head(h)])
            st_add = _dot_tn(v_bf[:, head(h)], k_c[:, head(h)])
        item[gi, h] = (o_intra + _dot_nt(q_c[:, head(h)], states[h].astype(BF16)), st_add)

    def finish(gi, h):
        o, st_add = item.pop((gi, h))
        ms = jnp.mean(o * o, axis=-1, keepdims=True)
        write(gi, h, o * lax.rsqrt(ms + NORM_EPS) * g[:, head(h)])
        states[h] = states[h] * ch[gi]["d_all"][:, head(h)] + st_add

    chunk_stages = (gates, prefix, decays)
    item_stages = (level_scores, combine, outputs, finish)
    n_items = group * HG_HEADS
    for step in range(-len(chunk_stages), n_items + len(item_stages) - 1):
        for gi in range(group):
            for k, stage in enumerate(chunk_stages):
                if step == HG_HEADS * gi - len(chunk_stages) + k:
                    stage(gi)
        for k, stage in enumerate(item_stages):
            if 0 <= step - k < n_items:
                stage(*divmod(step - k, HG_HEADS))
    return states


def _hgrn_kernel(*refs, c, group, layer, has_s0, v_t, n_steps):
    if has_s0:
        (hq_ref, hf_ref, hi_ref, lbraw_ref, g_ref, pre_ref, lvl_ref, s0_ref,
         o_ref, sout_ref, st_scr) = refs
    else:
        (hq_ref, hf_ref, hi_ref, lbraw_ref, g_ref, pre_ref, lvl_ref,
         o_ref, sout_ref, st_scr) = refs
        s0_ref = None
    ci = pl.program_id(1)

    @pl.when(ci == 0)
    def _():
        for h in range(HG_HEADS):
            if has_s0:
                st_scr[h] = s0_ref[0, h].T
            else:
                st_scr[h] = jnp.zeros((HG_VAL_DIM, HG_KEY_DIM), F32)

    raw = lbraw_ref[...]
    e = jnp.exp(raw - jnp.max(raw, axis=0, keepdims=True))
    lb = jnp.sum(e[:layer + 1], axis=0, keepdims=True) / jnp.sum(e, axis=0, keepdims=True)

    pre = pre_ref[...]
    lvl = lvl_ref[...]
    g = g_ref[...]
    def read(gi):
        rs = slice(gi * c, (gi + 1) * c)
        return hq_ref[0, rs, :], hf_ref[0, rs, :], hi_ref[0, :, rs] if v_t else hi_ref[0, rs, :]

    def write(gi, h, o):
        o_ref[0, gi * c:(gi + 1) * c, HG_VAL_DIM * h:HG_VAL_DIM * (h + 1)] = o

    states = _hgrn_group(read, write, lb, g, pre, lvl, [st_scr[h] for h in range(HG_HEADS)],
                         c=c, group=group, v_t=v_t)
    for h in range(HG_HEADS):
        st_scr[h] = states[h]

    @pl.when(ci == n_steps - 1)
    def _():
        for h in range(HG_HEADS):
            sout_ref[0, h] = st_scr[h].T


def _hgrn(hq, hf, hi, lb_raw, norm_g, s0, *, layer, c, group, v_t):
    b, t, w = hq.shape
    span = c * group
    assert w == HG_WIDTH and t % span == 0 and c % HG_BLOCK == 0
    n_steps = t // span
    pre, level = _hgrn_consts(c)
    tok = pl.BlockSpec((1, span, w), lambda bi, ci: (bi, ci, 0))
    tok_t = pl.BlockSpec((1, w, span), lambda bi, ci: (bi, 0, ci))
    const2 = lambda shape: pl.BlockSpec(shape, lambda bi, ci: (0, 0))
    st_spec = pl.BlockSpec((1, HG_HEADS, HG_KEY_DIM, HG_VAL_DIM), lambda bi, ci: (bi, 0, 0, 0))
    in_specs = [tok, tok, tok_t if v_t else tok, const2(lb_raw.shape), const2((1, w)),
                const2(pre.shape), const2(level.shape)]
    args = [hq, hf, hi, lb_raw, norm_g.reshape(1, w), pre, level]
    if s0 is not None:
        in_specs.append(st_spec)
        args.append(s0)
    return pl.pallas_call(
        functools.partial(_hgrn_kernel, c=c, group=group, layer=layer, has_s0=s0 is not None, v_t=v_t,
                          n_steps=n_steps),
        grid=(b, n_steps),
        in_specs=in_specs,
        out_specs=(tok, st_spec),
        out_shape=(jax.ShapeDtypeStruct((b, t, w), F32),
                   jax.ShapeDtypeStruct((b, HG_HEADS, HG_KEY_DIM, HG_VAL_DIM), F32)),
        scratch_shapes=[pltpu.VMEM((HG_HEADS, HG_VAL_DIM, HG_KEY_DIM), F32)],
        compiler_params=pltpu.CompilerParams(
            dimension_semantics=("arbitrary", "arbitrary"), vmem_limit_bytes=V7X_VMEM_LIMIT),
        name="hgrn2",
    )(*args)


def _out_kernel(x_ref, osb_ref, ohg_ref, gpre_ref, wg_ref, bg_ref, wa_ref, wb_ref, wo_ref, gp_ref, y_ref,
                *, tm, tiles):
    d_model = x_ref.shape[-1]
    off = np.cumsum([0, SB_WIDTH, HG_WIDTH, d_model, d_model])
    bg = bg_ref[...]
    for t in range(tiles):
        rows = slice(t * tm, (t + 1) * tm)
        x = x_ref[rows, :]
        xn = _rms_norm_bf16(x, gpre_ref[...])
        z_a, hz, ga, gb = (_dot(xn, wg_ref[:, off[i]:off[i + 1]]) for i in range(4))
        u_a = _dot((osb_ref[rows, :] * jax.nn.silu(z_a)).astype(BF16), wa_ref[...])
        u_b = _dot((ohg_ref[rows, :] * jax.nn.silu(hz)).astype(BF16), wb_ref[...])
        merged = (jax.nn.sigmoid(ga + bg[:, :d_model]) * u_a + jax.nn.sigmoid(gb + bg[:, d_model:]) * u_b)
        m = _dot(merged.astype(BF16), wo_ref[...])
        ms = jnp.mean(m * m, axis=-1, keepdims=True)
        y_ref[rows, :] = x + m * lax.rsqrt(ms + NORM_EPS) * gp_ref[...]


def _out_stage(x2d, o_sb, o_hg, g_pre, w_in_bf16, b_gate, wa, wb, wo, g_post, *, tm, tiles):
    n, d = x2d.shape
    span = tm * tiles
    assert n % span == 0
    _, offsets = _in_widths(d)
    w_gate = jnp.concatenate([w_in_bf16[:, offsets[i]:offsets[i + 1]] for i in IN_GATE_GROUPS], axis=1)
    row = lambda w: pl.BlockSpec((span, w), lambda i: (i, 0))
    const = lambda shape: pl.BlockSpec(shape, lambda i: (0, 0))
    return pl.pallas_call(
        functools.partial(_out_kernel, tm=tm, tiles=tiles),
        grid=(n // span,),
        in_specs=[row(d), row(SB_WIDTH), row(HG_WIDTH), const((1, d)), const(w_gate.shape),
                  const((1, 2 * d)), const(wa.shape), const(wb.shape), const(wo.shape), const((1, d))],
        out_specs=row(d),
        out_shape=jax.ShapeDtypeStruct((n, d), F32),
        compiler_params=pltpu.CompilerParams(
            dimension_semantics=("arbitrary",), vmem_limit_bytes=V7X_VMEM_LIMIT),
        name="out_stage",
    )(x2d, o_sb, o_hg, g_pre.reshape(1, d), w_gate, b_gate.reshape(1, 2 * d), wa, wb, wo, g_post.reshape(1, d))


def _largest_tile(n, cap):
    t = cap
    while n % t:
        t //= 2
    return t


def _trunk_layer(x, cache_k, cache_v, s0, layer, g_pre, w_in, b_gate, lb_raw, hg_norm_g,
                 w_a, w_b, w_out, g_post):
    b, t, d = x.shape
    n = b * t
    heads = (SB_HEADS, SB_HEAD_DIM)
    if cache_k is None:
        q, kt, vt, hq, hf, hi = _in_proj(x, g_pre, w_in, tm=_largest_tile(t, 1024), t_groups=(IN_K, IN_V, IN_HG_V))
        tq = _largest_tile(t, 256)
        o_sb = _sb_prompt(q, kt, vt, tq=tq, blocks=_largest_tile(t // tq, 8))
        k_out, v_out = (a.reshape(b, *heads, t).transpose(0, 3, 1, 2) for a in (kt, vt))
    else:
        tm = _largest_tile(n, 256)
        outs = _in_proj(x.reshape(1, n, d), g_pre, w_in, tm=tm, t_groups=())
        q, k, v, hq, hf, hi = (a.reshape(b, t, a.shape[-1]) for a in outs)
        past = cache_k.shape[1]
        cache_kt, cache_vt = (a.transpose(0, 2, 3, 1).reshape(b, SB_WIDTH, past) for a in (cache_k, cache_v))
        o_sb = _sb_cached(q, k, v, cache_kt, cache_vt, kn=_largest_tile(past, 256))
        k_out, v_out = k.reshape(b, t, *heads), v.reshape(b, t, *heads)
    c = _largest_tile(t, 128)
    o_hg, s_new = _hgrn(hq, hf, hi, lb_raw, hg_norm_g.reshape(-1), s0, layer=layer, c=c,
                        group=_largest_tile(t // c, 16), v_t=cache_k is None)
    flat = lambda a: a.reshape(n, a.shape[-1])
    tm = _largest_tile(n, 256)
    y = _out_stage(flat(x), flat(o_sb), flat(o_hg), g_pre, w_in, b_gate, w_a, w_b, w_out, g_post,
                   tm=tm, tiles=_largest_tile(n // tm, 4))
    return y.reshape(b, t, d), k_out, v_out, s_new


def kernel(x_prompt, x_sample, cache_sb_k, cache_sb_v, state_hgrn, g_pre, w_in, b_gate, hg_lb_raw,
           hg_norm_g, w_branch_a, w_branch_b, w_out, g_post):
    depth = w_in.shape[0]
    y_p, y_s = x_prompt, x_sample
    outs = [[] for _ in range(6)]
    for l in range(depth):
        weights = (g_pre[l], w_in[l].astype(BF16), b_gate[l], hg_lb_raw, hg_norm_g[l],
                   w_branch_a[l].astype(BF16), w_branch_b[l].astype(BF16), w_out[l].astype(BF16), g_post[l])
        y_p, kp, vp, sp = _trunk_layer(y_p, None, None, None, l, *weights)
        y_s, ks, vs, ss = _trunk_layer(y_s, cache_sb_k[l], cache_sb_v[l], state_hgrn[l], l, *weights)
        for lst, a in zip(outs, (kp, vp, sp, ks, vs, ss)):
            lst.append(a)
    return (y_p, y_s) + tuple(jnp.stack(lst) for lst in outs)
```

```python
import functools
import math

import numpy as np
import jax
import jax.numpy as jnp
from jax import lax
from jax.experimental import pallas as pl
from jax.experimental.pallas import tpu as pltpu

SB_HEADS = 8
SB_HEAD_DIM = 64
SB_WIDTH = SB_HEADS * SB_HEAD_DIM
HG_HEADS = 4
HG_KEY_DIM = 128
HG_VAL_DIM = 128
HG_WIDTH = HG_HEADS * HG_KEY_DIM
HG_BLOCK = 16
NORM_EPS = 1e-6

V7X_LANES = 128
V7X_VMEM_LIMIT = 56 * 1024 * 1024
SB_PAIRS = SB_WIDTH // V7X_LANES

SB_LOG2_UNDERFLOW = 150.5
LOG2_E = math.log2(math.e)

F32 = jnp.float32
BF16 = jnp.bfloat16

_NT = (((1,), (1,)), ((), ()))
_TN = (((0,), (0,)), ((), ()))


def _dot(a, b):
    return jnp.dot(a, b, preferred_element_type=F32)


def _dot_nt(a, b):
    return lax.dot_general(a, b, _NT, preferred_element_type=F32)


def _dot_tn(a, b):
    return lax.dot_general(a, b, _TN, preferred_element_type=F32)


def _split_bf16(x):
    hi = x.astype(BF16)
    lo = (x - hi.astype(F32)).astype(BF16)
    return hi, lo


IN_Q, IN_K, IN_V, IN_ZA, IN_HG_Q, IN_HG_F, IN_HG_V, IN_HG_Z, IN_GATE_A, IN_GATE_B = range(10)
IN_MIXER_GROUPS = (IN_Q, IN_K, IN_V, IN_HG_Q, IN_HG_F, IN_HG_V)
IN_GATE_GROUPS = (IN_ZA, IN_HG_Z, IN_GATE_A, IN_GATE_B)


def _in_widths(d_model):
    widths = (SB_WIDTH,) * 4 + (HG_WIDTH,) * 4 + (d_model, d_model)
    return widths, np.concatenate([[0], np.cumsum(widths)])


def _rms_norm_bf16(x, g):
    ms = jnp.mean(x * x, axis=-1, keepdims=True)
    return (x * lax.rsqrt(ms + NORM_EPS) * g).astype(BF16)


def _in_proj_kernel(*refs, groups, t_groups, widths, columns, q_scale, tm, tiles):
    if t_groups:
        x_ref, g_ref, w_ref, wt_ref = refs[:4]
        out_refs = refs[4:]
    else:
        x_ref, g_ref, w_ref = refs[:3]
        out_refs = refs[3:]
    for t in range(tiles):
        rows = slice(t * tm, (t + 1) * tm)
        xn = _rms_norm_bf16(x_ref[0, rows, :], g_ref[...])
        if t_groups:
            ht = _dot_nt(wt_ref[...], xn)
            off = 0
            for idx in t_groups:
                out_refs[groups.index(idx)][0, :, rows] = ht[off:off + widths[idx]]
                off += widths[idx]
        for ref, idx in zip(out_refs, groups):
            if idx not in t_groups:
                h = _dot(xn, w_ref[:, columns[idx][0]:columns[idx][1]])
                if idx == IN_Q:
                    h = h * q_scale
                ref[0, rows, :] = h.astype(ref.dtype)


def _in_proj(x, g_pre, w_in_bf16, *, tm, tiles, t_groups):
    b, t, d = x.shape
    widths, offsets = _in_widths(d)
    offsets = tuple(int(o) for o in offsets)
    span = tm * tiles
    assert offsets[-1] == w_in_bf16.shape[1] and t % span == 0
    out_shape, out_specs = [], []
    for idx in IN_MIXER_GROUPS:
        w = widths[idx]
        dt = BF16 if idx == IN_Q else F32
        if idx in t_groups:
            out_shape.append(jax.ShapeDtypeStruct((b, w, t), dt))
            out_specs.append(pl.BlockSpec((1, w, span), lambda bi, ti: (bi, 0, ti)))
        else:
            out_shape.append(jax.ShapeDtypeStruct((b, t, w), dt))
            out_specs.append(pl.BlockSpec((1, span, w), lambda bi, ti: (bi, ti, 0)))
    columns, parts, start = {}, [], 0
    for idx in IN_MIXER_GROUPS:
        if idx not in t_groups:
            parts.append(w_in_bf16[:, offsets[idx]:offsets[idx + 1]])
            columns[idx] = (start, start + widths[idx])
            start += widths[idx]
    w_used = jnp.concatenate(parts, axis=1)
    in_specs = [
        pl.BlockSpec((1, span, d), lambda bi, ti: (bi, ti, 0)),
        pl.BlockSpec((1, d), lambda bi, ti: (0, 0)),
        pl.BlockSpec(w_used.shape, lambda bi, ti: (0, 0)),
    ]
    args = [x, g_pre.reshape(1, d), w_used]
    if t_groups:
        w_t = jnp.concatenate([w_in_bf16[:, offsets[i]:offsets[i + 1]] for i in t_groups], axis=1).T
        in_specs.append(pl.BlockSpec(w_t.shape, lambda bi, ti: (0, 0)))
        args.append(w_t)
    return pl.pallas_call(
        functools.partial(_in_proj_kernel, groups=IN_MIXER_GROUPS, t_groups=t_groups, widths=widths,
                          columns=columns, q_scale=SB_HEAD_DIM ** -0.5 * LOG2_E, tm=tm, tiles=tiles),
        grid=(b, t // span),
        in_specs=in_specs,
        out_specs=tuple(out_specs),
        out_shape=tuple(out_shape),
        compiler_params=pltpu.CompilerParams(
            dimension_semantics=("arbitrary", "arbitrary"), vmem_limit_bytes=V7X_VMEM_LIMIT),
        name="in_proj",
    )(*args)


def _suffix_ones(n, copies):
    r = np.arange(n)
    u = (r[:, None] >= r[None, :]).astype(np.float32)
    return jnp.asarray(np.concatenate([u] * copies, axis=0), dtype=BF16)


def _sb_weights(z, u2, carry, mask):
    sp = jnp.maximum(z, 0.0) + jnp.log(1.0 + jnp.exp2(-jnp.abs(z))) * LOG2_E
    if mask is not None:
        sp = jnp.where(mask, sp, 0.0)
    hi, lo = _split_bf16(sp)
    tot = _dot(jnp.concatenate([hi, lo], axis=1), u2) + carry
    a = jnp.exp2(z - tot)
    if mask is not None:
        a = jnp.where(mask, a, 0.0)
    return a.astype(BF16), tot[:, 0:1]


def _head_split(x2, first):
    zero = jnp.zeros_like(x2)
    return jnp.where(first, x2, zero), jnp.where(first, zero, x2)


def _sb_prompt_kernel(q_ref, kt_ref, vt_ref, u_ref, o_ref, kt_scr, ve_scr, vo_scr, acc_scr, car_scr,
                      *, tq, blocks):
    step = pl.program_id(1)
    n_chunks = kt_scr.shape[0]

    @pl.when(step == 0)
    def _():
        row = lax.broadcasted_iota(jnp.int32, (SB_WIDTH, tq), 0)
        even = (row // SB_HEAD_DIM) % 2 == 0
        for n in range(n_chunks):
            kt_scr[n] = kt_ref[0, :, n * tq:(n + 1) * tq].astype(BF16)
            v = vt_ref[0, :, n * tq:(n + 1) * tq]
            ve_scr[n] = jnp.where(even, v, 0.0).astype(BF16)
            vo_scr[n] = jnp.where(even, 0.0, v).astype(BF16)

    late = tq - tq // 4

    def first(rows):
        return lax.broadcasted_iota(jnp.int32, (rows, V7X_LANES), 1) < SB_HEAD_DIM

    def sweep(jobs, row0):
        qs, z, sp16, tot, wts, pv = {}, {}, {}, {}, {}, {}
        masks = []
        for j, (n, (r0, r1), (k0, k1), offset, fresh) in enumerate(jobs):
            if offset is None:
                masks.append(None)
            else:
                r = lax.broadcasted_iota(jnp.int32, (r1 - r0, k1 - k0), 0)
                c = lax.broadcasted_iota(jnp.int32, (r1 - r0, k1 - k0), 1)
                masks.append(c < r + offset)
            for p in range(SB_PAIRS):
                q2 = q_ref[0, row0 + r0:row0 + r1, V7X_LANES * p:V7X_LANES * (p + 1)]
                qs[j, 2 * p], qs[j, 2 * p + 1] = _head_split(q2, first(r1 - r0))

        def pair_rows(h):
            return slice(V7X_LANES * (h // 2), V7X_LANES * (h // 2 + 1))

        def scores(j, h):
            n, _, (k0, k1), _, _ = jobs[j]
            z[j, h] = _dot(qs.pop((j, h)), kt_scr[n, pair_rows(h), k0:k1])

        def softplus(j, h):
            zz = z[j, h]
            sp = jnp.maximum(zz, 0.0) + jnp.log(1.0 + jnp.exp2(-jnp.abs(zz))) * LOG2_E
            if masks[j] is not None:
                sp = jnp.where(masks[j], sp, 0.0)
            sp16[j, h] = sp.astype(BF16)

        def suffix_sums(j, h):
            _, (r0, r1), (k0, k1), _, fresh = jobs[j]
            carry = jnp.zeros((r1 - r0, 1), F32) if fresh else car_scr[h, r0:r1]
            tot[j, h] = _dot(sp16.pop((j, h)), u_ref[:k1 - k0, :k1 - k0]) + carry

        def weights(j, h):
            a = jnp.exp2(z.pop((j, h)) - tot[j, h])
            if masks[j] is not None:
                a = jnp.where(masks[j], a, 0.0)
            wts[j, h] = a.astype(BF16)
            tot[j, h] = tot[j, h][:, 0:1]

        def weighted_values(j, h):
            n, _, (k0, k1), _, _ = jobs[j]
            v_scr = vo_scr if h % 2 else ve_scr
            pv[j, h] = _dot_nt(wts.pop((j, h)), v_scr[n, pair_rows(h), k0:k1])

        stages = (scores, softplus, suffix_sums, weights, weighted_values)
        n_items = len(jobs) * SB_HEADS
        for step in range(n_items + len(stages) - 1):
            for k, stage in enumerate(stages):
                if 0 <= step - k < n_items:
                    stage(*divmod(step - k, SB_HEADS))

        for j, (n, (r0, r1), _, _, fresh) in enumerate(jobs):
            for p in range(SB_PAIRS):
                new = pv[j, 2 * p] + pv[j, 2 * p + 1]
                acc_scr[p, r0:r1] = new if fresh else acc_scr[p, r0:r1] + new
            for h in range(SB_HEADS):
                car_scr[h, r0:r1] = tot[j, h]

    def lowest_total(r0, r1):
        low = car_scr[0, r0:r1]
        for h in range(1, SB_HEADS):
            low = jnp.minimum(low, car_scr[h, r0:r1])
        return jnp.min(low)

    def cond(st):
        n, low_early, low_late = st
        return jnp.logical_and(n >= 0, jnp.minimum(low_early, low_late) <= SB_LOG2_UNDERFLOW)

    for blk in range(blocks):
        i = step * blocks + blk
        row0 = blk * tq
        sweep([(i, (0, tq), (0, tq), 0, True)], row0)

        def older(n, r0, r1, low, row0=row0):
            def run():
                sweep([(n, (r0, r1), (0, tq), None, False)], row0)
                return lowest_total(r0, r1)
            return lax.cond(low <= SB_LOG2_UNDERFLOW, run, lambda: low)

        def body(st, older=older):
            n, low_early, low_late = st
            return n - 1, older(n, 0, late, low_early), older(n, late, tq, low_late)

        lax.while_loop(cond, body, (i - 1, lowest_total(0, late), lowest_total(late, tq)))
        for p in range(SB_PAIRS):
            o_ref[0, row0:row0 + tq, V7X_LANES * p:V7X_LANES * (p + 1)] = acc_scr[p]


def _sb_prompt(q, kt, vt, *, tq, blocks):
    b, t, w = q.shape
    assert w == SB_WIDTH and t % (tq * blocks) == 0 and kt.shape == (b, w, t)
    n_chunks = t // tq
    kv_spec = pl.BlockSpec((1, w, t), lambda bi, i: (bi, 0, 0))
    q_spec = pl.BlockSpec((1, tq * blocks, w), lambda bi, i: (bi, i, 0))
    return pl.pallas_call(
        functools.partial(_sb_prompt_kernel, tq=tq, blocks=blocks),
        grid=(b, n_chunks // blocks),
        in_specs=[q_spec, kv_spec, kv_spec, pl.BlockSpec((tq, tq), lambda bi, i: (0, 0))],
        out_specs=q_spec,
        out_shape=jax.ShapeDtypeStruct((b, t, w), F32),
        scratch_shapes=[pltpu.VMEM((n_chunks, w, tq), BF16)] * 3
        + [pltpu.VMEM((SB_PAIRS, tq, V7X_LANES), F32), pltpu.VMEM((SB_HEADS, tq, 1), F32)],
        compiler_params=pltpu.CompilerParams(
            dimension_semantics=("arbitrary", "arbitrary"), vmem_limit_bytes=V7X_VMEM_LIMIT),
        name="sb_prompt",
    )(q, kt, vt, _suffix_ones(tq, 1))


def _sb_cached_kernel(q_ref, k_ref, v_ref, ckt_ref, cvt_ref, ud_ref, uc_ref, o_ref, acc_scr, car_scr,
                      *, tq, kn, n_chunks):
    first = lax.broadcasted_iota(jnp.int32, (tq, V7X_LANES), 1) < SB_HEAD_DIM
    r = lax.broadcasted_iota(jnp.int32, (tq, tq), 0)
    c = lax.broadcasted_iota(jnp.int32, (tq, tq), 1)
    causal = c < r
    even = (lax.broadcasted_iota(jnp.int32, (V7X_LANES, kn), 0) // SB_HEAD_DIM) % 2 == 0
    qs = []
    for p in range(SB_PAIRS):
        qs.extend(_head_split(q_ref[0, :, V7X_LANES * p:V7X_LANES * (p + 1)], first))

    def pair_rows(h):
        return slice(V7X_LANES * (h // 2), V7X_LANES * (h // 2 + 1))

    def sweep(n):
        z, wts, tot, pv = {}, {}, {}, {}

        def scores(h):
            if n is None:
                z[h] = _dot_nt(qs[h], k_ref[0, :, pair_rows(h)].astype(BF16))
            else:
                z[h] = _dot(qs[h], ckt_ref[0, pair_rows(h), n * kn:(n + 1) * kn].astype(BF16))

        def weights(h):
            if n is None:
                wts[h], tot[h] = _sb_weights(z.pop(h), ud_ref[...], jnp.zeros((tq, 1), F32), causal)
            else:
                wts[h], tot[h] = _sb_weights(z.pop(h), uc_ref[...], car_scr[h], None)

        def weighted_values(h):
            if n is None:
                v = _head_split(v_ref[0, :, pair_rows(h)], first)[h % 2]
                pv[h] = _dot(wts.pop(h), v.astype(BF16))
            else:
                v = cvt_ref[0, pair_rows(h), n * kn:(n + 1) * kn]
                v = jnp.where(even, 0.0, v) if h % 2 else jnp.where(even, v, 0.0)
                pv[h] = _dot_nt(wts.pop(h), v.astype(BF16))

        stages = (scores, weights, weighted_values)
        for step in range(SB_HEADS + len(stages) - 1):
            for k, stage in enumerate(stages):
                if 0 <= step - k < SB_HEADS:
                    stage(step - k)
        low = None
        for p in range(SB_PAIRS):
            new = pv[2 * p] + pv[2 * p + 1]
            acc_scr[p] = new if n is None else acc_scr[p] + new
        for h in range(SB_HEADS):
            car_scr[h] = tot[h]
            low = tot[h] if low is None else jnp.minimum(low, tot[h])
        return jnp.min(low)

    lowest = sweep(None)
    for n in range(n_chunks - 1, -1, -1):
        lowest = lax.cond(lowest <= SB_LOG2_UNDERFLOW, functools.partial(sweep, n), lambda low=lowest: low)
    for p in range(SB_PAIRS):
        o_ref[0, :, V7X_LANES * p:V7X_LANES * (p + 1)] = acc_scr[p]


def _sb_cached(q, k, v, cache_kt, cache_vt, *, kn):
    b, tq, w = q.shape
    past = cache_kt.shape[2]
    assert w == SB_WIDTH and past % kn == 0 and cache_kt.shape == (b, w, past)
    new_spec = pl.BlockSpec((1, tq, w), lambda bi: (bi, 0, 0))
    cache_spec = pl.BlockSpec((1, w, past), lambda bi: (bi, 0, 0))
    return pl.pallas_call(
        functools.partial(_sb_cached_kernel, tq=tq, kn=kn, n_chunks=past // kn),
        grid=(b,),
        in_specs=[new_spec, new_spec, new_spec, cache_spec, cache_spec,
                  pl.BlockSpec((2 * tq, tq), lambda bi: (0, 0)),
                  pl.BlockSpec((2 * kn, kn), lambda bi: (0, 0))],
        out_specs=new_spec,
        out_shape=jax.ShapeDtypeStruct((b, tq, w), F32),
        scratch_shapes=[pltpu.VMEM((SB_PAIRS, tq, V7X_LANES), F32), pltpu.VMEM((SB_HEADS, tq, 1), F32)],
        compiler_params=pltpu.CompilerParams(
            dimension_semantics=("arbitrary",), vmem_limit_bytes=V7X_VMEM_LIMIT),
        name="sb_cached",
    )(q, k, v, cache_kt, cache_vt, _suffix_ones(tq, 2), _suffix_ones(kn, 2))


def _hgrn_consts(c):
    t = np.arange(c)
    sub = t // HG_BLOCK
    same = sub[:, None] == sub[None, :]
    tri = (same & (t[None, :] <= t[:, None])).astype(np.float32)
    ones = same.astype(np.float32)
    x = sub[:, None] ^ sub[None, :]
    level = np.where(x == 0, 0, np.floor(np.log2(np.maximum(x, 1))).astype(np.int64) + 1)
    level = np.where(t[None, :] <= t[:, None], level, -1).astype(np.int32)
    pre = np.block([[tri, tri], [ones, ones]])
    return jnp.asarray(pre, dtype=BF16), jnp.asarray(level)


def _prod(vecs):
    out = None
    for v in vecs:
        out = v if out is None else out * v
    return out


def _scale_rows(x, vecs):
    blocks = [x[HG_BLOCK * i:HG_BLOCK * (i + 1)] for i in range(len(vecs))]
    return jnp.concatenate([b if v is None else b * v for b, v in zip(blocks, vecs)], axis=0)


def _hgrn_group(read, write, lb, g, pre, lvl, states, *, c, group, v_t):
    nsub = c // HG_BLOCK
    levels = int(round(math.log2(nsub)))
    states = list(states)
    level_masks = [lvl == l for l in range(levels + 1)]
    ch = [dict() for _ in range(group)]
    item = {}

    def head(h):
        return slice(HG_KEY_DIM * h, HG_KEY_DIM * (h + 1))

    def gates(gi):
        hq, hf, v = read(gi)
        f = lb + (1.0 - lb) * jax.nn.sigmoid(hf)
        ch[gi].update(hq=hq, kb=1.0 - f,
                      v_bf=v.astype(BF16), split=jnp.concatenate(_split_bf16(jnp.log(f)), axis=0))

    def prefix(gi):
        ch[gi]["sums"] = _dot(pre, ch[gi].pop("split"))

    def decays(gi):
        sums, hq, kb = (ch[gi].pop(k) for k in ("sums", "hq", "kb"))
        b16 = sums[:c]
        bend = sums[c:]
        q_dec = hq * jnp.exp(b16)
        dmat = jnp.exp(bend)
        k_inv = kb * jnp.exp(-b16)
        k_end = k_inv * dmat
        d = [dmat[HG_BLOCK * i:HG_BLOCK * i + 1, :] for i in range(nsub)]
        q_lv = [q_dec.astype(BF16)]
        k_lv = [k_inv.astype(BF16)]
        if levels >= 1:
            q_lv.append(q_lv[0])
            k_lv.append(k_end.astype(BF16))
        for l in range(2, levels + 1):
            half = 2 ** (l - 1)
            alpha, beta = [], []
            for i in range(nsub):
                mid = (i // (2 * half)) * 2 * half + half
                alpha.append(_prod(d[mid:i]) if i > mid else None)
                beta.append(_prod(d[i + 1:mid]) if i + 1 < mid else None)
            q_lv.append(_scale_rows(q_dec, alpha).astype(BF16))
            k_lv.append(_scale_rows(k_end, beta).astype(BF16))
        if nsub > 1:
            alpha_c = [_prod(d[:i]) if i > 0 else None for i in range(nsub)]
            beta_c = [_prod(d[i + 1:]) if i + 1 < nsub else None for i in range(nsub)]
            q_c = _scale_rows(q_dec, alpha_c).astype(BF16)
            k_c = _scale_rows(k_end, beta_c).astype(BF16)
        else:
            q_c = q_lv[0]
            k_c = k_end.astype(BF16)
        ch[gi].update(q_lv=q_lv, k_lv=k_lv, q_c=q_c, k_c=k_c, d_all=_prod(d))

    def level_scores(gi, h):
        q_lv, k_lv = ch[gi]["q_lv"], ch[gi]["k_lv"]
        item[gi, h] = [_dot_nt(q_lv[l][:, head(h)], k_lv[l][:, head(h)]) for l in range(levels + 1)]

    def combine(gi, h):
        att = jnp.zeros((c, c), F32)
        for l, scores in enumerate(item[gi, h]):
            att = jnp.where(level_masks[l], scores, att)
        item[gi, h] = att.astype(BF16)

    def outputs(gi, h):
        att = item[gi, h]
        v_bf, q_c, k_c = ch[gi]["v_bf"], ch[gi]["q_c"], ch[gi]["k_c"]
        if v_t:
            vt_h = v_bf[head(h), :]
            o_intra = _dot_nt(att, vt_h)
            st_add = _dot(vt_h, k_c[:, head(h)])
        else:
            o_intra = _dot(att, v_bf[:, head(h)])
            st_add = _dot_tn(v_bf[:, head(h)], k_c[:, head(h)])
        item[gi, h] = (o_intra + _dot_nt(q_c[:, head(h)], states[h].astype(BF16)), st_add)

    def finish(gi, h):
        o, st_add = item.pop((gi, h))
        ms = jnp.mean(o * o, axis=-1, keepdims=True)
        write(gi, h, o * lax.rsqrt(ms + NORM_EPS) * g[:, head(h)])
        states[h] = states[h] * ch[gi]["d_all"][:, head(h)] + st_add

    chunk_stages = (gates, prefix, decays)
    item_stages = (level_scores, combine, outputs, finish)
    n_items = group * HG_HEADS
    for step in range(-len(chunk_stages), n_items + len(item_stages) - 1):
        for gi in range(group):
            for k, stage in enumerate(chunk_stages):
                if step == HG_HEADS * gi - len(chunk_stages) + k:
                    stage(gi)
        for k, stage in enumerate(item_stages):
            if 0 <= step - k < n_items:
                stage(*divmod(step - k, HG_HEADS))
    return states


def _hgrn_kernel(*refs, c, group, layer, has_s0, v_t, n_steps):
    if has_s0:
        (hq_ref, hf_ref, hi_ref, lbraw_ref, g_ref, pre_ref, lvl_ref, s0_ref,
         o_ref, sout_ref, st_scr) = refs
    else:
        (hq_ref, hf_ref, hi_ref, lbraw_ref, g_ref, pre_ref, lvl_ref,
         o_ref, sout_ref, st_scr) = refs
        s0_ref = None
    ci = pl.program_id(1)

    @pl.when(ci == 0)
    def _():
        for h in range(HG_HEADS):
            if has_s0:
                st_scr[h] = s0_ref[0, h].T
            else:
                st_scr[h] = jnp.zeros((HG_VAL_DIM, HG_KEY_DIM), F32)

    raw = lbraw_ref[...]
    e = jnp.exp(raw - jnp.max(raw, axis=0, keepdims=True))
    lb = jnp.sum(e[:layer + 1], axis=0, keepdims=True) / jnp.sum(e, axis=0, keepdims=True)

    pre = pre_ref[...]
    lvl = lvl_ref[...]
    g = g_ref[...]
    def read(gi):
        rs = slice(gi * c, (gi + 1) * c)
        return hq_ref[0, rs, :], hf_ref[0, rs, :], hi_ref[0, :, rs] if v_t else hi_ref[0, rs, :]

    def write(gi, h, o):
        o_ref[0, gi * c:(gi + 1) * c, HG_VAL_DIM * h:HG_VAL_DIM * (h + 1)] = o

    states = _hgrn_group(read, write, lb, g, pre, lvl, [st_scr[h] for h in range(HG_HEADS)],
                         c=c, group=group, v_t=v_t)
    for h in range(HG_HEADS):
        st_scr[h] = states[h]

    @pl.when(ci == n_steps - 1)
    def _():
        for h in range(HG_HEADS):
            sout_ref[0, h] = st_scr[h].T


def _hgrn(hq, hf, hi, lb_raw, norm_g, s0, *, layer, c, group, v_t):
    b, t, w = hq.shape
    span = c * group
    assert w == HG_WIDTH and t % span == 0 and c % HG_BLOCK == 0
    n_steps = t // span
    pre, level = _hgrn_consts(c)
    tok = pl.BlockSpec((1, span, w), lambda bi, ci: (bi, ci, 0))
    tok_t = pl.BlockSpec((1, w, span), lambda bi, ci: (bi, 0, ci))
    const2 = lambda shape: pl.BlockSpec(shape, lambda bi, ci: (0, 0))
    st_spec = pl.BlockSpec((1, HG_HEADS, HG_KEY_DIM, HG_VAL_DIM), lambda bi, ci: (bi, 0, 0, 0))
    in_specs = [tok, tok, tok_t if v_t else tok, const2(lb_raw.shape), const2((1, w)),
                const2(pre.shape), const2(level.shape)]
    args = [hq, hf, hi, lb_raw, norm_g.reshape(1, w), pre, level]
    if s0 is not None:
        in_specs.append(st_spec)
        args.append(s0)
    return pl.pallas_call(
        functools.partial(_hgrn_kernel, c=c, group=group, layer=layer, has_s0=s0 is not None, v_t=v_t,
                          n_steps=n_steps),
        grid=(b, n_steps),
        in_specs=in_specs,
        out_specs=(tok, st_spec),
        out_shape=(jax.ShapeDtypeStruct((b, t, w), F32),
                   jax.ShapeDtypeStruct((b, HG_HEADS, HG_KEY_DIM, HG_VAL_DIM), F32)),
        scratch_shapes=[pltpu.VMEM((HG_HEADS, HG_VAL_DIM, HG_KEY_DIM), F32)],
        compiler_params=pltpu.CompilerParams(
            dimension_semantics=("arbitrary", "arbitrary"), vmem_limit_bytes=V7X_VMEM_LIMIT),
        name="hgrn2",
    )(*args)


def _out_kernel(x_ref, osb_ref, ohg_ref, gpre_ref, wg_ref, bg_ref, wa_ref, wb_ref, wo_ref, gp_ref, y_ref,
                *, tm, tiles):
    d_model = x_ref.shape[-1]
    off = np.cumsum([0, SB_WIDTH, HG_WIDTH, d_model, d_model])
    bg = bg_ref[...]
    for t in range(tiles):
        rows = slice(t * tm, (t + 1) * tm)
        x = x_ref[rows, :]
        xn = _rms_norm_bf16(x, gpre_ref[...])
        z_a, hz, ga, gb = (_dot(xn, wg_ref[:, off[i]:off[i + 1]]) for i in range(4))
        u_a = _dot((osb_ref[rows, :] * jax.nn.silu(z_a)).astype(BF16), wa_ref[...])
        u_b = _dot((ohg_ref[rows, :] * jax.nn.silu(hz)).astype(BF16), wb_ref[...])
        merged = (jax.nn.sigmoid(ga + bg[:, :d_model]) * u_a + jax.nn.sigmoid(gb + bg[:, d_model:]) * u_b)
        m = _dot(merged.astype(BF16), wo_ref[...])
        ms = jnp.mean(m * m, axis=-1, keepdims=True)
        y_ref[rows, :] = x + m * lax.rsqrt(ms + NORM_EPS) * gp_ref[...]


def _out_stage(x2d, o_sb, o_hg, g_pre, w_in_bf16, b_gate, wa, wb, wo, g_post, *, tm, tiles):
    n, d = x2d.shape
    span = tm * tiles
    assert n % span == 0
    _, offsets = _in_widths(d)
    w_gate = jnp.concatenate([w_in_bf16[:, offsets[i]:offsets[i + 1]] for i in IN_GATE_GROUPS], axis=1)
    row = lambda w: pl.BlockSpec((span, w), lambda i: (i, 0))
    const = lambda shape: pl.BlockSpec(shape, lambda i: (0, 0))
    return pl.pallas_call(
        functools.partial(_out_kernel, tm=tm, tiles=tiles),
        grid=(n // span,),
        in_specs=[row(d), row(SB_WIDTH), row(HG_WIDTH), const((1, d)), const(w_gate.shape),
                  const((1, 2 * d)), const(wa.shape), const(wb.shape), const(wo.shape), const((1, d))],
        out_specs=row(d),
        out_shape=jax.ShapeDtypeStruct((n, d), F32),
        compiler_params=pltpu.CompilerParams(
            dimension_semantics=("arbitrary",), vmem_limit_bytes=V7X_VMEM_LIMIT),
        name="out_stage",
    )(x2d, o_sb, o_hg, g_pre.reshape(1, d), w_gate, b_gate.reshape(1, 2 * d), wa, wb, wo, g_post.reshape(1, d))


PROJ_TILE, IN_PROJ_TILES, OUT_TILES = 256, 4, 4
SB_BLOCK, SB_BLOCKS = 256, 8
HG_CHUNK, HG_CHUNKS = 128, 16


def _largest_tile(n, cap):
    t = cap
    while n % t:
        t //= 2
    return t


def _trunk_layer(x, cache_k, cache_v, s0, layer, g_pre, w_in, b_gate, lb_raw, hg_norm_g,
                 w_a, w_b, w_out, g_post):
    b, t, d = x.shape
    n = b * t
    heads = (SB_HEADS, SB_HEAD_DIM)
    if cache_k is None:
        tm = _largest_tile(t, PROJ_TILE)
        q, kt, vt, hq, hf, hi = _in_proj(x, g_pre, w_in, tm=tm, tiles=_largest_tile(t // tm, IN_PROJ_TILES),
                                         t_groups=(IN_K, IN_V, IN_HG_V))
        tq = _largest_tile(t, SB_BLOCK)
        o_sb = _sb_prompt(q, kt, vt, tq=tq, blocks=_largest_tile(t // tq, SB_BLOCKS))
        k_out, v_out = (a.reshape(b, *heads, t).transpose(0, 3, 1, 2) for a in (kt, vt))
    else:
        tm = _largest_tile(n, PROJ_TILE)
        outs = _in_proj(x.reshape(1, n, d), g_pre, w_in, tm=tm, tiles=_largest_tile(n // tm, IN_PROJ_TILES),
                        t_groups=())
        q, k, v, hq, hf, hi = (a.reshape(b, t, a.shape[-1]) for a in outs)
        past = cache_k.shape[1]
        cache_kt, cache_vt = (a.transpose(0, 2, 3, 1).reshape(b, SB_WIDTH, past) for a in (cache_k, cache_v))
        o_sb = _sb_cached(q, k, v, cache_kt, cache_vt, kn=_largest_tile(past, SB_BLOCK))
        k_out, v_out = k.reshape(b, t, *heads), v.reshape(b, t, *heads)
    c = _largest_tile(t, HG_CHUNK)
    o_hg, s_new = _hgrn(hq, hf, hi, lb_raw, hg_norm_g.reshape(-1), s0, layer=layer, c=c,
                        group=_largest_tile(t // c, HG_CHUNKS), v_t=cache_k is None)
    flat = lambda a: a.reshape(n, a.shape[-1])
    tm = _largest_tile(n, PROJ_TILE)
    y = _out_stage(flat(x), flat(o_sb), flat(o_hg), g_pre, w_in, b_gate, w_a, w_b, w_out, g_post,
                   tm=tm, tiles=_largest_tile(n // tm, OUT_TILES))
    return y.reshape(b, t, d), k_out, v_out, s_new


def kernel(x_prompt, x_sample, cache_sb_k, cache_sb_v, state_hgrn, g_pre, w_in, b_gate, hg_lb_raw,
           hg_norm_g, w_branch_a, w_branch_b, w_out, g_post):
    depth = w_in.shape[0]
    y_p, y_s = x_prompt, x_sample
    outs = [[] for _ in range(6)]
    for l in range(depth):
        weights = (g_pre[l], w_in[l].astype(BF16), b_gate[l], hg_lb_raw, hg_norm_g[l],
                   w_branch_a[l].astype(BF16), w_branch_b[l].astype(BF16), w_out[l].astype(BF16), g_post[l])
        y_p, kp, vp, sp = _trunk_layer(y_p, None, None, None, l, *weights)
        y_s, ks, vs, ss = _trunk_layer(y_s, cache_sb_k[l], cache_sb_v[l], state_hgrn[l], l, *weights)
        for lst, a in zip(outs, (kp, vp, sp, ks, vs, ss)):
            lst.append(a)
    return (y_p, y_s) + tuple(jnp.stack(lst) for lst in outs)
```

```python
import functools
import math

import numpy as np
import jax
import jax.numpy as jnp
from jax import lax
from jax.experimental import pallas as pl
from jax.experimental.pallas import tpu as pltpu

SB_HEADS = 8
SB_HEAD_DIM = 64
SB_WIDTH = SB_HEADS * SB_HEAD_DIM
HG_HEADS = 4
HG_KEY_DIM = 128
HG_VAL_DIM = 128
HG_WIDTH = HG_HEADS * HG_KEY_DIM
HG_BLOCK = 16
NORM_EPS = 1e-6

V7X_LANES = 128
V7X_VMEM_LIMIT = 56 * 1024 * 1024
SB_PAIRS = SB_WIDTH // V7X_LANES

SB_LOG2_UNDERFLOW = 150.5
SB_LATE_ROWS = 176
LOG2_E = math.log2(math.e)

F32 = jnp.float32
BF16 = jnp.bfloat16

_NT = (((1,), (1,)), ((), ()))
_TN = (((0,), (0,)), ((), ()))


def _dot(a, b):
    return jnp.dot(a, b, preferred_element_type=F32)


def _dot_nt(a, b):
    return lax.dot_general(a, b, _NT, preferred_element_type=F32)


def _dot_tn(a, b):
    return lax.dot_general(a, b, _TN, preferred_element_type=F32)


def _split_bf16(x):
    hi = x.astype(BF16)
    lo = (x - hi.astype(F32)).astype(BF16)
    return hi, lo


IN_Q, IN_K, IN_V, IN_ZA, IN_HG_Q, IN_HG_F, IN_HG_V, IN_HG_Z, IN_GATE_A, IN_GATE_B = range(10)
IN_MIXER_GROUPS = (IN_Q, IN_K, IN_V, IN_HG_Q, IN_HG_F, IN_HG_V)
IN_GATE_GROUPS = (IN_ZA, IN_HG_Z, IN_GATE_A, IN_GATE_B)


def _in_widths(d_model):
    widths = (SB_WIDTH,) * 4 + (HG_WIDTH,) * 4 + (d_model, d_model)
    return widths, np.concatenate([[0], np.cumsum(widths)])


def _rms_norm_bf16(x, g):
    ms = jnp.mean(x * x, axis=-1, keepdims=True)
    return (x * lax.rsqrt(ms + NORM_EPS) * g).astype(BF16)


def _in_proj_kernel(*refs, groups, t_groups, widths, columns, q_scale, tm, tiles):
    if t_groups:
        x_ref, g_ref, w_ref, wt_ref = refs[:4]
        out_refs = refs[4:]
    else:
        x_ref, g_ref, w_ref = refs[:3]
        out_refs = refs[3:]
    for t in range(tiles):
        rows = slice(t * tm, (t + 1) * tm)
        xn = _rms_norm_bf16(x_ref[0, rows, :], g_ref[...])
        if t_groups:
            ht = _dot_nt(wt_ref[...], xn)
            off = 0
            for idx in t_groups:
                out_refs[groups.index(idx)][0, :, rows] = ht[off:off + widths[idx]]
                off += widths[idx]
        for ref, idx in zip(out_refs, groups):
            if idx not in t_groups:
                h = _dot(xn, w_ref[:, columns[idx][0]:columns[idx][1]])
                if idx == IN_Q:
                    h = h * q_scale
                ref[0, rows, :] = h.astype(ref.dtype)


def _in_proj(x, g_pre, w_in_bf16, *, tm, tiles, t_groups):
    b, t, d = x.shape
    widths, offsets = _in_widths(d)
    offsets = tuple(int(o) for o in offsets)
    span = tm * tiles
    assert offsets[-1] == w_in_bf16.shape[1] and t % span == 0
    out_shape, out_specs = [], []
    for idx in IN_MIXER_GROUPS:
        w = widths[idx]
        dt = BF16 if idx == IN_Q else F32
        if idx in t_groups:
            out_shape.append(jax.ShapeDtypeStruct((b, w, t), dt))
            out_specs.append(pl.BlockSpec((1, w, span), lambda bi, ti: (bi, 0, ti)))
        else:
            out_shape.append(jax.ShapeDtypeStruct((b, t, w), dt))
            out_specs.append(pl.BlockSpec((1, span, w), lambda bi, ti: (bi, ti, 0)))
    columns, parts, start = {}, [], 0
    for idx in IN_MIXER_GROUPS:
        if idx not in t_groups:
            parts.append(w_in_bf16[:, offsets[idx]:offsets[idx + 1]])
            columns[idx] = (start, start + widths[idx])
            start += widths[idx]
    w_used = jnp.concatenate(parts, axis=1)
    in_specs = [
        pl.BlockSpec((1, span, d), lambda bi, ti: (bi, ti, 0)),
        pl.BlockSpec((1, d), lambda bi, ti: (0, 0)),
        pl.BlockSpec(w_used.shape, lambda bi, ti: (0, 0)),
    ]
    args = [x, g_pre.reshape(1, d), w_used]
    if t_groups:
        w_t = jnp.concatenate([w_in_bf16[:, offsets[i]:offsets[i + 1]] for i in t_groups], axis=1).T
        in_specs.append(pl.BlockSpec(w_t.shape, lambda bi, ti: (0, 0)))
        args.append(w_t)
    return pl.pallas_call(
        functools.partial(_in_proj_kernel, groups=IN_MIXER_GROUPS, t_groups=t_groups, widths=widths,
                          columns=columns, q_scale=SB_HEAD_DIM ** -0.5 * LOG2_E, tm=tm, tiles=tiles),
        grid=(b, t // span),
        in_specs=in_specs,
        out_specs=tuple(out_specs),
        out_shape=tuple(out_shape),
        compiler_params=pltpu.CompilerParams(
            dimension_semantics=("arbitrary", "arbitrary"), vmem_limit_bytes=V7X_VMEM_LIMIT),
        name="in_proj",
    )(*args)


def _suffix_ones(n, copies):
    r = np.arange(n)
    u = (r[:, None] >= r[None, :]).astype(np.float32)
    return jnp.asarray(np.concatenate([u] * copies, axis=0), dtype=BF16)


def _sb_weights(z, u2, carry, mask):
    sp = jnp.maximum(z, 0.0) + jnp.log(1.0 + jnp.exp2(-jnp.abs(z))) * LOG2_E
    if mask is not None:
        sp = jnp.where(mask, sp, 0.0)
    hi, lo = _split_bf16(sp)
    tot = _dot(jnp.concatenate([hi, lo], axis=1), u2) + carry
    a = jnp.exp2(z - tot)
    if mask is not None:
        a = jnp.where(mask, a, 0.0)
    return a.astype(BF16), tot[:, 0:1]


def _head_split(x2, first):
    zero = jnp.zeros_like(x2)
    return jnp.where(first, x2, zero), jnp.where(first, zero, x2)


def _sb_prompt_kernel(q_ref, kt_ref, vt_ref, u_ref, o_ref, kt_scr, ve_scr, vo_scr, acc_scr, car_scr,
                      *, tq, blocks):
    step = pl.program_id(1)
    n_chunks = kt_scr.shape[0]

    @pl.when(step == 0)
    def _():
        row = lax.broadcasted_iota(jnp.int32, (SB_WIDTH, tq), 0)
        even = (row // SB_HEAD_DIM) % 2 == 0
        for n in range(n_chunks):
            kt_scr[n] = kt_ref[0, :, n * tq:(n + 1) * tq].astype(BF16)
            v = vt_ref[0, :, n * tq:(n + 1) * tq]
            ve_scr[n] = jnp.where(even, v, 0.0).astype(BF16)
            vo_scr[n] = jnp.where(even, 0.0, v).astype(BF16)

    late = min(SB_LATE_ROWS, tq - 16)

    def first(rows):
        return lax.broadcasted_iota(jnp.int32, (rows, V7X_LANES), 1) < SB_HEAD_DIM

    def sweep(jobs, row0):
        qs, z, sp16, tot, wts, pv = {}, {}, {}, {}, {}, {}
        masks = []
        for j, (n, (r0, r1), (k0, k1), offset, fresh) in enumerate(jobs):
            if offset is None:
                masks.append(None)
            else:
                r = lax.broadcasted_iota(jnp.int32, (r1 - r0, k1 - k0), 0)
                c = lax.broadcasted_iota(jnp.int32, (r1 - r0, k1 - k0), 1)
                masks.append(c < r + offset)
            for p in range(SB_PAIRS):
                q2 = q_ref[0, row0 + r0:row0 + r1, V7X_LANES * p:V7X_LANES * (p + 1)]
                qs[j, 2 * p], qs[j, 2 * p + 1] = _head_split(q2, first(r1 - r0))

        def pair_rows(h):
            return slice(V7X_LANES * (h // 2), V7X_LANES * (h // 2 + 1))

        def scores(j, h):
            n, _, (k0, k1), _, _ = jobs[j]
            z[j, h] = _dot(qs.pop((j, h)), kt_scr[n, pair_rows(h), k0:k1])

        def softplus(j, h):
            zz = z[j, h]
            sp = jnp.maximum(zz, 0.0) + jnp.log(1.0 + jnp.exp2(-jnp.abs(zz))) * LOG2_E
            if masks[j] is not None:
                sp = jnp.where(masks[j], sp, 0.0)
            sp16[j, h] = sp.astype(BF16)

        def suffix_sums(j, h):
            _, (r0, r1), (k0, k1), _, fresh = jobs[j]
            carry = jnp.zeros((r1 - r0, 1), F32) if fresh else car_scr[h, r0:r1]
            tot[j, h] = _dot(sp16.pop((j, h)), u_ref[:k1 - k0, :k1 - k0]) + carry

        def weights(j, h):
            a = jnp.exp2(z.pop((j, h)) - tot[j, h])
            if masks[j] is not None:
                a = jnp.where(masks[j], a, 0.0)
            wts[j, h] = a.astype(BF16)
            tot[j, h] = tot[j, h][:, 0:1]

        def weighted_values(j, h):
            n, _, (k0, k1), _, _ = jobs[j]
            v_scr = vo_scr if h % 2 else ve_scr
            pv[j, h] = _dot_nt(wts.pop((j, h)), v_scr[n, pair_rows(h), k0:k1])

        stages = (scores, softplus, suffix_sums, weights, weighted_values)
        n_items = len(jobs) * SB_HEADS
        for step in range(n_items + len(stages) - 1):
            for k, stage in enumerate(stages):
                if 0 <= step - k < n_items:
                    stage(*divmod(step - k, SB_HEADS))

        for j, (n, (r0, r1), _, _, fresh) in enumerate(jobs):
            for p in range(SB_PAIRS):
                new = pv[j, 2 * p] + pv[j, 2 * p + 1]
                acc_scr[p, r0:r1] = new if fresh else acc_scr[p, r0:r1] + new
            for h in range(SB_HEADS):
                car_scr[h, r0:r1] = tot[j, h]

    def lowest_total(r0, r1):
        low = car_scr[0, r0:r1]
        for h in range(1, SB_HEADS):
            low = jnp.minimum(low, car_scr[h, r0:r1])
        return jnp.min(low)

    def cond(st):
        n, low_early, low_late = st
        return jnp.logical_and(n >= 0, jnp.minimum(low_early, low_late) <= SB_LOG2_UNDERFLOW)

    for blk in range(blocks):
        i = step * blocks + blk
        row0 = blk * tq
        sweep([(i, (0, tq), (0, tq), 0, True)], row0)

        def older(n, r0, r1, low, row0=row0):
            def run():
                sweep([(n, (r0, r1), (0, tq), None, False)], row0)
                return lowest_total(r0, r1)
            return lax.cond(low <= SB_LOG2_UNDERFLOW, run, lambda: low)

        def body(st, older=older):
            n, low_early, low_late = st
            return n - 1, older(n, 0, late, low_early), older(n, late, tq, low_late)

        lax.while_loop(cond, body, (i - 1, lowest_total(0, late), lowest_total(late, tq)))
        for p in range(SB_PAIRS):
            o_ref[0, row0:row0 + tq, V7X_LANES * p:V7X_LANES * (p + 1)] = acc_scr[p]


def _sb_prompt(q, kt, vt, *, tq, blocks):
    b, t, w = q.shape
    assert w == SB_WIDTH and t % (tq * blocks) == 0 and kt.shape == (b, w, t)
    n_chunks = t // tq
    kv_spec = pl.BlockSpec((1, w, t), lambda bi, i: (bi, 0, 0))
    q_spec = pl.BlockSpec((1, tq * blocks, w), lambda bi, i: (bi, i, 0))
    return pl.pallas_call(
        functools.partial(_sb_prompt_kernel, tq=tq, blocks=blocks),
        grid=(b, n_chunks // blocks),
        in_specs=[q_spec, kv_spec, kv_spec, pl.BlockSpec((tq, tq), lambda bi, i: (0, 0))],
        out_specs=q_spec,
        out_shape=jax.ShapeDtypeStruct((b, t, w), F32),
        scratch_shapes=[pltpu.VMEM((n_chunks, w, tq), BF16)] * 3
        + [pltpu.VMEM((SB_PAIRS, tq, V7X_LANES), F32), pltpu.VMEM((SB_HEADS, tq, 1), F32)],
        compiler_params=pltpu.CompilerParams(
            dimension_semantics=("arbitrary", "arbitrary"), vmem_limit_bytes=V7X_VMEM_LIMIT),
        name="sb_prompt",
    )(q, kt, vt, _suffix_ones(tq, 1))


def _sb_cached_kernel(q_ref, k_ref, v_ref, ckt_ref, cvt_ref, ud_ref, uc_ref, o_ref, acc_scr, car_scr,
                      *, tq, kn, n_chunks):
    first = lax.broadcasted_iota(jnp.int32, (tq, V7X_LANES), 1) < SB_HEAD_DIM
    r = lax.broadcasted_iota(jnp.int32, (tq, tq), 0)
    c = lax.broadcasted_iota(jnp.int32, (tq, tq), 1)
    causal = c < r
    even = (lax.broadcasted_iota(jnp.int32, (V7X_LANES, kn), 0) // SB_HEAD_DIM) % 2 == 0
    qs = []
    for p in range(SB_PAIRS):
        qs.extend(_head_split(q_ref[0, :, V7X_LANES * p:V7X_LANES * (p + 1)], first))

    def pair_rows(h):
        return slice(V7X_LANES * (h // 2), V7X_LANES * (h // 2 + 1))

    def sweep(n):
        z, wts, tot, pv = {}, {}, {}, {}

        def scores(h):
            if n is None:
                z[h] = _dot_nt(qs[h], k_ref[0, :, pair_rows(h)].astype(BF16))
            else:
                z[h] = _dot(qs[h], ckt_ref[0, pair_rows(h), n * kn:(n + 1) * kn].astype(BF16))

        def weights(h):
            if n is None:
                wts[h], tot[h] = _sb_weights(z.pop(h), ud_ref[...], jnp.zeros((tq, 1), F32), causal)
            else:
                wts[h], tot[h] = _sb_weights(z.pop(h), uc_ref[...], car_scr[h], None)

        def weighted_values(h):
            if n is None:
                v = _head_split(v_ref[0, :, pair_rows(h)], first)[h % 2]
                pv[h] = _dot(wts.pop(h), v.astype(BF16))
            else:
                v = cvt_ref[0, pair_rows(h), n * kn:(n + 1) * kn]
                v = jnp.where(even, 0.0, v) if h % 2 else jnp.where(even, v, 0.0)
                pv[h] = _dot_nt(wts.pop(h), v.astype(BF16))

        stages = (scores, weights, weighted_values)
        for step in range(SB_HEADS + len(stages) - 1):
            for k, stage in enumerate(stages):
                if 0 <= step - k < SB_HEADS:
                    stage(step - k)
        low = None
        for p in range(SB_PAIRS):
            new = pv[2 * p] + pv[2 * p + 1]
            acc_scr[p] = new if n is None else acc_scr[p] + new
        for h in range(SB_HEADS):
            car_scr[h] = tot[h]
            low = tot[h] if low is None else jnp.minimum(low, tot[h])
        return jnp.min(low)

    lowest = sweep(None)
    for n in range(n_chunks - 1, -1, -1):
        lowest = lax.cond(lowest <= SB_LOG2_UNDERFLOW, functools.partial(sweep, n), lambda low=lowest: low)
    for p in range(SB_PAIRS):
        o_ref[0, :, V7X_LANES * p:V7X_LANES * (p + 1)] = acc_scr[p]


def _sb_cached(q, k, v, cache_kt, cache_vt, *, kn):
    b, tq, w = q.shape
    past = cache_kt.shape[2]
    assert w == SB_WIDTH and past % kn == 0 and cache_kt.shape == (b, w, past)
    new_spec = pl.BlockSpec((1, tq, w), lambda bi: (bi, 0, 0))
    cache_spec = pl.BlockSpec((1, w, past), lambda bi: (bi, 0, 0))
    return pl.pallas_call(
        functools.partial(_sb_cached_kernel, tq=tq, kn=kn, n_chunks=past // kn),
        grid=(b,),
        in_specs=[new_spec, new_spec, new_spec, cache_spec, cache_spec,
                  pl.BlockSpec((2 * tq, tq), lambda bi: (0, 0)),
                  pl.BlockSpec((2 * kn, kn), lambda bi: (0, 0))],
        out_specs=new_spec,
        out_shape=jax.ShapeDtypeStruct((b, tq, w), F32),
        scratch_shapes=[pltpu.VMEM((SB_PAIRS, tq, V7X_LANES), F32), pltpu.VMEM((SB_HEADS, tq, 1), F32)],
        compiler_params=pltpu.CompilerParams(
            dimension_semantics=("arbitrary",), vmem_limit_bytes=V7X_VMEM_LIMIT),
        name="sb_cached",
    )(q, k, v, cache_kt, cache_vt, _suffix_ones(tq, 2), _suffix_ones(kn, 2))


def _hgrn_consts(c):
    t = np.arange(c)
    sub = t // HG_BLOCK
    same = sub[:, None] == sub[None, :]
    tri = (same & (t[None, :] <= t[:, None])).astype(np.float32)
    ones = same.astype(np.float32)
    x = sub[:, None] ^ sub[None, :]
    level = np.where(x == 0, 0, np.floor(np.log2(np.maximum(x, 1))).astype(np.int64) + 1)
    level = np.where(t[None, :] <= t[:, None], level, -1).astype(np.int32)
    pre = np.block([[tri, tri], [ones, ones]])
    return jnp.asarray(pre, dtype=BF16), jnp.asarray(level)


def _prod(vecs):
    out = None
    for v in vecs:
        out = v if out is None else out * v
    return out


def _scale_rows(x, vecs):
    blocks = [x[HG_BLOCK * i:HG_BLOCK * (i + 1)] for i in range(len(vecs))]
    return jnp.concatenate([b if v is None else b * v for b, v in zip(blocks, vecs)], axis=0)


def _hgrn_group(read, write, lb, g, pre, lvl, states, *, c, group, v_t):
    nsub = c // HG_BLOCK
    levels = int(round(math.log2(nsub)))
    states = list(states)
    level_masks = [lvl == l for l in range(levels + 1)]
    ch = [dict() for _ in range(group)]
    item = {}

    def head(h):
        return slice(HG_KEY_DIM * h, HG_KEY_DIM * (h + 1))

    def gates(gi):
        hq, hf, v = read(gi)
        f = lb + (1.0 - lb) * jax.nn.sigmoid(hf)
        ch[gi].update(hq=hq, kb=1.0 - f,
                      v_bf=v.astype(BF16), split=jnp.concatenate(_split_bf16(jnp.log(f)), axis=0))

    def prefix(gi):
        ch[gi]["sums"] = _dot(pre, ch[gi].pop("split"))

    def decays(gi):
        sums, hq, kb = (ch[gi].pop(k) for k in ("sums", "hq", "kb"))
        b16 = sums[:c]
        bend = sums[c:]
        q_dec = hq * jnp.exp(b16)
        dmat = jnp.exp(bend)
        k_inv = kb * jnp.exp(-b16)
        k_end = k_inv * dmat
        d = [dmat[HG_BLOCK * i:HG_BLOCK * i + 1, :] for i in range(nsub)]
        q_lv = [q_dec.astype(BF16)]
        k_lv = [k_inv.astype(BF16)]
        if levels >= 1:
            q_lv.append(q_lv[0])
            k_lv.append(k_end.astype(BF16))
        for l in range(2, levels + 1):
            half = 2 ** (l - 1)
            alpha, beta = [], []
            for i in range(nsub):
                mid = (i // (2 * half)) * 2 * half + half
                alpha.append(_prod(d[mid:i]) if i > mid else None)
                beta.append(_prod(d[i + 1:mid]) if i + 1 < mid else None)
            q_lv.append(_scale_rows(q_dec, alpha).astype(BF16))
            k_lv.append(_scale_rows(k_end, beta).astype(BF16))
        if nsub > 1:
            alpha_c = [_prod(d[:i]) if i > 0 else None for i in range(nsub)]
            beta_c = [_prod(d[i + 1:]) if i + 1 < nsub else None for i in range(nsub)]
            q_c = _scale_rows(q_dec, alpha_c).astype(BF16)
            k_c = _scale_rows(k_end, beta_c).astype(BF16)
        else:
            q_c = q_lv[0]
            k_c = k_end.astype(BF16)
        ch[gi].update(q_lv=q_lv, k_lv=k_lv, q_c=q_c, k_c=k_c, d_all=_prod(d))

    def level_scores(gi, h):
        q_lv, k_lv = ch[gi]["q_lv"], ch[gi]["k_lv"]
        item[gi, h] = [_dot_nt(q_lv[l][:, head(h)], k_lv[l][:, head(h)]) for l in range(levels + 1)]

    def combine(gi, h):
        att = jnp.zeros((c, c), F32)
        for l, scores in enumerate(item[gi, h]):
            att = jnp.where(level_masks[l], scores, att)
        item[gi, h] = att.astype(BF16)

    def outputs(gi, h):
        att = item[gi, h]
        v_bf, q_c, k_c = ch[gi]["v_bf"], ch[gi]["q_c"], ch[gi]["k_c"]
        if v_t:
            vt_h = v_bf[head(h), :]
            o_intra = _dot_nt(att, vt_h)
            st_add = _dot(vt_h, k_c[:, head(h)])
        else:
            o_intra = _dot(att, v_bf[:, head(h)])
            st_add = _dot_tn(v_bf[:, head(h)], k_c[:, head(h)])
        item[gi, h] = (o_intra + _dot_nt(q_c[:, head(h)], states[h].astype(BF16)), st_add)

    def finish(gi, h):
        o, st_add = item.pop((gi, h))
        ms = jnp.mean(o * o, axis=-1, keepdims=True)
        write(gi, h, o * lax.rsqrt(ms + NORM_EPS) * g[:, head(h)])
        states[h] = states[h] * ch[gi]["d_all"][:, head(h)] + st_add

    chunk_stages = (gates, prefix, decays)
    item_stages = (level_scores, combine, outputs, finish)
    n_items = group * HG_HEADS
    for step in range(-len(chunk_stages), n_items + len(item_stages) - 1):
        for gi in range(group):
            for k, stage in enumerate(chunk_stages):
                if step == HG_HEADS * gi - len(chunk_stages) + k:
                    stage(gi)
        for k, stage in enumerate(item_stages):
            if 0 <= step - k < n_items:
                stage(*divmod(step - k, HG_HEADS))
    return states


def _hgrn_kernel(*refs, c, group, layer, has_s0, v_t, n_steps):
    if has_s0:
        (hq_ref, hf_ref, hi_ref, lbraw_ref, g_ref, pre_ref, lvl_ref, s0_ref,
         o_ref, sout_ref, st_scr) = refs
    else:
        (hq_ref, hf_ref, hi_ref, lbraw_ref, g_ref, pre_ref, lvl_ref,
         o_ref, sout_ref, st_scr) = refs
        s0_ref = None
    ci = pl.program_id(1)

    @pl.when(ci == 0)
    def _():
        for h in range(HG_HEADS):
            if has_s0:
                st_scr[h] = s0_ref[0, h].T
            else:
                st_scr[h] = jnp.zeros((HG_VAL_DIM, HG_KEY_DIM), F32)

    raw = lbraw_ref[...]
    e = jnp.exp(raw - jnp.max(raw, axis=0, keepdims=True))
    lb = jnp.sum(e[:layer + 1], axis=0, keepdims=True) / jnp.sum(e, axis=0, keepdims=True)

    pre = pre_ref[...]
    lvl = lvl_ref[...]
    g = g_ref[...]
    def read(gi):
        rs = slice(gi * c, (gi + 1) * c)
        return hq_ref[0, rs, :], hf_ref[0, rs, :], hi_ref[0, :, rs] if v_t else hi_ref[0, rs, :]

    def write(gi, h, o):
        o_ref[0, gi * c:(gi + 1) * c, HG_VAL_DIM * h:HG_VAL_DIM * (h + 1)] = o

    states = _hgrn_group(read, write, lb, g, pre, lvl, [st_scr[h] for h in range(HG_HEADS)],
                         c=c, group=group, v_t=v_t)
    for h in range(HG_HEADS):
        st_scr[h] = states[h]

    @pl.when(ci == n_steps - 1)
    def _():
        for h in range(HG_HEADS):
            sout_ref[0, h] = st_scr[h].T


def _hgrn(hq, hf, hi, lb_raw, norm_g, s0, *, layer, c, group, v_t):
    b, t, w = hq.shape
    span = c * group
    assert w == HG_WIDTH and t % span == 0 and c % HG_BLOCK == 0
    n_steps = t // span
    pre, level = _hgrn_consts(c)
    tok = pl.BlockSpec((1, span, w), lambda bi, ci: (bi, ci, 0))
    tok_t = pl.BlockSpec((1, w, span), lambda bi, ci: (bi, 0, ci))
    const2 = lambda shape: pl.BlockSpec(shape, lambda bi, ci: (0, 0))
    st_spec = pl.BlockSpec((1, HG_HEADS, HG_KEY_DIM, HG_VAL_DIM), lambda bi, ci: (bi, 0, 0, 0))
    in_specs = [tok, tok, tok_t if v_t else tok, const2(lb_raw.shape), const2((1, w)),
                const2(pre.shape), const2(level.shape)]
    args = [hq, hf, hi, lb_raw, norm_g.reshape(1, w), pre, level]
    if s0 is not None:
        in_specs.append(st_spec)
        args.append(s0)
    return pl.pallas_call(
        functools.partial(_hgrn_kernel, c=c, group=group, layer=layer, has_s0=s0 is not None, v_t=v_t,
                          n_steps=n_steps),
        grid=(b, n_steps),
        in_specs=in_specs,
        out_specs=(tok, st_spec),
        out_shape=(jax.ShapeDtypeStruct((b, t, w), F32),
                   jax.ShapeDtypeStruct((b, HG_HEADS, HG_KEY_DIM, HG_VAL_DIM), F32)),
        scratch_shapes=[pltpu.VMEM((HG_HEADS, HG_VAL_DIM, HG_KEY_DIM), F32)],
        compiler_params=pltpu.CompilerParams(
            dimension_semantics=("arbitrary", "arbitrary"), vmem_limit_bytes=V7X_VMEM_LIMIT),
        name="hgrn2",
    )(*args)


def _out_kernel(x_ref, osb_ref, ohg_ref, gpre_ref, wg_ref, bg_ref, wa_ref, wb_ref, wo_ref, gp_ref, y_ref,
                *, tm, tiles):
    d_model = x_ref.shape[-1]
    off = np.cumsum([0, SB_WIDTH, HG_WIDTH, d_model, d_model])
    bg = bg_ref[...]
    for t in range(tiles):
        rows = slice(t * tm, (t + 1) * tm)
        x = x_ref[rows, :]
        xn = _rms_norm_bf16(x, gpre_ref[...])
        z_a, hz, ga, gb = (_dot(xn, wg_ref[:, off[i]:off[i + 1]]) for i in range(4))
        u_a = _dot((osb_ref[rows, :] * jax.nn.silu(z_a)).astype(BF16), wa_ref[...])
        u_b = _dot((ohg_ref[rows, :] * jax.nn.silu(hz)).astype(BF16), wb_ref[...])
        merged = (jax.nn.sigmoid(ga + bg[:, :d_model]) * u_a + jax.nn.sigmoid(gb + bg[:, d_model:]) * u_b)
        m = _dot(merged.astype(BF16), wo_ref[...])
        ms = jnp.mean(m * m, axis=-1, keepdims=True)
        y_ref[rows, :] = x + m * lax.rsqrt(ms + NORM_EPS) * gp_ref[...]


def _out_stage(x2d, o_sb, o_hg, g_pre, w_in_bf16, b_gate, wa, wb, wo, g_post, *, tm, tiles):
    n, d = x2d.shape
    span = tm * tiles
    assert n % span == 0
    _, offsets = _in_widths(d)
    w_gate = jnp.concatenate([w_in_bf16[:, offsets[i]:offsets[i + 1]] for i in IN_GATE_GROUPS], axis=1)
    row = lambda w: pl.BlockSpec((span, w), lambda i: (i, 0))
    const = lambda shape: pl.BlockSpec(shape, lambda i: (0, 0))
    return pl.pallas_call(
        functools.partial(_out_kernel, tm=tm, tiles=tiles),
        grid=(n // span,),
        in_specs=[row(d), row(SB_WIDTH), row(HG_WIDTH), const((1, d)), const(w_gate.shape),
                  const((1, 2 * d)), const(wa.shape), const(wb.shape), const(wo.shape), const((1, d))],
        out_specs=row(d),
        out_shape=jax.ShapeDtypeStruct((n, d), F32),
        compiler_params=pltpu.CompilerParams(
            dimension_semantics=("arbitrary",), vmem_limit_bytes=V7X_VMEM_LIMIT),
        name="out_stage",
    )(x2d, o_sb, o_hg, g_pre.reshape(1, d), w_gate, b_gate.reshape(1, 2 * d), wa, wb, wo, g_post.reshape(1, d))


PROJ_TILE, IN_PROJ_TILES, OUT_TILES = 256, 4, 4
SB_BLOCK, SB_BLOCKS = 256, 8
HG_CHUNK, HG_CHUNKS = 128, 16


def _largest_tile(n, cap):
    t = cap
    while n % t:
        t //= 2
    return t


def _trunk_layer(x, cache_k, cache_v, s0, layer, g_pre, w_in, b_gate, lb_raw, hg_norm_g,
                 w_a, w_b, w_out, g_post):
    b, t, d = x.shape
    n = b * t
    heads = (SB_HEADS, SB_HEAD_DIM)
    if cache_k is None:
        tm = _largest_tile(t, PROJ_TILE)
        q, kt, vt, hq, hf, hi = _in_proj(x, g_pre, w_in, tm=tm, tiles=_largest_tile(t // tm, IN_PROJ_TILES),
                                         t_groups=(IN_K, IN_V, IN_HG_V))
        tq = _largest_tile(t, SB_BLOCK)
        o_sb = _sb_prompt(q, kt, vt, tq=tq, blocks=_largest_tile(t // tq, SB_BLOCKS))
        k_out, v_out = (a.reshape(b, *heads, t).transpose(0, 3, 1, 2) for a in (kt, vt))
    else:
        tm = _largest_tile(n, PROJ_TILE)
        outs = _in_proj(x.reshape(1, n, d), g_pre, w_in, tm=tm, tiles=_largest_tile(n // tm, IN_PROJ_TILES),
                        t_groups=())
        q, k, v, hq, hf, hi = (a.reshape(b, t, a.shape[-1]) for a in outs)
        past = cache_k.shape[1]
        cache_kt, cache_vt = (a.transpose(0, 2, 3, 1).reshape(b, SB_WIDTH, past) for a in (cache_k, cache_v))
        o_sb = _sb_cached(q, k, v, cache_kt, cache_vt, kn=_largest_tile(past, SB_BLOCK))
        k_out, v_out = k.reshape(b, t, *heads), v.reshape(b, t, *heads)
    c = _largest_tile(t, HG_CHUNK)
    o_hg, s_new = _hgrn(hq, hf, hi, lb_raw, hg_norm_g.reshape(-1), s0, layer=layer, c=c,
                        group=_largest_tile(t // c, HG_CHUNKS), v_t=cache_k is None)
    flat = lambda a: a.reshape(n, a.shape[-1])
    tm = _largest_tile(n, PROJ_TILE)
    y = _out_stage(flat(x), flat(o_sb), flat(o_hg), g_pre, w_in, b_gate, w_a, w_b, w_out, g_post,
                   tm=tm, tiles=_largest_tile(n // tm, OUT_TILES))
    return y.reshape(b, t, d), k_out, v_out, s_new


def kernel(x_prompt, x_sample, cache_sb_k, cache_sb_v, state_hgrn, g_pre, w_in, b_gate, hg_lb_raw,
           hg_norm_g, w_branch_a, w_branch_b, w_out, g_post):
    depth = w_in.shape[0]
    y_p, y_s = x_prompt, x_sample
    outs = [[] for _ in range(6)]
    for l in range(depth):
        weights = (g_pre[l], w_in[l].astype(BF16), b_gate[l], hg_lb_raw, hg_norm_g[l],
                   w_branch_a[l].astype(BF16), w_branch_b[l].astype(BF16), w_out[l].astype(BF16), g_post[l])
        y_p, kp, vp, sp = _trunk_layer(y_p, None, None, None, l, *weights)
        y_s, ks, vs, ss = _trunk_layer(y_s, cache_sb_k[l], cache_sb_v[l], state_hgrn[l], l, *weights)
        for lst, a in zip(outs, (kp, vp, sp, ks, vs, ss)):
            lst.append(a)
    return (y_p, y_s) + tuple(jnp.stack(lst) for lst in outs)
```

```python
import functools
import math

import numpy as np
import jax
import jax.numpy as jnp
from jax import lax
from jax.experimental import pallas as pl
from jax.experimental.pallas import tpu as pltpu

SB_HEADS = 8
SB_HEAD_DIM = 64
SB_WIDTH = SB_HEADS * SB_HEAD_DIM
HG_HEADS = 4
HG_KEY_DIM = 128
HG_VAL_DIM = 128
HG_WIDTH = HG_HEADS * HG_KEY_DIM
HG_BLOCK = 16
NORM_EPS = 1e-6

V7X_LANES = 128
V7X_VMEM_LIMIT = 56 * 1024 * 1024
SB_PAIRS = SB_WIDTH // V7X_LANES

SB_LOG2_UNDERFLOW = 150.5
SB_LATE_ROWS = 160
LOG2_E = math.log2(math.e)

F32 = jnp.float32
BF16 = jnp.bfloat16

_NT = (((1,), (1,)), ((), ()))
_TN = (((0,), (0,)), ((), ()))


def _dot(a, b):
    return jnp.dot(a, b, preferred_element_type=F32)


def _dot_nt(a, b):
    return lax.dot_general(a, b, _NT, preferred_element_type=F32)


def _dot_tn(a, b):
    return lax.dot_general(a, b, _TN, preferred_element_type=F32)


def _split_bf16(x):
    hi = x.astype(BF16)
    lo = (x - hi.astype(F32)).astype(BF16)
    return hi, lo


IN_Q, IN_K, IN_V, IN_ZA, IN_HG_Q, IN_HG_F, IN_HG_V, IN_HG_Z, IN_GATE_A, IN_GATE_B = range(10)
IN_MIXER_GROUPS = (IN_Q, IN_K, IN_V, IN_HG_Q, IN_HG_F, IN_HG_V)
IN_GATE_GROUPS = (IN_ZA, IN_HG_Z, IN_GATE_A, IN_GATE_B)


def _in_widths(d_model):
    widths = (SB_WIDTH,) * 4 + (HG_WIDTH,) * 4 + (d_model, d_model)
    return widths, np.concatenate([[0], np.cumsum(widths)])


def _rms_norm_bf16(x, g):
    ms = jnp.mean(x * x, axis=-1, keepdims=True)
    return (x * lax.rsqrt(ms + NORM_EPS) * g).astype(BF16)


def _in_proj_kernel(*refs, groups, t_groups, widths, columns, q_scale, tm, tiles):
    if t_groups:
        x_ref, g_ref, w_ref, wt_ref = refs[:4]
        out_refs = refs[4:]
    else:
        x_ref, g_ref, w_ref = refs[:3]
        out_refs = refs[3:]
    for t in range(tiles):
        rows = slice(t * tm, (t + 1) * tm)
        xn = _rms_norm_bf16(x_ref[0, rows, :], g_ref[...])
        if t_groups:
            ht = _dot_nt(wt_ref[...], xn)
            off = 0
            for idx in t_groups:
                out_refs[groups.index(idx)][0, :, rows] = ht[off:off + widths[idx]]
                off += widths[idx]
        for ref, idx in zip(out_refs, groups):
            if idx not in t_groups:
                h = _dot(xn, w_ref[:, columns[idx][0]:columns[idx][1]])
                if idx == IN_Q:
                    h = h * q_scale
                ref[0, rows, :] = h.astype(ref.dtype)


def _in_proj(x, g_pre, w_in_bf16, *, tm, tiles, t_groups):
    b, t, d = x.shape
    widths, offsets = _in_widths(d)
    offsets = tuple(int(o) for o in offsets)
    span = tm * tiles
    assert offsets[-1] == w_in_bf16.shape[1] and t % span == 0
    out_shape, out_specs = [], []
    for idx in IN_MIXER_GROUPS:
        w = widths[idx]
        dt = BF16 if idx == IN_Q else F32
        if idx in t_groups:
            out_shape.append(jax.ShapeDtypeStruct((b, w, t), dt))
            out_specs.append(pl.BlockSpec((1, w, span), lambda bi, ti: (bi, 0, ti)))
        else:
            out_shape.append(jax.ShapeDtypeStruct((b, t, w), dt))
            out_specs.append(pl.BlockSpec((1, span, w), lambda bi, ti: (bi, ti, 0)))
    columns, parts, start = {}, [], 0
    for idx in IN_MIXER_GROUPS:
        if idx not in t_groups:
            parts.append(w_in_bf16[:, offsets[idx]:offsets[idx + 1]])
            columns[idx] = (start, start + widths[idx])
            start += widths[idx]
    w_used = jnp.concatenate(parts, axis=1)
    in_specs = [
        pl.BlockSpec((1, span, d), lambda bi, ti: (bi, ti, 0)),
        pl.BlockSpec((1, d), lambda bi, ti: (0, 0)),
        pl.BlockSpec(w_used.shape, lambda bi, ti: (0, 0)),
    ]
    args = [x, g_pre.reshape(1, d), w_used]
    if t_groups:
        w_t = jnp.concatenate([w_in_bf16[:, offsets[i]:offsets[i + 1]] for i in t_groups], axis=1).T
        in_specs.append(pl.BlockSpec(w_t.shape, lambda bi, ti: (0, 0)))
        args.append(w_t)
    return pl.pallas_call(
        functools.partial(_in_proj_kernel, groups=IN_MIXER_GROUPS, t_groups=t_groups, widths=widths,
                          columns=columns, q_scale=SB_HEAD_DIM ** -0.5 * LOG2_E, tm=tm, tiles=tiles),
        grid=(b, t // span),
        in_specs=in_specs,
        out_specs=tuple(out_specs),
        out_shape=tuple(out_shape),
        compiler_params=pltpu.CompilerParams(
            dimension_semantics=("arbitrary", "arbitrary"), vmem_limit_bytes=V7X_VMEM_LIMIT),
        name="in_proj",
    )(*args)


def _suffix_ones(n, copies):
    r = np.arange(n)
    u = (r[:, None] >= r[None, :]).astype(np.float32)
    return jnp.asarray(np.concatenate([u] * copies, axis=0), dtype=BF16)


def _sb_weights(z, u2, carry, mask):
    sp = jnp.maximum(z, 0.0) + jnp.log(1.0 + jnp.exp2(-jnp.abs(z))) * LOG2_E
    if mask is not None:
        sp = jnp.where(mask, sp, 0.0)
    hi, lo = _split_bf16(sp)
    tot = _dot(jnp.concatenate([hi, lo], axis=1), u2) + carry
    a = jnp.exp2(z - tot)
    if mask is not None:
        a = jnp.where(mask, a, 0.0)
    return a.astype(BF16), tot[:, 0:1]


def _head_split(x2, first):
    zero = jnp.zeros_like(x2)
    return jnp.where(first, x2, zero), jnp.where(first, zero, x2)


def _sb_prompt_kernel(q_ref, kt_ref, vt_ref, u_ref, o_ref, kt_scr, ve_scr, vo_scr, acc_scr, car_scr,
                      *, tq, blocks):
    step = pl.program_id(1)
    n_chunks = kt_scr.shape[0]

    @pl.when(step == 0)
    def _():
        row = lax.broadcasted_iota(jnp.int32, (SB_WIDTH, tq), 0)
        even = (row // SB_HEAD_DIM) % 2 == 0
        for n in range(n_chunks):
            kt_scr[n] = kt_ref[0, :, n * tq:(n + 1) * tq].astype(BF16)
            v = vt_ref[0, :, n * tq:(n + 1) * tq]
            ve_scr[n] = jnp.where(even, v, 0.0).astype(BF16)
            vo_scr[n] = jnp.where(even, 0.0, v).astype(BF16)

    late = min(SB_LATE_ROWS, tq - 16)

    def first(rows):
        return lax.broadcasted_iota(jnp.int32, (rows, V7X_LANES), 1) < SB_HEAD_DIM

    def sweep(jobs, row0):
        qs, z, sp16, tot, wts, pv = {}, {}, {}, {}, {}, {}
        masks = []
        for j, (n, (r0, r1), (k0, k1), offset, fresh) in enumerate(jobs):
            if offset is None:
                masks.append(None)
            else:
                r = lax.broadcasted_iota(jnp.int32, (r1 - r0, k1 - k0), 0)
                c = lax.broadcasted_iota(jnp.int32, (r1 - r0, k1 - k0), 1)
                masks.append(c < r + offset)
            for p in range(SB_PAIRS):
                q2 = q_ref[0, row0 + r0:row0 + r1, V7X_LANES * p:V7X_LANES * (p + 1)]
                qs[j, 2 * p], qs[j, 2 * p + 1] = _head_split(q2, first(r1 - r0))

        def pair_rows(h):
            return slice(V7X_LANES * (h // 2), V7X_LANES * (h // 2 + 1))

        def scores(j, h):
            n, _, (k0, k1), _, _ = jobs[j]
            z[j, h] = _dot(qs.pop((j, h)), kt_scr[n, pair_rows(h), k0:k1])

        def softplus(j, h):
            zz = z[j, h]
            sp = jnp.maximum(zz, 0.0) + jnp.log(1.0 + jnp.exp2(-jnp.abs(zz))) * LOG2_E
            if masks[j] is not None:
                sp = jnp.where(masks[j], sp, 0.0)
            sp16[j, h] = sp.astype(BF16)

        def suffix_sums(j, h):
            _, (r0, r1), (k0, k1), _, fresh = jobs[j]
            carry = jnp.zeros((r1 - r0, 1), F32) if fresh else car_scr[h, r0:r1]
            tot[j, h] = _dot(sp16.pop((j, h)), u_ref[:k1 - k0, :k1 - k0]) + carry

        def weights(j, h):
            a = jnp.exp2(z.pop((j, h)) - tot[j, h])
            if masks[j] is not None:
                a = jnp.where(masks[j], a, 0.0)
            wts[j, h] = a.astype(BF16)
            tot[j, h] = tot[j, h][:, 0:1]

        def weighted_values(j, h):
            n, _, (k0, k1), _, _ = jobs[j]
            v_scr = vo_scr if h % 2 else ve_scr
            pv[j, h] = _dot_nt(wts.pop((j, h)), v_scr[n, pair_rows(h), k0:k1])

        stages = (scores, softplus, suffix_sums, weights, weighted_values)
        n_items = len(jobs) * SB_HEADS
        for step in range(n_items + len(stages) - 1):
            for k, stage in enumerate(stages):
                if 0 <= step - k < n_items:
                    stage(*divmod(step - k, SB_HEADS))

        for j, (n, (r0, r1), _, _, fresh) in enumerate(jobs):
            for p in range(SB_PAIRS):
                new = pv[j, 2 * p] + pv[j, 2 * p + 1]
                acc_scr[p, r0:r1] = new if fresh else acc_scr[p, r0:r1] + new
            for h in range(SB_HEADS):
                car_scr[h, r0:r1] = tot[j, h]

    def lowest_total(r0, r1):
        low = car_scr[0, r0:r1]
        for h in range(1, SB_HEADS):
            low = jnp.minimum(low, car_scr[h, r0:r1])
        return jnp.min(low)

    def cond(st):
        n, low_early, low_late = st
        return jnp.logical_and(n >= 0, jnp.minimum(low_early, low_late) <= SB_LOG2_UNDERFLOW)

    for blk in range(blocks):
        i = step * blocks + blk
        row0 = blk * tq
        sweep([(i, (0, tq), (0, tq), 0, True)], row0)

        def older(n, r0, r1, low, row0=row0):
            def run():
                sweep([(n, (r0, r1), (0, tq), None, False)], row0)
                return lowest_total(r0, r1)
            return lax.cond(low <= SB_LOG2_UNDERFLOW, run, lambda: low)

        def body(st, older=older):
            n, low_early, low_late = st
            return n - 1, older(n, 0, late, low_early), older(n, late, tq, low_late)

        lax.while_loop(cond, body, (i - 1, lowest_total(0, late), lowest_total(late, tq)))
        for p in range(SB_PAIRS):
            o_ref[0, row0:row0 + tq, V7X_LANES * p:V7X_LANES * (p + 1)] = acc_scr[p]


def _sb_prompt(q, kt, vt, *, tq, blocks):
    b, t, w = q.shape
    assert w == SB_WIDTH and t % (tq * blocks) == 0 and kt.shape == (b, w, t)
    n_chunks = t // tq
    kv_spec = pl.BlockSpec((1, w, t), lambda bi, i: (bi, 0, 0))
    q_spec = pl.BlockSpec((1, tq * blocks, w), lambda bi, i: (bi, i, 0))
    return pl.pallas_call(
        functools.partial(_sb_prompt_kernel, tq=tq, blocks=blocks),
        grid=(b, n_chunks // blocks),
        in_specs=[q_spec, kv_spec, kv_spec, pl.BlockSpec((tq, tq), lambda bi, i: (0, 0))],
        out_specs=q_spec,
        out_shape=jax.ShapeDtypeStruct((b, t, w), F32),
        scratch_shapes=[pltpu.VMEM((n_chunks, w, tq), BF16)] * 3
        + [pltpu.VMEM((SB_PAIRS, tq, V7X_LANES), F32), pltpu.VMEM((SB_HEADS, tq, 1), F32)],
        compiler_params=pltpu.CompilerParams(
            dimension_semantics=("arbitrary", "arbitrary"), vmem_limit_bytes=V7X_VMEM_LIMIT),
        name="sb_prompt",
    )(q, kt, vt, _suffix_ones(tq, 1))


def _sb_cached_kernel(q_ref, k_ref, v_ref, ckt_ref, cvt_ref, ud_ref, uc_ref, o_ref, acc_scr, car_scr,
                      *, tq, kn, n_chunks):
    first = lax.broadcasted_iota(jnp.int32, (tq, V7X_LANES), 1) < SB_HEAD_DIM
    r = lax.broadcasted_iota(jnp.int32, (tq, tq), 0)
    c = lax.broadcasted_iota(jnp.int32, (tq, tq), 1)
    causal = c < r
    even = (lax.broadcasted_iota(jnp.int32, (V7X_LANES, kn), 0) // SB_HEAD_DIM) % 2 == 0
    qs = []
    for p in range(SB_PAIRS):
        qs.extend(_head_split(q_ref[0, :, V7X_LANES * p:V7X_LANES * (p + 1)], first))

    def pair_rows(h):
        return slice(V7X_LANES * (h // 2), V7X_LANES * (h // 2 + 1))

    def sweep(n):
        z, wts, tot, pv = {}, {}, {}, {}

        def scores(h):
            if n is None:
                z[h] = _dot_nt(qs[h], k_ref[0, :, pair_rows(h)].astype(BF16))
            else:
                z[h] = _dot(qs[h], ckt_ref[0, pair_rows(h), n * kn:(n + 1) * kn].astype(BF16))

        def weights(h):
            if n is None:
                wts[h], tot[h] = _sb_weights(z.pop(h), ud_ref[...], jnp.zeros((tq, 1), F32), causal)
            else:
                wts[h], tot[h] = _sb_weights(z.pop(h), uc_ref[...], car_scr[h], None)

        def weighted_values(h):
            if n is None:
                v = _head_split(v_ref[0, :, pair_rows(h)], first)[h % 2]
                pv[h] = _dot(wts.pop(h), v.astype(BF16))
            else:
                v = cvt_ref[0, pair_rows(h), n * kn:(n + 1) * kn]
                v = jnp.where(even, 0.0, v) if h % 2 else jnp.where(even, v, 0.0)
                pv[h] = _dot_nt(wts.pop(h), v.astype(BF16))

        stages = (scores, weights, weighted_values)
        for step in range(SB_HEADS + len(stages) - 1):
            for k, stage in enumerate(stages):
                if 0 <= step - k < SB_HEADS:
                    stage(step - k)
        low = None
        for p in range(SB_PAIRS):
            new = pv[2 * p] + pv[2 * p + 1]
            acc_scr[p] = new if n is None else acc_scr[p] + new
        for h in range(SB_HEADS):
            car_scr[h] = tot[h]
            low = tot[h] if low is None else jnp.minimum(low, tot[h])
        return jnp.min(low)

    lowest = sweep(None)
    for n in range(n_chunks - 1, -1, -1):
        lowest = lax.cond(lowest <= SB_LOG2_UNDERFLOW, functools.partial(sweep, n), lambda low=lowest: low)
    for p in range(SB_PAIRS):
        o_ref[0, :, V7X_LANES * p:V7X_LANES * (p + 1)] = acc_scr[p]


def _sb_cached(q, k, v, cache_kt, cache_vt, *, kn):
    b, tq, w = q.shape
    past = cache_kt.shape[2]
    assert w == SB_WIDTH and past % kn == 0 and cache_kt.shape == (b, w, past)
    new_spec = pl.BlockSpec((1, tq, w), lambda bi: (bi, 0, 0))
    cache_spec = pl.BlockSpec((1, w, past), lambda bi: (bi, 0, 0))
    return pl.pallas_call(
        functools.partial(_sb_cached_kernel, tq=tq, kn=kn, n_chunks=past // kn),
        grid=(b,),
        in_specs=[new_spec, new_spec, new_spec, cache_spec, cache_spec,
                  pl.BlockSpec((2 * tq, tq), lambda bi: (0, 0)),
                  pl.BlockSpec((2 * kn, kn), lambda bi: (0, 0))],
        out_specs=new_spec,
        out_shape=jax.ShapeDtypeStruct((b, tq, w), F32),
        scratch_shapes=[pltpu.VMEM((SB_PAIRS, tq, V7X_LANES), F32), pltpu.VMEM((SB_HEADS, tq, 1), F32)],
        compiler_params=pltpu.CompilerParams(
            dimension_semantics=("arbitrary",), vmem_limit_bytes=V7X_VMEM_LIMIT),
        name="sb_cached",
    )(q, k, v, cache_kt, cache_vt, _suffix_ones(tq, 2), _suffix_ones(kn, 2))


def _hgrn_consts(c):
    t = np.arange(c)
    sub = t // HG_BLOCK
    same = sub[:, None] == sub[None, :]
    tri = (same & (t[None, :] <= t[:, None])).astype(np.float32)
    ones = same.astype(np.float32)
    x = sub[:, None] ^ sub[None, :]
    level = np.where(x == 0, 0, np.floor(np.log2(np.maximum(x, 1))).astype(np.int64) + 1)
    level = np.where(t[None, :] <= t[:, None], level, -1).astype(np.int32)
    pre = np.block([[tri, tri], [ones, ones]])
    return jnp.asarray(pre, dtype=BF16), jnp.asarray(level)


def _prod(vecs):
    out = None
    for v in vecs:
        out = v if out is None else out * v
    return out


def _scale_rows(x, vecs):
    blocks = [x[HG_BLOCK * i:HG_BLOCK * (i + 1)] for i in range(len(vecs))]
    return jnp.concatenate([b if v is None else b * v for b, v in zip(blocks, vecs)], axis=0)


def _hgrn_group(read, write, lb, g, pre, lvl, states, *, c, group, v_t):
    nsub = c // HG_BLOCK
    levels = int(round(math.log2(nsub)))
    states = list(states)
    level_masks = [lvl == l for l in range(levels + 1)]
    ch = [dict() for _ in range(group)]
    item = {}

    def head(h):
        return slice(HG_KEY_DIM * h, HG_KEY_DIM * (h + 1))

    def gates(gi):
        hq, hf, v = read(gi)
        f = lb + (1.0 - lb) * jax.nn.sigmoid(hf)
        ch[gi].update(hq=hq, kb=1.0 - f,
                      v_bf=v.astype(BF16), split=jnp.concatenate(_split_bf16(jnp.log(f)), axis=0))

    def prefix(gi):
        ch[gi]["sums"] = _dot(pre, ch[gi].pop("split"))

    def decays(gi):
        sums, hq, kb = (ch[gi].pop(k) for k in ("sums", "hq", "kb"))
        b16 = sums[:c]
        bend = sums[c:]
        q_dec = hq * jnp.exp(b16)
        dmat = jnp.exp(bend)
        k_inv = kb * jnp.exp(-b16)
        k_end = k_inv * dmat
        d = [dmat[HG_BLOCK * i:HG_BLOCK * i + 1, :] for i in range(nsub)]
        q_lv = [q_dec.astype(BF16)]
        k_lv = [k_inv.astype(BF16)]
        if levels >= 1:
            q_lv.append(q_lv[0])
            k_lv.append(k_end.astype(BF16))
        for l in range(2, levels + 1):
            half = 2 ** (l - 1)
            alpha, beta = [], []
            for i in range(nsub):
                mid = (i // (2 * half)) * 2 * half + half
                alpha.append(_prod(d[mid:i]) if i > mid else None)
                beta.append(_prod(d[i + 1:mid]) if i + 1 < mid else None)
            q_lv.append(_scale_rows(q_dec, alpha).astype(BF16))
            k_lv.append(_scale_rows(k_end, beta).astype(BF16))
        if nsub > 1:
            alpha_c = [_prod(d[:i]) if i > 0 else None for i in range(nsub)]
            beta_c = [_prod(d[i + 1:]) if i + 1 < nsub else None for i in range(nsub)]
            q_c = _scale_rows(q_dec, alpha_c).astype(BF16)
            k_c = _scale_rows(k_end, beta_c).astype(BF16)
        else:
            q_c = q_lv[0]
            k_c = k_end.astype(BF16)
        ch[gi].update(q_lv=q_lv, k_lv=k_lv, q_c=q_c, k_c=k_c, d_all=_prod(d))

    def level_scores(gi, h):
        q_lv, k_lv = ch[gi]["q_lv"], ch[gi]["k_lv"]
        item[gi, h] = [_dot_nt(q_lv[l][:, head(h)], k_lv[l][:, head(h)]) for l in range(levels + 1)]

    def combine(gi, h):
        att = jnp.zeros((c, c), F32)
        for l, scores in enumerate(item[gi, h]):
            att = jnp.where(level_masks[l], scores, att)
        item[gi, h] = att.astype(BF16)

    def outputs(gi, h):
        att = item[gi, h]
        v_bf, q_c, k_c = ch[gi]["v_bf"], ch[gi]["q_c"], ch[gi]["k_c"]
        if v_t:
            vt_h = v_bf[head(h), :]
            o_intra = _dot_nt(att, vt_h)
            st_add = _dot(vt_h, k_c[:, head(h)])
        else:
            o_intra = _dot(att, v_bf[:, head(h)])
            st_add = _dot_tn(v_bf[:, head(h)], k_c[:, head(h)])
        item[gi, h] = (o_intra + _dot_nt(q_c[:, head(h)], states[h].astype(BF16)), st_add)

    def finish(gi, h):
        o, st_add = item.pop((gi, h))
        ms = jnp.mean(o * o, axis=-1, keepdims=True)
        write(gi, h, o * lax.rsqrt(ms + NORM_EPS) * g[:, head(h)])
        states[h] = states[h] * ch[gi]["d_all"][:, head(h)] + st_add

    chunk_stages = (gates, prefix, decays)
    item_stages = (level_scores, combine, outputs, finish)
    n_items = group * HG_HEADS
    for step in range(-len(chunk_stages), n_items + len(item_stages) - 1):
        for gi in range(group):
            for k, stage in enumerate(chunk_stages):
                if step == HG_HEADS * gi - len(chunk_stages) + k:
                    stage(gi)
        for k, stage in enumerate(item_stages):
            if 0 <= step - k < n_items:
                stage(*divmod(step - k, HG_HEADS))
    return states


def _hgrn_kernel(*refs, c, group, layer, has_s0, v_t, n_steps):
    if has_s0:
        (hq_ref, hf_ref, hi_ref, lbraw_ref, g_ref, pre_ref, lvl_ref, s0_ref,
         o_ref, sout_ref, st_scr) = refs
    else:
        (hq_ref, hf_ref, hi_ref, lbraw_ref, g_ref, pre_ref, lvl_ref,
         o_ref, sout_ref, st_scr) = refs
        s0_ref = None
    ci = pl.program_id(1)

    @pl.when(ci == 0)
    def _():
        for h in range(HG_HEADS):
            if has_s0:
                st_scr[h] = s0_ref[0, h].T
            else:
                st_scr[h] = jnp.zeros((HG_VAL_DIM, HG_KEY_DIM), F32)

    raw = lbraw_ref[...]
    e = jnp.exp(raw - jnp.max(raw, axis=0, keepdims=True))
    lb = jnp.sum(e[:layer + 1], axis=0, keepdims=True) / jnp.sum(e, axis=0, keepdims=True)

    pre = pre_ref[...]
    lvl = lvl_ref[...]
    g = g_ref[...]
    def read(gi):
        rs = slice(gi * c, (gi + 1) * c)
        return hq_ref[0, rs, :], hf_ref[0, rs, :], hi_ref[0, :, rs] if v_t else hi_ref[0, rs, :]

    def write(gi, h, o):
        o_ref[0, gi * c:(gi + 1) * c, HG_VAL_DIM * h:HG_VAL_DIM * (h + 1)] = o

    states = _hgrn_group(read, write, lb, g, pre, lvl, [st_scr[h] for h in range(HG_HEADS)],
                         c=c, group=group, v_t=v_t)
    for h in range(HG_HEADS):
        st_scr[h] = states[h]

    @pl.when(ci == n_steps - 1)
    def _():
        for h in range(HG_HEADS):
            sout_ref[0, h] = st_scr[h].T


def _hgrn(hq, hf, hi, lb_raw, norm_g, s0, *, layer, c, group, v_t):
    b, t, w = hq.shape
    span = c * group
    assert w == HG_WIDTH and t % span == 0 and c % HG_BLOCK == 0
    n_steps = t // span
    pre, level = _hgrn_consts(c)
    tok = pl.BlockSpec((1, span, w), lambda bi, ci: (bi, ci, 0))
    tok_t = pl.BlockSpec((1, w, span), lambda bi, ci: (bi, 0, ci))
    const2 = lambda shape: pl.BlockSpec(shape, lambda bi, ci: (0, 0))
    st_spec = pl.BlockSpec((1, HG_HEADS, HG_KEY_DIM, HG_VAL_DIM), lambda bi, ci: (bi, 0, 0, 0))
    in_specs = [tok, tok, tok_t if v_t else tok, const2(lb_raw.shape), const2((1, w)),
                const2(pre.shape), const2(level.shape)]
    args = [hq, hf, hi, lb_raw, norm_g.reshape(1, w), pre, level]
    if s0 is not None:
        in_specs.append(st_spec)
        args.append(s0)
    return pl.pallas_call(
        functools.partial(_hgrn_kernel, c=c, group=group, layer=layer, has_s0=s0 is not None, v_t=v_t,
                          n_steps=n_steps),
        grid=(b, n_steps),
        in_specs=in_specs,
        out_specs=(tok, st_spec),
        out_shape=(jax.ShapeDtypeStruct((b, t, w), F32),
                   jax.ShapeDtypeStruct((b, HG_HEADS, HG_KEY_DIM, HG_VAL_DIM), F32)),
        scratch_shapes=[pltpu.VMEM((HG_HEADS, HG_VAL_DIM, HG_KEY_DIM), F32)],
        compiler_params=pltpu.CompilerParams(
            dimension_semantics=("arbitrary", "arbitrary"), vmem_limit_bytes=V7X_VMEM_LIMIT),
        name="hgrn2",
    )(*args)


def _out_kernel(x_ref, osb_ref, ohg_ref, gpre_ref, wg_ref, bg_ref, wa_ref, wb_ref, wo_ref, gp_ref, y_ref,
                *, tm, tiles):
    d_model = x_ref.shape[-1]
    off = np.cumsum([0, SB_WIDTH, HG_WIDTH, d_model, d_model])
    bg = bg_ref[...]
    for t in range(tiles):
        rows = slice(t * tm, (t + 1) * tm)
        x = x_ref[rows, :]
        xn = _rms_norm_bf16(x, gpre_ref[...])
        z_a, hz, ga, gb = (_dot(xn, wg_ref[:, off[i]:off[i + 1]]) for i in range(4))
        u_a = _dot((osb_ref[rows, :] * jax.nn.silu(z_a)).astype(BF16), wa_ref[...])
        u_b = _dot((ohg_ref[rows, :] * jax.nn.silu(hz)).astype(BF16), wb_ref[...])
        merged = (jax.nn.sigmoid(ga + bg[:, :d_model]) * u_a + jax.nn.sigmoid(gb + bg[:, d_model:]) * u_b)
        m = _dot(merged.astype(BF16), wo_ref[...])
        ms = jnp.mean(m * m, axis=-1, keepdims=True)
        y_ref[rows, :] = x + m * lax.rsqrt(ms + NORM_EPS) * gp_ref[...]


def _out_stage(x2d, o_sb, o_hg, g_pre, w_in_bf16, b_gate, wa, wb, wo, g_post, *, tm, tiles):
    n, d = x2d.shape
    span = tm * tiles
    assert n % span == 0
    _, offsets = _in_widths(d)
    w_gate = jnp.concatenate([w_in_bf16[:, offsets[i]:offsets[i + 1]] for i in IN_GATE_GROUPS], axis=1)
    row = lambda w: pl.BlockSpec((span, w), lambda i: (i, 0))
    const = lambda shape: pl.BlockSpec(shape, lambda i: (0, 0))
    return pl.pallas_call(
        functools.partial(_out_kernel, tm=tm, tiles=tiles),
        grid=(n // span,),
        in_specs=[row(d), row(SB_WIDTH), row(HG_WIDTH), const((1, d)), const(w_gate.shape),
                  const((1, 2 * d)), const(wa.shape), const(wb.shape), const(wo.shape), const((1, d))],
        out_specs=row(d),
        out_shape=jax.ShapeDtypeStruct((n, d), F32),
        compiler_params=pltpu.CompilerParams(
            dimension_semantics=("arbitrary",), vmem_limit_bytes=V7X_VMEM_LIMIT),
        name="out_stage",
    )(x2d, o_sb, o_hg, g_pre.reshape(1, d), w_gate, b_gate.reshape(1, 2 * d), wa, wb, wo, g_post.reshape(1, d))


PROJ_TILE, IN_PROJ_TILES, OUT_TILES = 256, 4, 4
SB_BLOCK, SB_BLOCKS = 256, 8
HG_CHUNK, HG_CHUNKS = 128, 16


def _largest_tile(n, cap):
    t = cap
    while n % t:
        t //= 2
    return t


def _trunk_layer(x, cache_k, cache_v, s0, layer, g_pre, w_in, b_gate, lb_raw, hg_norm_g,
                 w_a, w_b, w_out, g_post):
    b, t, d = x.shape
    n = b * t
    heads = (SB_HEADS, SB_HEAD_DIM)
    if cache_k is None:
        tm = _largest_tile(t, PROJ_TILE)
        q, kt, vt, hq, hf, hi = _in_proj(x, g_pre, w_in, tm=tm, tiles=_largest_tile(t // tm, IN_PROJ_TILES),
                                         t_groups=(IN_K, IN_V, IN_HG_V))
        tq = _largest_tile(t, SB_BLOCK)
        o_sb = _sb_prompt(q, kt, vt, tq=tq, blocks=_largest_tile(t // tq, SB_BLOCKS))
        k_out, v_out = (a.reshape(b, *heads, t).transpose(0, 3, 1, 2) for a in (kt, vt))
    else:
        tm = _largest_tile(n, PROJ_TILE)
        outs = _in_proj(x.reshape(1, n, d), g_pre, w_in, tm=tm, tiles=_largest_tile(n // tm, IN_PROJ_TILES),
                        t_groups=())
        q, k, v, hq, hf, hi = (a.reshape(b, t, a.shape[-1]) for a in outs)
        past = cache_k.shape[1]
        cache_kt, cache_vt = (a.transpose(0, 2, 3, 1).reshape(b, SB_WIDTH, past) for a in (cache_k, cache_v))
        o_sb = _sb_cached(q, k, v, cache_kt, cache_vt, kn=_largest_tile(past, SB_BLOCK))
        k_out, v_out = k.reshape(b, t, *heads), v.reshape(b, t, *heads)
    c = _largest_tile(t, HG_CHUNK)
    o_hg, s_new = _hgrn(hq, hf, hi, lb_raw, hg_norm_g.reshape(-1), s0, layer=layer, c=c,
                        group=_largest_tile(t // c, HG_CHUNKS), v_t=cache_k is None)
    flat = lambda a: a.reshape(n, a.shape[-1])
    tm = _largest_tile(n, PROJ_TILE)
    y = _out_stage(flat(x), flat(o_sb), flat(o_hg), g_pre, w_in, b_gate, w_a, w_b, w_out, g_post,
                   tm=tm, tiles=_largest_tile(n // tm, OUT_TILES))
    return y.reshape(b, t, d), k_out, v_out, s_new


def kernel(x_prompt, x_sample, cache_sb_k, cache_sb_v, state_hgrn, g_pre, w_in, b_gate, hg_lb_raw,
           hg_norm_g, w_branch_a, w_branch_b, w_out, g_post):
    depth = w_in.shape[0]
    y_p, y_s = x_prompt, x_sample
    outs = [[] for _ in range(6)]
    for l in range(depth):
        weights = (g_pre[l], w_in[l].astype(BF16), b_gate[l], hg_lb_raw, hg_norm_g[l],
                   w_branch_a[l].astype(BF16), w_branch_b[l].astype(BF16), w_out[l].astype(BF16), g_post[l])
        y_p, kp, vp, sp = _trunk_layer(y_p, None, None, None, l, *weights)
        y_s, ks, vs, ss = _trunk_layer(y_s, cache_sb_k[l], cache_sb_v[l], state_hgrn[l], l, *weights)
        for lst, a in zip(outs, (kp, vp, sp, ks, vs, ss)):
            lst.append(a)
    return (y_p, y_s) + tuple(jnp.stack(lst) for lst in outs)
```

```python
import functools
import math

import numpy as np
import jax
import jax.numpy as jnp
from jax import lax
from jax.experimental import pallas as pl
from jax.experimental.pallas import tpu as pltpu

SB_HEADS = 8
SB_HEAD_DIM = 64
SB_WIDTH = SB_HEADS * SB_HEAD_DIM
HG_HEADS = 4
HG_KEY_DIM = 128
HG_VAL_DIM = 128
HG_WIDTH = HG_HEADS * HG_KEY_DIM
HG_BLOCK = 16
NORM_EPS = 1e-6

V7X_LANES = 128
V7X_VMEM_LIMIT = 56 * 1024 * 1024
SB_PAIRS = SB_WIDTH // V7X_LANES

SB_LOG2_UNDERFLOW = 150.5
SB_LATE_ROWS = 144
LOG2_E = math.log2(math.e)

F32 = jnp.float32
BF16 = jnp.bfloat16

_NT = (((1,), (1,)), ((), ()))
_TN = (((0,), (0,)), ((), ()))


def _dot(a, b):
    return jnp.dot(a, b, preferred_element_type=F32)


def _dot_nt(a, b):
    return lax.dot_general(a, b, _NT, preferred_element_type=F32)


def _dot_tn(a, b):
    return lax.dot_general(a, b, _TN, preferred_element_type=F32)


def _split_bf16(x):
    hi = x.astype(BF16)
    lo = (x - hi.astype(F32)).astype(BF16)
    return hi, lo


IN_Q, IN_K, IN_V, IN_ZA, IN_HG_Q, IN_HG_F, IN_HG_V, IN_HG_Z, IN_GATE_A, IN_GATE_B = range(10)
IN_MIXER_GROUPS = (IN_Q, IN_K, IN_V, IN_HG_Q, IN_HG_F, IN_HG_V)
IN_GATE_GROUPS = (IN_ZA, IN_HG_Z, IN_GATE_A, IN_GATE_B)


def _in_widths(d_model):
    widths = (SB_WIDTH,) * 4 + (HG_WIDTH,) * 4 + (d_model, d_model)
    return widths, np.concatenate([[0], np.cumsum(widths)])


def _rms_norm_bf16(x, g):
    ms = jnp.mean(x * x, axis=-1, keepdims=True)
    return (x * lax.rsqrt(ms + NORM_EPS) * g).astype(BF16)


def _in_proj_kernel(*refs, groups, t_groups, widths, columns, q_scale, tm, tiles):
    if t_groups:
        x_ref, g_ref, w_ref, wt_ref = refs[:4]
        out_refs = refs[4:]
    else:
        x_ref, g_ref, w_ref = refs[:3]
        out_refs = refs[3:]
    for t in range(tiles):
        rows = slice(t * tm, (t + 1) * tm)
        xn = _rms_norm_bf16(x_ref[0, rows, :], g_ref[...])
        if t_groups:
            ht = _dot_nt(wt_ref[...], xn)
            off = 0
            for idx in t_groups:
                out_refs[groups.index(idx)][0, :, rows] = ht[off:off + widths[idx]]
                off += widths[idx]
        for ref, idx in zip(out_refs, groups):
            if idx not in t_groups:
                h = _dot(xn, w_ref[:, columns[idx][0]:columns[idx][1]])
                if idx == IN_Q:
                    h = h * q_scale
                ref[0, rows, :] = h.astype(ref.dtype)


def _in_proj(x, g_pre, w_in_bf16, *, tm, tiles, t_groups):
    b, t, d = x.shape
    widths, offsets = _in_widths(d)
    offsets = tuple(int(o) for o in offsets)
    span = tm * tiles
    assert offsets[-1] == w_in_bf16.shape[1] and t % span == 0
    out_shape, out_specs = [], []
    for idx in IN_MIXER_GROUPS:
        w = widths[idx]
        dt = BF16 if idx == IN_Q else F32
        if idx in t_groups:
            out_shape.append(jax.ShapeDtypeStruct((b, w, t), dt))
            out_specs.append(pl.BlockSpec((1, w, span), lambda bi, ti: (bi, 0, ti)))
        else:
            out_shape.append(jax.ShapeDtypeStruct((b, t, w), dt))
            out_specs.append(pl.BlockSpec((1, span, w), lambda bi, ti: (bi, ti, 0)))
    columns, parts, start = {}, [], 0
    for idx in IN_MIXER_GROUPS:
        if idx not in t_groups:
            parts.append(w_in_bf16[:, offsets[idx]:offsets[idx + 1]])
            columns[idx] = (start, start + widths[idx])
            start += widths[idx]
    w_used = jnp.concatenate(parts, axis=1)
    in_specs = [
        pl.BlockSpec((1, span, d), lambda bi, ti: (bi, ti, 0)),
        pl.BlockSpec((1, d), lambda bi, ti: (0, 0)),
        pl.BlockSpec(w_used.shape, lambda bi, ti: (0, 0)),
    ]
    args = [x, g_pre.reshape(1, d), w_used]
    if t_groups:
        w_t = jnp.concatenate([w_in_bf16[:, offsets[i]:offsets[i + 1]] for i in t_groups], axis=1).T
        in_specs.append(pl.BlockSpec(w_t.shape, lambda bi, ti: (0, 0)))
        args.append(w_t)
    return pl.pallas_call(
        functools.partial(_in_proj_kernel, groups=IN_MIXER_GROUPS, t_groups=t_groups, widths=widths,
                          columns=columns, q_scale=SB_HEAD_DIM ** -0.5 * LOG2_E, tm=tm, tiles=tiles),
        grid=(b, t // span),
        in_specs=in_specs,
        out_specs=tuple(out_specs),
        out_shape=tuple(out_shape),
        compiler_params=pltpu.CompilerParams(
            dimension_semantics=("arbitrary", "arbitrary"), vmem_limit_bytes=V7X_VMEM_LIMIT),
        name="in_proj",
    )(*args)


def _suffix_ones(n, copies):
    r = np.arange(n)
    u = (r[:, None] >= r[None, :]).astype(np.float32)
    return jnp.asarray(np.concatenate([u] * copies, axis=0), dtype=BF16)


def _sb_weights(z, u2, carry, mask):
    sp = jnp.maximum(z, 0.0) + jnp.log(1.0 + jnp.exp2(-jnp.abs(z))) * LOG2_E
    if mask is not None:
        sp = jnp.where(mask, sp, 0.0)
    hi, lo = _split_bf16(sp)
    tot = _dot(jnp.concatenate([hi, lo], axis=1), u2) + carry
    a = jnp.exp2(z - tot)
    if mask is not None:
        a = jnp.where(mask, a, 0.0)
    return a.astype(BF16), tot[:, 0:1]


def _head_split(x2, first):
    zero = jnp.zeros_like(x2)
    return jnp.where(first, x2, zero), jnp.where(first, zero, x2)


def _sb_prompt_kernel(q_ref, kt_ref, vt_ref, u_ref, o_ref, kt_scr, ve_scr, vo_scr, acc_scr, car_scr,
                      *, tq, blocks):
    step = pl.program_id(1)
    n_chunks = kt_scr.shape[0]

    @pl.when(step == 0)
    def _():
        row = lax.broadcasted_iota(jnp.int32, (SB_WIDTH, tq), 0)
        even = (row // SB_HEAD_DIM) % 2 == 0
        for n in range(n_chunks):
            kt_scr[n] = kt_ref[0, :, n * tq:(n + 1) * tq].astype(BF16)
            v = vt_ref[0, :, n * tq:(n + 1) * tq]
            ve_scr[n] = jnp.where(even, v, 0.0).astype(BF16)
            vo_scr[n] = jnp.where(even, 0.0, v).astype(BF16)

    late = min(SB_LATE_ROWS, tq - 16)

    def first(rows):
        return lax.broadcasted_iota(jnp.int32, (rows, V7X_LANES), 1) < SB_HEAD_DIM

    def sweep(jobs, row0):
        qs, z, sp16, tot, wts, pv = {}, {}, {}, {}, {}, {}
        masks = []
        for j, (n, (r0, r1), (k0, k1), offset, fresh) in enumerate(jobs):
            if offset is None:
                masks.append(None)
            else:
                r = lax.broadcasted_iota(jnp.int32, (r1 - r0, k1 - k0), 0)
                c = lax.broadcasted_iota(jnp.int32, (r1 - r0, k1 - k0), 1)
                masks.append(c < r + offset)
            for p in range(SB_PAIRS):
                q2 = q_ref[0, row0 + r0:row0 + r1, V7X_LANES * p:V7X_LANES * (p + 1)]
                qs[j, 2 * p], qs[j, 2 * p + 1] = _head_split(q2, first(r1 - r0))

        def pair_rows(h):
            return slice(V7X_LANES * (h // 2), V7X_LANES * (h // 2 + 1))

        def scores(j, h):
            n, _, (k0, k1), _, _ = jobs[j]
            z[j, h] = _dot(qs.pop((j, h)), kt_scr[n, pair_rows(h), k0:k1])

        def softplus(j, h):
            zz = z[j, h]
            sp = jnp.maximum(zz, 0.0) + jnp.log(1.0 + jnp.exp2(-jnp.abs(zz))) * LOG2_E
            if masks[j] is not None:
                sp = jnp.where(masks[j], sp, 0.0)
            sp16[j, h] = sp.astype(BF16)

        def suffix_sums(j, h):
            _, (r0, r1), (k0, k1), _, fresh = jobs[j]
            carry = jnp.zeros((r1 - r0, 1), F32) if fresh else car_scr[h, r0:r1]
            tot[j, h] = _dot(sp16.pop((j, h)), u_ref[:k1 - k0, :k1 - k0]) + carry

        def weights(j, h):
            a = jnp.exp2(z.pop((j, h)) - tot[j, h])
            if masks[j] is not None:
                a = jnp.where(masks[j], a, 0.0)
            wts[j, h] = a.astype(BF16)
            tot[j, h] = tot[j, h][:, 0:1]

        def weighted_values(j, h):
            n, _, (k0, k1), _, _ = jobs[j]
            v_scr = vo_scr if h % 2 else ve_scr
            pv[j, h] = _dot_nt(wts.pop((j, h)), v_scr[n, pair_rows(h), k0:k1])

        stages = (scores, softplus, suffix_sums, weights, weighted_values)
        n_items = len(jobs) * SB_HEADS
        for step in range(n_items + len(stages) - 1):
            for k, stage in enumerate(stages):
                if 0 <= step - k < n_items:
                    stage(*divmod(step - k, SB_HEADS))

        for j, (n, (r0, r1), _, _, fresh) in enumerate(jobs):
            for p in range(SB_PAIRS):
                new = pv[j, 2 * p] + pv[j, 2 * p + 1]
                acc_scr[p, r0:r1] = new if fresh else acc_scr[p, r0:r1] + new
            for h in range(SB_HEADS):
                car_scr[h, r0:r1] = tot[j, h]

    def lowest_total(r0, r1):
        low = car_scr[0, r0:r1]
        for h in range(1, SB_HEADS):
            low = jnp.minimum(low, car_scr[h, r0:r1])
        return jnp.min(low)

    def cond(st):
        n, low_early, low_late = st
        return jnp.logical_and(n >= 0, jnp.minimum(low_early, low_late) <= SB_LOG2_UNDERFLOW)

    for blk in range(blocks):
        i = step * blocks + blk
        row0 = blk * tq
        sweep([(i, (0, tq), (0, tq), 0, True)], row0)

        def older(n, r0, r1, low, row0=row0):
            def run():
                sweep([(n, (r0, r1), (0, tq), None, False)], row0)
                return lowest_total(r0, r1)
            return lax.cond(low <= SB_LOG2_UNDERFLOW, run, lambda: low)

        def body(st, older=older):
            n, low_early, low_late = st
            return n - 1, older(n, 0, late, low_early), older(n, late, tq, low_late)

        lax.while_loop(cond, body, (i - 1, lowest_total(0, late), lowest_total(late, tq)))
        for p in range(SB_PAIRS):
            o_ref[0, row0:row0 + tq, V7X_LANES * p:V7X_LANES * (p + 1)] = acc_scr[p]


def _sb_prompt(q, kt, vt, *, tq, blocks):
    b, t, w = q.shape
    assert w == SB_WIDTH and t % (tq * blocks) == 0 and kt.shape == (b, w, t)
    n_chunks = t // tq
    kv_spec = pl.BlockSpec((1, w, t), lambda bi, i: (bi, 0, 0))
    q_spec = pl.BlockSpec((1, tq * blocks, w), lambda bi, i: (bi, i, 0))
    return pl.pallas_call(
        functools.partial(_sb_prompt_kernel, tq=tq, blocks=blocks),
        grid=(b, n_chunks // blocks),
        in_specs=[q_spec, kv_spec, kv_spec, pl.BlockSpec((tq, tq), lambda bi, i: (0, 0))],
        out_specs=q_spec,
        out_shape=jax.ShapeDtypeStruct((b, t, w), F32),
        scratch_shapes=[pltpu.VMEM((n_chunks, w, tq), BF16)] * 3
        + [pltpu.VMEM((SB_PAIRS, tq, V7X_LANES), F32), pltpu.VMEM((SB_HEADS, tq, 1), F32)],
        compiler_params=pltpu.CompilerParams(
            dimension_semantics=("arbitrary", "arbitrary"), vmem_limit_bytes=V7X_VMEM_LIMIT),
        name="sb_prompt",
    )(q, kt, vt, _suffix_ones(tq, 1))


def _sb_cached_kernel(q_ref, k_ref, v_ref, ckt_ref, cvt_ref, ud_ref, uc_ref, o_ref, acc_scr, car_scr,
                      *, tq, kn, n_chunks):
    first = lax.broadcasted_iota(jnp.int32, (tq, V7X_LANES), 1) < SB_HEAD_DIM
    r = lax.broadcasted_iota(jnp.int32, (tq, tq), 0)
    c = lax.broadcasted_iota(jnp.int32, (tq, tq), 1)
    causal = c < r
    even = (lax.broadcasted_iota(jnp.int32, (V7X_LANES, kn), 0) // SB_HEAD_DIM) % 2 == 0
    qs = []
    for p in range(SB_PAIRS):
        qs.extend(_head_split(q_ref[0, :, V7X_LANES * p:V7X_LANES * (p + 1)], first))

    def pair_rows(h):
        return slice(V7X_LANES * (h // 2), V7X_LANES * (h // 2 + 1))

    def sweep(n):
        z, wts, tot, pv = {}, {}, {}, {}

        def scores(h):
            if n is None:
                z[h] = _dot_nt(qs[h], k_ref[0, :, pair_rows(h)].astype(BF16))
            else:
                z[h] = _dot(qs[h], ckt_ref[0, pair_rows(h), n * kn:(n + 1) * kn].astype(BF16))

        def weights(h):
            if n is None:
                wts[h], tot[h] = _sb_weights(z.pop(h), ud_ref[...], jnp.zeros((tq, 1), F32), causal)
            else:
                wts[h], tot[h] = _sb_weights(z.pop(h), uc_ref[...], car_scr[h], None)

        def weighted_values(h):
            if n is None:
                v = _head_split(v_ref[0, :, pair_rows(h)], first)[h % 2]
                pv[h] = _dot(wts.pop(h), v.astype(BF16))
            else:
                v = cvt_ref[0, pair_rows(h), n * kn:(n + 1) * kn]
                v = jnp.where(even, 0.0, v) if h % 2 else jnp.where(even, v, 0.0)
                pv[h] = _dot_nt(wts.pop(h), v.astype(BF16))

        stages = (scores, weights, weighted_values)
        for step in range(SB_HEADS + len(stages) - 1):
            for k, stage in enumerate(stages):
                if 0 <= step - k < SB_HEADS:
                    stage(step - k)
        low = None
        for p in range(SB_PAIRS):
            new = pv[2 * p] + pv[2 * p + 1]
            acc_scr[p] = new if n is None else acc_scr[p] + new
        for h in range(SB_HEADS):
            car_scr[h] = tot[h]
            low = tot[h] if low is None else jnp.minimum(low, tot[h])
        return jnp.min(low)

    lowest = sweep(None)
    for n in range(n_chunks - 1, -1, -1):
        lowest = lax.cond(lowest <= SB_LOG2_UNDERFLOW, functools.partial(sweep, n), lambda low=lowest: low)
    for p in range(SB_PAIRS):
        o_ref[0, :, V7X_LANES * p:V7X_LANES * (p + 1)] = acc_scr[p]


def _sb_cached(q, k, v, cache_kt, cache_vt, *, kn):
    b, tq, w = q.shape
    past = cache_kt.shape[2]
    assert w == SB_WIDTH and past % kn == 0 and cache_kt.shape == (b, w, past)
    new_spec = pl.BlockSpec((1, tq, w), lambda bi: (bi, 0, 0))
    cache_spec = pl.BlockSpec((1, w, past), lambda bi: (bi, 0, 0))
    return pl.pallas_call(
        functools.partial(_sb_cached_kernel, tq=tq, kn=kn, n_chunks=past // kn),
        grid=(b,),
        in_specs=[new_spec, new_spec, new_spec, cache_spec, cache_spec,
                  pl.BlockSpec((2 * tq, tq), lambda bi: (0, 0)),
                  pl.BlockSpec((2 * kn, kn), lambda bi: (0, 0))],
        out_specs=new_spec,
        out_shape=jax.ShapeDtypeStruct((b, tq, w), F32),
        scratch_shapes=[pltpu.VMEM((SB_PAIRS, tq, V7X_LANES), F32), pltpu.VMEM((SB_HEADS, tq, 1), F32)],
        compiler_params=pltpu.CompilerParams(
            dimension_semantics=("arbitrary",), vmem_limit_bytes=V7X_VMEM_LIMIT),
        name="sb_cached",
    )(q, k, v, cache_kt, cache_vt, _suffix_ones(tq, 2), _suffix_ones(kn, 2))


def _hgrn_consts(c):
    t = np.arange(c)
    sub = t // HG_BLOCK
    same = sub[:, None] == sub[None, :]
    tri = (same & (t[None, :] <= t[:, None])).astype(np.float32)
    ones = same.astype(np.float32)
    x = sub[:, None] ^ sub[None, :]
    level = np.where(x == 0, 0, np.floor(np.log2(np.maximum(x, 1))).astype(np.int64) + 1)
    level = np.where(t[None, :] <= t[:, None], level, -1).astype(np.int32)
    pre = np.block([[tri, tri], [ones, ones]])
    return jnp.asarray(pre, dtype=BF16), jnp.asarray(level)


def _prod(vecs):
    out = None
    for v in vecs:
        out = v if out is None else out * v
    return out


def _scale_rows(x, vecs):
    blocks = [x[HG_BLOCK * i:HG_BLOCK * (i + 1)] for i in range(len(vecs))]
    return jnp.concatenate([b if v is None else b * v for b, v in zip(blocks, vecs)], axis=0)


def _hgrn_group(read, write, lb, g, pre, lvl, states, *, c, group, v_t):
    nsub = c // HG_BLOCK
    levels = int(round(math.log2(nsub)))
    states = list(states)
    level_masks = [lvl == l for l in range(levels + 1)]
    ch = [dict() for _ in range(group)]
    item = {}

    def head(h):
        return slice(HG_KEY_DIM * h, HG_KEY_DIM * (h + 1))

    def gates(gi):
        hq, hf, v = read(gi)
        f = lb + (1.0 - lb) * jax.nn.sigmoid(hf)
        ch[gi].update(hq=hq, kb=1.0 - f,
                      v_bf=v.astype(BF16), split=jnp.concatenate(_split_bf16(jnp.log(f)), axis=0))

    def prefix(gi):
        ch[gi]["sums"] = _dot(pre, ch[gi].pop("split"))

    def decays(gi):
        sums, hq, kb = (ch[gi].pop(k) for k in ("sums", "hq", "kb"))
        b16 = sums[:c]
        bend = sums[c:]
        q_dec = hq * jnp.exp(b16)
        dmat = jnp.exp(bend)
        k_inv = kb * jnp.exp(-b16)
        k_end = k_inv * dmat
        d = [dmat[HG_BLOCK * i:HG_BLOCK * i + 1, :] for i in range(nsub)]
        q_lv = [q_dec.astype(BF16)]
        k_lv = [k_inv.astype(BF16)]
        if levels >= 1:
            q_lv.append(q_lv[0])
            k_lv.append(k_end.astype(BF16))
        for l in range(2, levels + 1):
            half = 2 ** (l - 1)
            alpha, beta = [], []
            for i in range(nsub):
                mid = (i // (2 * half)) * 2 * half + half
                alpha.append(_prod(d[mid:i]) if i > mid else None)
                beta.append(_prod(d[i + 1:mid]) if i + 1 < mid else None)
            q_lv.append(_scale_rows(q_dec, alpha).astype(BF16))
            k_lv.append(_scale_rows(k_end, beta).astype(BF16))
        if nsub > 1:
            alpha_c = [_prod(d[:i]) if i > 0 else None for i in range(nsub)]
            beta_c = [_prod(d[i + 1:]) if i + 1 < nsub else None for i in range(nsub)]
            q_c = _scale_rows(q_dec, alpha_c).astype(BF16)
            k_c = _scale_rows(k_end, beta_c).astype(BF16)
        else:
            q_c = q_lv[0]
            k_c = k_end.astype(BF16)
        ch[gi].update(q_lv=q_lv, k_lv=k_lv, q_c=q_c, k_c=k_c, d_all=_prod(d))

    def level_scores(gi, h):
        q_lv, k_lv = ch[gi]["q_lv"], ch[gi]["k_lv"]
        item[gi, h] = [_dot_nt(q_lv[l][:, head(h)], k_lv[l][:, head(h)]) for l in range(levels + 1)]

    def combine(gi, h):
        att = jnp.zeros((c, c), F32)
        for l, scores in enumerate(item[gi, h]):
            att = jnp.where(level_masks[l], scores, att)
        item[gi, h] = att.astype(BF16)

    def outputs(gi, h):
        att = item[gi, h]
        v_bf, q_c, k_c = ch[gi]["v_bf"], ch[gi]["q_c"], ch[gi]["k_c"]
        if v_t:
            vt_h = v_bf[head(h), :]
            o_intra = _dot_nt(att, vt_h)
            st_add = _dot(vt_h, k_c[:, head(h)])
        else:
            o_intra = _dot(att, v_bf[:, head(h)])
            st_add = _dot_tn(v_bf[:, head(h)], k_c[:, head(h)])
        item[gi, h] = (o_intra + _dot_nt(q_c[:, head(h)], states[h].astype(BF16)), st_add)

    def finish(gi, h):
        o, st_add = item.pop((gi, h))
        ms = jnp.mean(o * o, axis=-1, keepdims=True)
        write(gi, h, o * lax.rsqrt(ms + NORM_EPS) * g[:, head(h)])
        states[h] = states[h] * ch[gi]["d_all"][:, head(h)] + st_add

    chunk_stages = (gates, prefix, decays)
    item_stages = (level_scores, combine, outputs, finish)
    n_items = group * HG_HEADS
    for step in range(-len(chunk_stages), n_items + len(item_stages) - 1):
        for gi in range(group):
            for k, stage in enumerate(chunk_stages):
                if step == HG_HEADS * gi - len(chunk_stages) + k:
                    stage(gi)
        for k, stage in enumerate(item_stages):
            if 0 <= step - k < n_items:
                stage(*divmod(step - k, HG_HEADS))
    return states


def _hgrn_kernel(*refs, c, group, layer, has_s0, v_t, n_steps):
    if has_s0:
        (hq_ref, hf_ref, hi_ref, lbraw_ref, g_ref, pre_ref, lvl_ref, s0_ref,
         o_ref, sout_ref, st_scr) = refs
    else:
        (hq_ref, hf_ref, hi_ref, lbraw_ref, g_ref, pre_ref, lvl_ref,
         o_ref, sout_ref, st_scr) = refs
        s0_ref = None
    ci = pl.program_id(1)

    @pl.when(ci == 0)
    def _():
        for h in range(HG_HEADS):
            if has_s0:
                st_scr[h] = s0_ref[0, h].T
            else:
                st_scr[h] = jnp.zeros((HG_VAL_DIM, HG_KEY_DIM), F32)

    raw = lbraw_ref[...]
    e = jnp.exp(raw - jnp.max(raw, axis=0, keepdims=True))
    lb = jnp.sum(e[:layer + 1], axis=0, keepdims=True) / jnp.sum(e, axis=0, keepdims=True)

    pre = pre_ref[...]
    lvl = lvl_ref[...]
    g = g_ref[...]
    def read(gi):
        rs = slice(gi * c, (gi + 1) * c)
        return hq_ref[0, rs, :], hf_ref[0, rs, :], hi_ref[0, :, rs] if v_t else hi_ref[0, rs, :]

    def write(gi, h, o):
        o_ref[0, gi * c:(gi + 1) * c, HG_VAL_DIM * h:HG_VAL_DIM * (h + 1)] = o

    states = _hgrn_group(read, write, lb, g, pre, lvl, [st_scr[h] for h in range(HG_HEADS)],
                         c=c, group=group, v_t=v_t)
    for h in range(HG_HEADS):
        st_scr[h] = states[h]

    @pl.when(ci == n_steps - 1)
    def _():
        for h in range(HG_HEADS):
            sout_ref[0, h] = st_scr[h].T


def _hgrn(hq, hf, hi, lb_raw, norm_g, s0, *, layer, c, group, v_t):
    b, t, w = hq.shape
    span = c * group
    assert w == HG_WIDTH and t % span == 0 and c % HG_BLOCK == 0
    n_steps = t // span
    pre, level = _hgrn_consts(c)
    tok = pl.BlockSpec((1, span, w), lambda bi, ci: (bi, ci, 0))
    tok_t = pl.BlockSpec((1, w, span), lambda bi, ci: (bi, 0, ci))
    const2 = lambda shape: pl.BlockSpec(shape, lambda bi, ci: (0, 0))
    st_spec = pl.BlockSpec((1, HG_HEADS, HG_KEY_DIM, HG_VAL_DIM), lambda bi, ci: (bi, 0, 0, 0))
    in_specs = [tok, tok, tok_t if v_t else tok, const2(lb_raw.shape), const2((1, w)),
                const2(pre.shape), const2(level.shape)]
    args = [hq, hf, hi, lb_raw, norm_g.reshape(1, w), pre, level]
    if s0 is not None:
        in_specs.append(st_spec)
        args.append(s0)
    return pl.pallas_call(
        functools.partial(_hgrn_kernel, c=c, group=group, layer=layer, has_s0=s0 is not None, v_t=v_t,
                          n_steps=n_steps),
        grid=(b, n_steps),
        in_specs=in_specs,
        out_specs=(tok, st_spec),
        out_shape=(jax.ShapeDtypeStruct((b, t, w), F32),
                   jax.ShapeDtypeStruct((b, HG_HEADS, HG_KEY_DIM, HG_VAL_DIM), F32)),
        scratch_shapes=[pltpu.VMEM((HG_HEADS, HG_VAL_DIM, HG_KEY_DIM), F32)],
        compiler_params=pltpu.CompilerParams(
            dimension_semantics=("arbitrary", "arbitrary"), vmem_limit_bytes=V7X_VMEM_LIMIT),
        name="hgrn2",
    )(*args)


def _out_kernel(x_ref, osb_ref, ohg_ref, gpre_ref, wg_ref, bg_ref, wa_ref, wb_ref, wo_ref, gp_ref, y_ref,
                *, tm, tiles):
    d_model = x_ref.shape[-1]
    off = np.cumsum([0, SB_WIDTH, HG_WIDTH, d_model, d_model])
    bg = bg_ref[...]
    for t in range(tiles):
        rows = slice(t * tm, (t + 1) * tm)
        x = x_ref[rows, :]
        xn = _rms_norm_bf16(x, gpre_ref[...])
        z_a, hz, ga, gb = (_dot(xn, wg_ref[:, off[i]:off[i + 1]]) for i in range(4))
        u_a = _dot((osb_ref[rows, :] * jax.nn.silu(z_a)).astype(BF16), wa_ref[...])
        u_b = _dot((ohg_ref[rows, :] * jax.nn.silu(hz)).astype(BF16), wb_ref[...])
        merged = (jax.nn.sigmoid(ga + bg[:, :d_model]) * u_a + jax.nn.sigmoid(gb + bg[:, d_model:]) * u_b)
        m = _dot(merged.astype(BF16), wo_ref[...])
        ms = jnp.mean(m * m, axis=-1, keepdims=True)
        y_ref[rows, :] = x + m * lax.rsqrt(ms + NORM_EPS) * gp_ref[...]


def _out_stage(x2d, o_sb, o_hg, g_pre, w_in_bf16, b_gate, wa, wb, wo, g_post, *, tm, tiles):
    n, d = x2d.shape
    span = tm * tiles
    assert n % span == 0
    _, offsets = _in_widths(d)
    w_gate = jnp.concatenate([w_in_bf16[:, offsets[i]:offsets[i + 1]] for i in IN_GATE_GROUPS], axis=1)
    row = lambda w: pl.BlockSpec((span, w), lambda i: (i, 0))
    const = lambda shape: pl.BlockSpec(shape, lambda i: (0, 0))
    return pl.pallas_call(
        functools.partial(_out_kernel, tm=tm, tiles=tiles),
        grid=(n // span,),
        in_specs=[row(d), row(SB_WIDTH), row(HG_WIDTH), const((1, d)), const(w_gate.shape),
                  const((1, 2 * d)), const(wa.shape), const(wb.shape), const(wo.shape), const((1, d))],
        out_specs=row(d),
        out_shape=jax.ShapeDtypeStruct((n, d), F32),
        compiler_params=pltpu.CompilerParams(
            dimension_semantics=("arbitrary",), vmem_limit_bytes=V7X_VMEM_LIMIT),
        name="out_stage",
    )(x2d, o_sb, o_hg, g_pre.reshape(1, d), w_gate, b_gate.reshape(1, 2 * d), wa, wb, wo, g_post.reshape(1, d))


PROJ_TILE, IN_PROJ_TILES, OUT_TILES = 256, 4, 4
SB_BLOCK, SB_BLOCKS = 256, 8
HG_CHUNK, HG_CHUNKS = 128, 16


def _largest_tile(n, cap):
    t = cap
    while n % t:
        t //= 2
    return t


def _trunk_layer(x, cache_k, cache_v, s0, layer, g_pre, w_in, b_gate, lb_raw, hg_norm_g,
                 w_a, w_b, w_out, g_post):
    b, t, d = x.shape
    n = b * t
    heads = (SB_HEADS, SB_HEAD_DIM)
    if cache_k is None:
        tm = _largest_tile(t, PROJ_TILE)
        q, kt, vt, hq, hf, hi = _in_proj(x, g_pre, w_in, tm=tm, tiles=_largest_tile(t // tm, IN_PROJ_TILES),
                                         t_groups=(IN_K, IN_V, IN_HG_V))
        tq = _largest_tile(t, SB_BLOCK)
        o_sb = _sb_prompt(q, kt, vt, tq=tq, blocks=_largest_tile(t // tq, SB_BLOCKS))
        k_out, v_out = (a.reshape(b, *heads, t).transpose(0, 3, 1, 2) for a in (kt, vt))
    else:
        tm = _largest_tile(n, PROJ_TILE)
        outs = _in_proj(x.reshape(1, n, d), g_pre, w_in, tm=tm, tiles=_largest_tile(n // tm, IN_PROJ_TILES),
                        t_groups=())
        q, k, v, hq, hf, hi = (a.reshape(b, t, a.shape[-1]) for a in outs)
        past = cache_k.shape[1]
        cache_kt, cache_vt = (a.transpose(0, 2, 3, 1).reshape(b, SB_WIDTH, past) for a in (cache_k, cache_v))
        o_sb = _sb_cached(q, k, v, cache_kt, cache_vt, kn=_largest_tile(past, SB_BLOCK))
        k_out, v_out = k.reshape(b, t, *heads), v.reshape(b, t, *heads)
    c = _largest_tile(t, HG_CHUNK)
    o_hg, s_new = _hgrn(hq, hf, hi, lb_raw, hg_norm_g.reshape(-1), s0, layer=layer, c=c,
                        group=_largest_tile(t // c, HG_CHUNKS), v_t=cache_k is None)
    flat = lambda a: a.reshape(n, a.shape[-1])
    tm = _largest_tile(n, PROJ_TILE)
    y = _out_stage(flat(x), flat(o_sb), flat(o_hg), g_pre, w_in, b_gate, w_a, w_b, w_out, g_post,
                   tm=tm, tiles=_largest_tile(n // tm, OUT_TILES))
    return y.reshape(b, t, d), k_out, v_out, s_new


def kernel(x_prompt, x_sample, cache_sb_k, cache_sb_v, state_hgrn, g_pre, w_in, b_gate, hg_lb_raw,
           hg_norm_g, w_branch_a, w_branch_b, w_out, g_post):
    depth = w_in.shape[0]
    y_p, y_s = x_prompt, x_sample
    outs = [[] for _ in range(6)]
    for l in range(depth):
        weights = (g_pre[l], w_in[l].astype(BF16), b_gate[l], hg_lb_raw, hg_norm_g[l],
                   w_branch_a[l].astype(BF16), w_branch_b[l].astype(BF16), w_out[l].astype(BF16), g_post[l])
        y_p, kp, vp, sp = _trunk_layer(y_p, None, None, None, l, *weights)
        y_s, ks, vs, ss = _trunk_layer(y_s, cache_sb_k[l], cache_sb_v[l], state_hgrn[l], l, *weights)
        for lst, a in zip(outs, (kp, vp, sp, ks, vs, ss)):
            lst.append(a)
    return (y_p, y_s) + tuple(jnp.stack(lst) for lst in outs)
```

```python
import functools
import math

import numpy as np
import jax
import jax.numpy as jnp
from jax import lax
from jax.experimental import pallas as pl
from jax.experimental.pallas import tpu as pltpu

SB_HEADS = 8
SB_HEAD_DIM = 64
SB_WIDTH = SB_HEADS * SB_HEAD_DIM
HG_HEADS = 4
HG_KEY_DIM = 128
HG_VAL_DIM = 128
HG_WIDTH = HG_HEADS * HG_KEY_DIM
HG_BLOCK = 16
NORM_EPS = 1e-6

V7X_LANES = 128
V7X_VMEM_LIMIT = 56 * 1024 * 1024
SB_PAIRS = SB_WIDTH // V7X_LANES

SB_LOG2_UNDERFLOW = 150.5
SB_LATE_ROWS = 160
LOG2_E = math.log2(math.e)

F32 = jnp.float32
BF16 = jnp.bfloat16

_NT = (((1,), (1,)), ((), ()))
_TN = (((0,), (0,)), ((), ()))


def _dot(a, b):
    return jnp.dot(a, b, preferred_element_type=F32)


def _dot_nt(a, b):
    return lax.dot_general(a, b, _NT, preferred_element_type=F32)


def _dot_tn(a, b):
    return lax.dot_general(a, b, _TN, preferred_element_type=F32)


def _split_bf16(x):
    hi = x.astype(BF16)
    lo = (x - hi.astype(F32)).astype(BF16)
    return hi, lo


IN_Q, IN_K, IN_V, IN_ZA, IN_HG_Q, IN_HG_F, IN_HG_V, IN_HG_Z, IN_GATE_A, IN_GATE_B = range(10)
IN_MIXER_GROUPS = (IN_Q, IN_K, IN_V, IN_HG_Q, IN_HG_F, IN_HG_V)
IN_GATE_GROUPS = (IN_ZA, IN_HG_Z, IN_GATE_A, IN_GATE_B)


def _in_widths(d_model):
    widths = (SB_WIDTH,) * 4 + (HG_WIDTH,) * 4 + (d_model, d_model)
    return widths, np.concatenate([[0], np.cumsum(widths)])


def _rms_norm_bf16(x, g):
    ms = jnp.mean(x * x, axis=-1, keepdims=True)
    return (x * lax.rsqrt(ms + NORM_EPS) * g).astype(BF16)


def _in_proj_kernel(*refs, groups, t_groups, widths, columns, q_scale, tm, tiles):
    sb_refs = ()
    if t_groups:
        x_ref, g_ref, w_ref, wt_ref = refs[:4]
        out_refs, sb_refs = refs[4:-3], refs[-3:]
    else:
        x_ref, g_ref, w_ref = refs[:3]
        out_refs = refs[3:]
    for t in range(tiles):
        rows = slice(t * tm, (t + 1) * tm)
        xn = _rms_norm_bf16(x_ref[0, rows, :], g_ref[...])
        if t_groups:
            ht = _dot_nt(wt_ref[...], xn)
            off = 0
            for idx in t_groups:
                part = ht[off:off + widths[idx]]
                out_refs[groups.index(idx)][0, :, rows] = part
                off += widths[idx]
                if idx == IN_K:
                    sb_refs[0][0, t] = part.astype(BF16)
                elif idx == IN_V:
                    even = (lax.broadcasted_iota(jnp.int32, part.shape, 0) // SB_HEAD_DIM) % 2 == 0
                    sb_refs[1][0, t] = jnp.where(even, part, 0.0).astype(BF16)
                    sb_refs[2][0, t] = jnp.where(even, 0.0, part).astype(BF16)
        for ref, idx in zip(out_refs, groups):
            if idx not in t_groups:
                h = _dot(xn, w_ref[:, columns[idx][0]:columns[idx][1]])
                if idx == IN_Q:
                    h = h * q_scale
                ref[0, rows, :] = h.astype(ref.dtype)


def _in_proj(x, g_pre, w_in_bf16, *, tm, tiles, t_groups):
    b, t, d = x.shape
    widths, offsets = _in_widths(d)
    offsets = tuple(int(o) for o in offsets)
    span = tm * tiles
    assert offsets[-1] == w_in_bf16.shape[1] and t % span == 0
    out_shape, out_specs = [], []
    for idx in IN_MIXER_GROUPS:
        w = widths[idx]
        dt = BF16 if idx == IN_Q else F32
        if idx in t_groups:
            out_shape.append(jax.ShapeDtypeStruct((b, w, t), dt))
            out_specs.append(pl.BlockSpec((1, w, span), lambda bi, ti: (bi, 0, ti)))
        else:
            out_shape.append(jax.ShapeDtypeStruct((b, t, w), dt))
            out_specs.append(pl.BlockSpec((1, span, w), lambda bi, ti: (bi, ti, 0)))
    if t_groups:
        assert IN_K in t_groups and IN_V in t_groups
        for _ in range(3):
            out_shape.append(jax.ShapeDtypeStruct((b, t // tm, SB_WIDTH, tm), BF16))
            out_specs.append(pl.BlockSpec((1, tiles, SB_WIDTH, tm), lambda bi, ti: (bi, ti, 0, 0)))
    columns, parts, start = {}, [], 0
    for idx in IN_MIXER_GROUPS:
        if idx not in t_groups:
            parts.append(w_in_bf16[:, offsets[idx]:offsets[idx + 1]])
            columns[idx] = (start, start + widths[idx])
            start += widths[idx]
    w_used = jnp.concatenate(parts, axis=1)
    in_specs = [
        pl.BlockSpec((1, span, d), lambda bi, ti: (bi, ti, 0)),
        pl.BlockSpec((1, d), lambda bi, ti: (0, 0)),
        pl.BlockSpec(w_used.shape, lambda bi, ti: (0, 0)),
    ]
    args = [x, g_pre.reshape(1, d), w_used]
    if t_groups:
        w_t = jnp.concatenate([w_in_bf16[:, offsets[i]:offsets[i + 1]] for i in t_groups], axis=1).T
        in_specs.append(pl.BlockSpec(w_t.shape, lambda bi, ti: (0, 0)))
        args.append(w_t)
    return pl.pallas_call(
        functools.partial(_in_proj_kernel, groups=IN_MIXER_GROUPS, t_groups=t_groups, widths=widths,
                          columns=columns, q_scale=SB_HEAD_DIM ** -0.5 * LOG2_E, tm=tm, tiles=tiles),
        grid=(b, t // span),
        in_specs=in_specs,
        out_specs=tuple(out_specs),
        out_shape=tuple(out_shape),
        compiler_params=pltpu.CompilerParams(
            dimension_semantics=("arbitrary", "arbitrary"), vmem_limit_bytes=V7X_VMEM_LIMIT),
        name="in_proj",
    )(*args)


def _suffix_ones(n, copies):
    r = np.arange(n)
    u = (r[:, None] >= r[None, :]).astype(np.float32)
    return jnp.asarray(np.concatenate([u] * copies, axis=0), dtype=BF16)


def _sb_weights(z, u2, carry, mask):
    sp = jnp.maximum(z, 0.0) + jnp.log(1.0 + jnp.exp2(-jnp.abs(z))) * LOG2_E
    if mask is not None:
        sp = jnp.where(mask, sp, 0.0)
    hi, lo = _split_bf16(sp)
    tot = _dot(jnp.concatenate([hi, lo], axis=1), u2) + carry
    a = jnp.exp2(z - tot)
    if mask is not None:
        a = jnp.where(mask, a, 0.0)
    return a.astype(BF16), tot[:, 0:1]


def _head_split(x2, first):
    zero = jnp.zeros_like(x2)
    return jnp.where(first, x2, zero), jnp.where(first, zero, x2)


def _sb_prompt_kernel(q_ref, kt_ref, ve_ref, vo_ref, u_ref, o_ref, acc_scr, car_scr, *, tq, blocks):
    step = pl.program_id(1)

    late = min(SB_LATE_ROWS, tq - 16)

    def first(rows):
        return lax.broadcasted_iota(jnp.int32, (rows, V7X_LANES), 1) < SB_HEAD_DIM

    def sweep(jobs, row0):
        qs, z, sp16, tot, wts, pv = {}, {}, {}, {}, {}, {}
        masks = []
        for j, (n, (r0, r1), (k0, k1), offset, fresh) in enumerate(jobs):
            if offset is None:
                masks.append(None)
            else:
                r = lax.broadcasted_iota(jnp.int32, (r1 - r0, k1 - k0), 0)
                c = lax.broadcasted_iota(jnp.int32, (r1 - r0, k1 - k0), 1)
                masks.append(c < r + offset)
            for p in range(SB_PAIRS):
                q2 = q_ref[0, row0 + r0:row0 + r1, V7X_LANES * p:V7X_LANES * (p + 1)]
                qs[j, 2 * p], qs[j, 2 * p + 1] = _head_split(q2, first(r1 - r0))

        def pair_rows(h):
            return slice(V7X_LANES * (h // 2), V7X_LANES * (h // 2 + 1))

        def scores(j, h):
            n, _, (k0, k1), _, _ = jobs[j]
            z[j, h] = _dot(qs.pop((j, h)), kt_ref[0, n, pair_rows(h), k0:k1])

        def softplus(j, h):
            zz = z[j, h]
            sp = jnp.maximum(zz, 0.0) + jnp.log(1.0 + jnp.exp2(-jnp.abs(zz))) * LOG2_E
            if masks[j] is not None:
                sp = jnp.where(masks[j], sp, 0.0)
            sp16[j, h] = sp.astype(BF16)

        def suffix_sums(j, h):
            _, (r0, r1), (k0, k1), _, fresh = jobs[j]
            carry = jnp.zeros((r1 - r0, 1), F32) if fresh else car_scr[h, r0:r1]
            tot[j, h] = _dot(sp16.pop((j, h)), u_ref[:k1 - k0, :k1 - k0]) + carry

        def weights(j, h):
            a = jnp.exp2(z.pop((j, h)) - tot[j, h])
            if masks[j] is not None:
                a = jnp.where(masks[j], a, 0.0)
            wts[j, h] = a.astype(BF16)
            tot[j, h] = tot[j, h][:, 0:1]

        def weighted_values(j, h):
            n, _, (k0, k1), _, _ = jobs[j]
            v_ref = vo_ref if h % 2 else ve_ref
            pv[j, h] = _dot_nt(wts.pop((j, h)), v_ref[0, n, pair_rows(h), k0:k1])

        stages = (scores, softplus, suffix_sums, weights, weighted_values)
        n_items = len(jobs) * SB_HEADS
        for step in range(n_items + len(stages) - 1):
            for k, stage in enumerate(stages):
                if 0 <= step - k < n_items:
                    stage(*divmod(step - k, SB_HEADS))

        for j, (n, (r0, r1), _, _, fresh) in enumerate(jobs):
            for p in range(SB_PAIRS):
                new = pv[j, 2 * p] + pv[j, 2 * p + 1]
                acc_scr[p, r0:r1] = new if fresh else acc_scr[p, r0:r1] + new
            for h in range(SB_HEADS):
                car_scr[h, r0:r1] = tot[j, h]

    def lowest_total(r0, r1):
        low = car_scr[0, r0:r1]
        for h in range(1, SB_HEADS):
            low = jnp.minimum(low, car_scr[h, r0:r1])
        return jnp.min(low)

    def cond(st):
        n, low_early, low_late = st
        return jnp.logical_and(n >= 0, jnp.minimum(low_early, low_late) <= SB_LOG2_UNDERFLOW)

    for blk in range(blocks):
        i = step * blocks + blk
        row0 = blk * tq
        sweep([(i, (0, tq), (0, tq), 0, True)], row0)

        def older(n, r0, r1, low, row0=row0):
            def run():
                sweep([(n, (r0, r1), (0, tq), None, False)], row0)
                return lowest_total(r0, r1)
            return lax.cond(low <= SB_LOG2_UNDERFLOW, run, lambda: low)

        def body(st, older=older):
            n, low_early, low_late = st
            return n - 1, older(n, 0, late, low_early), older(n, late, tq, low_late)

        lax.while_loop(cond, body, (i - 1, lowest_total(0, late), lowest_total(late, tq)))
        for p in range(SB_PAIRS):
            o_ref[0, row0:row0 + tq, V7X_LANES * p:V7X_LANES * (p + 1)] = acc_scr[p]


def _sb_prompt(q, kt, ve, vo, *, tq, blocks):
    b, t, w = q.shape
    n_chunks = t // tq
    assert w == SB_WIDTH and t % (tq * blocks) == 0 and kt.shape == (b, n_chunks, w, tq)
    kv_spec = pl.BlockSpec((1, n_chunks, w, tq), lambda bi, i: (bi, 0, 0, 0))
    q_spec = pl.BlockSpec((1, tq * blocks, w), lambda bi, i: (bi, i, 0))
    return pl.pallas_call(
        functools.partial(_sb_prompt_kernel, tq=tq, blocks=blocks),
        grid=(b, n_chunks // blocks),
        in_specs=[q_spec, kv_spec, kv_spec, kv_spec, pl.BlockSpec((tq, tq), lambda bi, i: (0, 0))],
        out_specs=q_spec,
        out_shape=jax.ShapeDtypeStruct((b, t, w), F32),
        scratch_shapes=[pltpu.VMEM((SB_PAIRS, tq, V7X_LANES), F32), pltpu.VMEM((SB_HEADS, tq, 1), F32)],
        compiler_params=pltpu.CompilerParams(
            dimension_semantics=("arbitrary", "arbitrary"), vmem_limit_bytes=V7X_VMEM_LIMIT),
        name="sb_prompt",
    )(q, kt, ve, vo, _suffix_ones(tq, 1))


def _sb_cached_kernel(q_ref, k_ref, v_ref, ckt_ref, cvt_ref, ud_ref, uc_ref, o_ref, acc_scr, car_scr,
                      *, tq, kn, n_chunks):
    first = lax.broadcasted_iota(jnp.int32, (tq, V7X_LANES), 1) < SB_HEAD_DIM
    r = lax.broadcasted_iota(jnp.int32, (tq, tq), 0)
    c = lax.broadcasted_iota(jnp.int32, (tq, tq), 1)
    causal = c < r
    even = (lax.broadcasted_iota(jnp.int32, (V7X_LANES, kn), 0) // SB_HEAD_DIM) % 2 == 0
    qs = []
    for p in range(SB_PAIRS):
        qs.extend(_head_split(q_ref[0, :, V7X_LANES * p:V7X_LANES * (p + 1)], first))

    def pair_rows(h):
        return slice(V7X_LANES * (h // 2), V7X_LANES * (h // 2 + 1))

    def sweep(n):
        z, wts, tot, pv = {}, {}, {}, {}

        def scores(h):
            if n is None:
                z[h] = _dot_nt(qs[h], k_ref[0, :, pair_rows(h)].astype(BF16))
            else:
                z[h] = _dot(qs[h], ckt_ref[0, pair_rows(h), n * kn:(n + 1) * kn].astype(BF16))

        def weights(h):
            if n is None:
                wts[h], tot[h] = _sb_weights(z.pop(h), ud_ref[...], jnp.zeros((tq, 1), F32), causal)
            else:
                wts[h], tot[h] = _sb_weights(z.pop(h), uc_ref[...], car_scr[h], None)

        def weighted_values(h):
            if n is None:
                v = _head_split(v_ref[0, :, pair_rows(h)], first)[h % 2]
                pv[h] = _dot(wts.pop(h), v.astype(BF16))
            else:
                v = cvt_ref[0, pair_rows(h), n * kn:(n + 1) * kn]
                v = jnp.where(even, 0.0, v) if h % 2 else jnp.where(even, v, 0.0)
                pv[h] = _dot_nt(wts.pop(h), v.astype(BF16))

        stages = (scores, weights, weighted_values)
        for step in range(SB_HEADS + len(stages) - 1):
            for k, stage in enumerate(stages):
                if 0 <= step - k < SB_HEADS:
                    stage(step - k)
        low = None
        for p in range(SB_PAIRS):
            new = pv[2 * p] + pv[2 * p + 1]
            acc_scr[p] = new if n is None else acc_scr[p] + new
        for h in range(SB_HEADS):
            car_scr[h] = tot[h]
            low = tot[h] if low is None else jnp.minimum(low, tot[h])
        return jnp.min(low)

    lowest = sweep(None)
    for n in range(n_chunks - 1, -1, -1):
        lowest = lax.cond(lowest <= SB_LOG2_UNDERFLOW, functools.partial(sweep, n), lambda low=lowest: low)
    for p in range(SB_PAIRS):
        o_ref[0, :, V7X_LANES * p:V7X_LANES * (p + 1)] = acc_scr[p]


def _sb_cached(q, k, v, cache_kt, cache_vt, *, kn):
    b, tq, w = q.shape
    past = cache_kt.shape[2]
    assert w == SB_WIDTH and past % kn == 0 and cache_kt.shape == (b, w, past)
    new_spec = pl.BlockSpec((1, tq, w), lambda bi: (bi, 0, 0))
    cache_spec = pl.BlockSpec((1, w, past), lambda bi: (bi, 0, 0))
    return pl.pallas_call(
        functools.partial(_sb_cached_kernel, tq=tq, kn=kn, n_chunks=past // kn),
        grid=(b,),
        in_specs=[new_spec, new_spec, new_spec, cache_spec, cache_spec,
                  pl.BlockSpec((2 * tq, tq), lambda bi: (0, 0)),
                  pl.BlockSpec((2 * kn, kn), lambda bi: (0, 0))],
        out_specs=new_spec,
        out_shape=jax.ShapeDtypeStruct((b, tq, w), F32),
        scratch_shapes=[pltpu.VMEM((SB_PAIRS, tq, V7X_LANES), F32), pltpu.VMEM((SB_HEADS, tq, 1), F32)],
        compiler_params=pltpu.CompilerParams(
            dimension_semantics=("arbitrary",), vmem_limit_bytes=V7X_VMEM_LIMIT),
        name="sb_cached",
    )(q, k, v, cache_kt, cache_vt, _suffix_ones(tq, 2), _suffix_ones(kn, 2))


def _hgrn_consts(c):
    t = np.arange(c)
    sub = t // HG_BLOCK
    same = sub[:, None] == sub[None, :]
    tri = (same & (t[None, :] <= t[:, None])).astype(np.float32)
    ones = same.astype(np.float32)
    x = sub[:, None] ^ sub[None, :]
    level = np.where(x == 0, 0, np.floor(np.log2(np.maximum(x, 1))).astype(np.int64) + 1)
    level = np.where(t[None, :] <= t[:, None], level, -1).astype(np.int32)
    pre = np.block([[tri, tri], [ones, ones]])
    return jnp.asarray(pre, dtype=BF16), jnp.asarray(level)


def _prod(vecs):
    out = None
    for v in vecs:
        out = v if out is None else out * v
    return out


def _scale_rows(x, vecs):
    blocks = [x[HG_BLOCK * i:HG_BLOCK * (i + 1)] for i in range(len(vecs))]
    return jnp.concatenate([b if v is None else b * v for b, v in zip(blocks, vecs)], axis=0)


def _hgrn_group(read, write, lb, g, pre, lvl, states, *, c, group, v_t):
    nsub = c // HG_BLOCK
    levels = int(round(math.log2(nsub)))
    states = list(states)
    level_masks = [lvl == l for l in range(levels + 1)]
    ch = [dict() for _ in range(group)]
    item = {}

    def head(h):
        return slice(HG_KEY_DIM * h, HG_KEY_DIM * (h + 1))

    def gates(gi):
        hq, hf, v = read(gi)
        f = lb + (1.0 - lb) * jax.nn.sigmoid(hf)
        ch[gi].update(hq=hq, kb=1.0 - f,
                      v_bf=v.astype(BF16), split=jnp.concatenate(_split_bf16(jnp.log(f)), axis=0))

    def prefix(gi):
        ch[gi]["sums"] = _dot(pre, ch[gi].pop("split"))

    def decays(gi):
        sums, hq, kb = (ch[gi].pop(k) for k in ("sums", "hq", "kb"))
        b16 = sums[:c]
        bend = sums[c:]
        q_dec = hq * jnp.exp(b16)
        dmat = jnp.exp(bend)
        k_inv = kb * jnp.exp(-b16)
        k_end = k_inv * dmat
        d = [dmat[HG_BLOCK * i:HG_BLOCK * i + 1, :] for i in range(nsub)]
        q_lv = [q_dec.astype(BF16)]
        k_lv = [k_inv.astype(BF16)]
        if levels >= 1:
            q_lv.append(q_lv[0])
            k_lv.append(k_end.astype(BF16))
        for l in range(2, levels + 1):
            half = 2 ** (l - 1)
            alpha, beta = [], []
            for i in range(nsub):
                mid = (i // (2 * half)) * 2 * half + half
                alpha.append(_prod(d[mid:i]) if i > mid else None)
                beta.append(_prod(d[i + 1:mid]) if i + 1 < mid else None)
            q_lv.append(_scale_rows(q_dec, alpha).astype(BF16))
            k_lv.append(_scale_rows(k_end, beta).astype(BF16))
        if nsub > 1:
            alpha_c = [_prod(d[:i]) if i > 0 else None for i in range(nsub)]
            beta_c = [_prod(d[i + 1:]) if i + 1 < nsub else None for i in range(nsub)]
            q_c = _scale_rows(q_dec, alpha_c).astype(BF16)
            k_c = _scale_rows(k_end, beta_c).astype(BF16)
        else:
            q_c = q_lv[0]
            k_c = k_end.astype(BF16)
        ch[gi].update(q_lv=q_lv, k_lv=k_lv, q_c=q_c, k_c=k_c, d_all=_prod(d))

    def level_scores(gi, h):
        q_lv, k_lv = ch[gi]["q_lv"], ch[gi]["k_lv"]
        item[gi, h] = [_dot_nt(q_lv[l][:, head(h)], k_lv[l][:, head(h)]) for l in range(levels + 1)]

    def combine(gi, h):
        att = jnp.zeros((c, c), F32)
        for l, scores in enumerate(item[gi, h]):
            att = jnp.where(level_masks[l], scores, att)
        item[gi, h] = att.astype(BF16)

    def outputs(gi, h):
        att = item[gi, h]
        v_bf, q_c, k_c = ch[gi]["v_bf"], ch[gi]["q_c"], ch[gi]["k_c"]
        if v_t:
            vt_h = v_bf[head(h), :]
            o_intra = _dot_nt(att, vt_h)
            st_add = _dot(vt_h, k_c[:, head(h)])
        else:
            o_intra = _dot(att, v_bf[:, head(h)])
            st_add = _dot_tn(v_bf[:, head(h)], k_c[:, head(h)])
        item[gi, h] = (o_intra + _dot_nt(q_c[:, head(h)], states[h].astype(BF16)), st_add)

    def finish(gi, h):
        o, st_add = item.pop((gi, h))
        ms = jnp.mean(o * o, axis=-1, keepdims=True)
        write(gi, h, o * lax.rsqrt(ms + NORM_EPS) * g[:, head(h)])
        states[h] = states[h] * ch[gi]["d_all"][:, head(h)] + st_add

    chunk_stages = (gates, prefix, decays)
    item_stages = (level_scores, combine, outputs, finish)
    n_items = group * HG_HEADS
    for step in range(-len(chunk_stages), n_items + len(item_stages) - 1):
        for gi in range(group):
            for k, stage in enumerate(chunk_stages):
                if step == HG_HEADS * gi - len(chunk_stages) + k:
                    stage(gi)
        for k, stage in enumerate(item_stages):
            if 0 <= step - k < n_items:
                stage(*divmod(step - k, HG_HEADS))
    return states


def _hgrn_kernel(*refs, c, group, layer, has_s0, v_t, n_steps):
    if has_s0:
        (hq_ref, hf_ref, hi_ref, lbraw_ref, g_ref, pre_ref, lvl_ref, s0_ref,
         o_ref, sout_ref, st_scr) = refs
    else:
        (hq_ref, hf_ref, hi_ref, lbraw_ref, g_ref, pre_ref, lvl_ref,
         o_ref, sout_ref, st_scr) = refs
        s0_ref = None
    ci = pl.program_id(1)

    @pl.when(ci == 0)
    def _():
        for h in range(HG_HEADS):
            if has_s0:
                st_scr[h] = s0_ref[0, h].T
            else:
                st_scr[h] = jnp.zeros((HG_VAL_DIM, HG_KEY_DIM), F32)

    raw = lbraw_ref[...]
    e = jnp.exp(raw - jnp.max(raw, axis=0, keepdims=True))
    lb = jnp.sum(e[:layer + 1], axis=0, keepdims=True) / jnp.sum(e, axis=0, keepdims=True)

    pre = pre_ref[...]
    lvl = lvl_ref[...]
    g = g_ref[...]
    def read(gi):
        rs = slice(gi * c, (gi + 1) * c)
        return hq_ref[0, rs, :], hf_ref[0, rs, :], hi_ref[0, :, rs] if v_t else hi_ref[0, rs, :]

    def write(gi, h, o):
        o_ref[0, gi * c:(gi + 1) * c, HG_VAL_DIM * h:HG_VAL_DIM * (h + 1)] = o

    states = _hgrn_group(read, write, lb, g, pre, lvl, [st_scr[h] for h in range(HG_HEADS)],
                         c=c, group=group, v_t=v_t)
    for h in range(HG_HEADS):
        st_scr[h] = states[h]

    @pl.when(ci == n_steps - 1)
    def _():
        for h in range(HG_HEADS):
            sout_ref[0, h] = st_scr[h].T


def _hgrn(hq, hf, hi, lb_raw, norm_g, s0, *, layer, c, group, v_t):
    b, t, w = hq.shape
    span = c * group
    assert w == HG_WIDTH and t % span == 0 and c % HG_BLOCK == 0
    n_steps = t // span
    pre, level = _hgrn_consts(c)
    tok = pl.BlockSpec((1, span, w), lambda bi, ci: (bi, ci, 0))
    tok_t = pl.BlockSpec((1, w, span), lambda bi, ci: (bi, 0, ci))
    const2 = lambda shape: pl.BlockSpec(shape, lambda bi, ci: (0, 0))
    st_spec = pl.BlockSpec((1, HG_HEADS, HG_KEY_DIM, HG_VAL_DIM), lambda bi, ci: (bi, 0, 0, 0))
    in_specs = [tok, tok, tok_t if v_t else tok, const2(lb_raw.shape), const2((1, w)),
                const2(pre.shape), const2(level.shape)]
    args = [hq, hf, hi, lb_raw, norm_g.reshape(1, w), pre, level]
    if s0 is not None:
        in_specs.append(st_spec)
        args.append(s0)
    return pl.pallas_call(
        functools.partial(_hgrn_kernel, c=c, group=group, layer=layer, has_s0=s0 is not None, v_t=v_t,
                          n_steps=n_steps),
        grid=(b, n_steps),
        in_specs=in_specs,
        out_specs=(tok, st_spec),
        out_shape=(jax.ShapeDtypeStruct((b, t, w), F32),
                   jax.ShapeDtypeStruct((b, HG_HEADS, HG_KEY_DIM, HG_VAL_DIM), F32)),
        scratch_shapes=[pltpu.VMEM((HG_HEADS, HG_VAL_DIM, HG_KEY_DIM), F32)],
        compiler_params=pltpu.CompilerParams(
            dimension_semantics=("arbitrary", "arbitrary"), vmem_limit_bytes=V7X_VMEM_LIMIT),
        name="hgrn2",
    )(*args)


def _out_kernel(x_ref, osb_ref, ohg_ref, gpre_ref, wg_ref, bg_ref, wa_ref, wb_ref, wo_ref, gp_ref, y_ref,
                *, tm, tiles):
    d_model = x_ref.shape[-1]
    off = np.cumsum([0, SB_WIDTH, HG_WIDTH, d_model, d_model])
    bg = bg_ref[...]
    for t in range(tiles):
        rows = slice(t * tm, (t + 1) * tm)
        x = x_ref[rows, :]
        xn = _rms_norm_bf16(x, gpre_ref[...])
        z_a, hz, ga, gb = (_dot(xn, wg_ref[:, off[i]:off[i + 1]]) for i in range(4))
        u_a = _dot((osb_ref[rows, :] * jax.nn.silu(z_a)).astype(BF16), wa_ref[...])
        u_b = _dot((ohg_ref[rows, :] * jax.nn.silu(hz)).astype(BF16), wb_ref[...])
        merged = (jax.nn.sigmoid(ga + bg[:, :d_model]) * u_a + jax.nn.sigmoid(gb + bg[:, d_model:]) * u_b)
        m = _dot(merged.astype(BF16), wo_ref[...])
        ms = jnp.mean(m * m, axis=-1, keepdims=True)
        y_ref[rows, :] = x + m * lax.rsqrt(ms + NORM_EPS) * gp_ref[...]


def _out_stage(x2d, o_sb, o_hg, g_pre, w_in_bf16, b_gate, wa, wb, wo, g_post, *, tm, tiles):
    n, d = x2d.shape
    span = tm * tiles
    assert n % span == 0
    _, offsets = _in_widths(d)
    w_gate = jnp.concatenate([w_in_bf16[:, offsets[i]:offsets[i + 1]] for i in IN_GATE_GROUPS], axis=1)
    row = lambda w: pl.BlockSpec((span, w), lambda i: (i, 0))
    const = lambda shape: pl.BlockSpec(shape, lambda i: (0, 0))
    return pl.pallas_call(
        functools.partial(_out_kernel, tm=tm, tiles=tiles),
        grid=(n // span,),
        in_specs=[row(d), row(SB_WIDTH), row(HG_WIDTH), const((1, d)), const(w_gate.shape),
                  const((1, 2 * d)), const(wa.shape), const(wb.shape), const(wo.shape), const((1, d))],
        out_specs=row(d),
        out_shape=jax.ShapeDtypeStruct((n, d), F32),
        compiler_params=pltpu.CompilerParams(
            dimension_semantics=("arbitrary",), vmem_limit_bytes=V7X_VMEM_LIMIT),
        name="out_stage",
    )(x2d, o_sb, o_hg, g_pre.reshape(1, d), w_gate, b_gate.reshape(1, 2 * d), wa, wb, wo, g_post.reshape(1, d))


PROJ_TILE, IN_PROJ_TILES, OUT_TILES = 256, 4, 4
SB_BLOCK, SB_BLOCKS = 256, 8
HG_CHUNK, HG_CHUNKS = 128, 16


def _largest_tile(n, cap):
    t = cap
    while n % t:
        t //= 2
    return t


def _trunk_layer(x, cache_k, cache_v, s0, layer, g_pre, w_in, b_gate, lb_raw, hg_norm_g,
                 w_a, w_b, w_out, g_post):
    b, t, d = x.shape
    n = b * t
    heads = (SB_HEADS, SB_HEAD_DIM)
    if cache_k is None:
        tm = _largest_tile(t, PROJ_TILE)
        q, kt, vt, hq, hf, hi, kt16, ve16, vo16 = _in_proj(
            x, g_pre, w_in, tm=tm, tiles=_largest_tile(t // tm, IN_PROJ_TILES), t_groups=(IN_K, IN_V, IN_HG_V))
        o_sb = _sb_prompt(q, kt16, ve16, vo16, tq=tm, blocks=_largest_tile(t // tm, SB_BLOCKS))
        k_out, v_out = (a.reshape(b, *heads, t).transpose(0, 3, 1, 2) for a in (kt, vt))
    else:
        tm = _largest_tile(n, PROJ_TILE)
        outs = _in_proj(x.reshape(1, n, d), g_pre, w_in, tm=tm, tiles=_largest_tile(n // tm, IN_PROJ_TILES),
                        t_groups=())
        q, k, v, hq, hf, hi = (a.reshape(b, t, a.shape[-1]) for a in outs)
        past = cache_k.shape[1]
        cache_kt, cache_vt = (a.transpose(0, 2, 3, 1).reshape(b, SB_WIDTH, past) for a in (cache_k, cache_v))
        o_sb = _sb_cached(q, k, v, cache_kt, cache_vt, kn=_largest_tile(past, SB_BLOCK))
        k_out, v_out = k.reshape(b, t, *heads), v.reshape(b, t, *heads)
    c = _largest_tile(t, HG_CHUNK)
    o_hg, s_new = _hgrn(hq, hf, hi, lb_raw, hg_norm_g.reshape(-1), s0, layer=layer, c=c,
                        group=_largest_tile(t // c, HG_CHUNKS), v_t=cache_k is None)
    flat = lambda a: a.reshape(n, a.shape[-1])
    tm = _largest_tile(n, PROJ_TILE)
    y = _out_stage(flat(x), flat(o_sb), flat(o_hg), g_pre, w_in, b_gate, w_a, w_b, w_out, g_post,
                   tm=tm, tiles=_largest_tile(n // tm, OUT_TILES))
    return y.reshape(b, t, d), k_out, v_out, s_new


def kernel(x_prompt, x_sample, cache_sb_k, cache_sb_v, state_hgrn, g_pre, w_in, b_gate, hg_lb_raw,
           hg_norm_g, w_branch_a, w_branch_b, w_out, g_post):
    depth = w_in.shape[0]
    y_p, y_s = x_prompt, x_sample
    outs = [[] for _ in range(6)]
    for l in range(depth):
        weights = (g_pre[l], w_in[l].astype(BF16), b_gate[l], hg_lb_raw, hg_norm_g[l],
                   w_branch_a[l].astype(BF16), w_branch_b[l].astype(BF16), w_out[l].astype(BF16), g_post[l])
        y_p, kp, vp, sp = _trunk_layer(y_p, None, None, None, l, *weights)
        y_s, ks, vs, ss = _trunk_layer(y_s, cache_sb_k[l], cache_sb_v[l], state_hgrn[l], l, *weights)
        for lst, a in zip(outs, (kp, vp, sp, ks, vs, ss)):
            lst.append(a)
    return (y_p, y_s) + tuple(jnp.stack(lst) for lst in outs)
```

```python
import functools
import math

import numpy as np
import jax
import jax.numpy as jnp
from jax import lax
from jax.experimental import pallas as pl
from jax.experimental.pallas import tpu as pltpu

SB_HEADS = 8
SB_HEAD_DIM = 64
SB_WIDTH = SB_HEADS * SB_HEAD_DIM
HG_HEADS = 4
HG_KEY_DIM = 128
HG_VAL_DIM = 128
HG_WIDTH = HG_HEADS * HG_KEY_DIM
HG_BLOCK = 16
NORM_EPS = 1e-6

V7X_LANES = 128
V7X_VMEM_LIMIT = 56 * 1024 * 1024
SB_PAIRS = SB_WIDTH // V7X_LANES

SB_LOG2_UNDERFLOW = 150.5
SB_LATE_ROWS = 160
LOG2_E = math.log2(math.e)

F32 = jnp.float32
BF16 = jnp.bfloat16

_NT = (((1,), (1,)), ((), ()))
_TN = (((0,), (0,)), ((), ()))


def _dot(a, b):
    return jnp.dot(a, b, preferred_element_type=F32)


def _dot_nt(a, b):
    return lax.dot_general(a, b, _NT, preferred_element_type=F32)


def _dot_tn(a, b):
    return lax.dot_general(a, b, _TN, preferred_element_type=F32)


def _split_bf16(x):
    hi = x.astype(BF16)
    lo = (x - hi.astype(F32)).astype(BF16)
    return hi, lo


IN_Q, IN_K, IN_V, IN_ZA, IN_HG_Q, IN_HG_F, IN_HG_V, IN_HG_Z, IN_GATE_A, IN_GATE_B = range(10)
IN_MIXER_GROUPS = (IN_Q, IN_K, IN_V, IN_HG_Q, IN_HG_F, IN_HG_V)
IN_GATE_GROUPS = (IN_ZA, IN_HG_Z, IN_GATE_A, IN_GATE_B)


def _in_widths(d_model):
    widths = (SB_WIDTH,) * 4 + (HG_WIDTH,) * 4 + (d_model, d_model)
    return widths, np.concatenate([[0], np.cumsum(widths)])


def _rms_norm_bf16(x, g):
    ms = jnp.mean(x * x, axis=-1, keepdims=True)
    return (x * lax.rsqrt(ms + NORM_EPS) * g).astype(BF16)


def _in_proj_kernel(*refs, groups, t_groups, widths, columns, q_scale, tm, tiles):
    sb_refs = ()
    if t_groups:
        x_ref, g_ref, w_ref, wt_ref = refs[:4]
        out_refs, sb_refs = refs[4:-3], refs[-3:]
    else:
        x_ref, g_ref, w_ref = refs[:3]
        out_refs = refs[3:]
    for t in range(tiles):
        rows = slice(t * tm, (t + 1) * tm)
        xn = _rms_norm_bf16(x_ref[0, rows, :], g_ref[...])
        if t_groups:
            ht = _dot_nt(wt_ref[...], xn)
            off = 0
            for idx in t_groups:
                part = ht[off:off + widths[idx]]
                ref = out_refs[groups.index(idx)]
                ref[0, :, rows] = part.astype(ref.dtype)
                off += widths[idx]
                if idx == IN_K:
                    sb_refs[0][0, t] = part.astype(BF16)
                elif idx == IN_V:
                    even = (lax.broadcasted_iota(jnp.int32, part.shape, 0) // SB_HEAD_DIM) % 2 == 0
                    sb_refs[1][0, t] = jnp.where(even, part, 0.0).astype(BF16)
                    sb_refs[2][0, t] = jnp.where(even, 0.0, part).astype(BF16)
        for ref, idx in zip(out_refs, groups):
            if idx not in t_groups:
                h = _dot(xn, w_ref[:, columns[idx][0]:columns[idx][1]])
                if idx == IN_Q:
                    h = h * q_scale
                ref[0, rows, :] = h.astype(ref.dtype)


def _in_proj(x, g_pre, w_in_bf16, *, tm, tiles, t_groups):
    b, t, d = x.shape
    widths, offsets = _in_widths(d)
    offsets = tuple(int(o) for o in offsets)
    span = tm * tiles
    assert offsets[-1] == w_in_bf16.shape[1] and t % span == 0
    out_shape, out_specs = [], []
    for idx in IN_MIXER_GROUPS:
        w = widths[idx]
        dt = BF16 if idx == IN_Q or (idx == IN_HG_V and idx in t_groups) else F32
        if idx in t_groups:
            out_shape.append(jax.ShapeDtypeStruct((b, w, t), dt))
            out_specs.append(pl.BlockSpec((1, w, span), lambda bi, ti: (bi, 0, ti)))
        else:
            out_shape.append(jax.ShapeDtypeStruct((b, t, w), dt))
            out_specs.append(pl.BlockSpec((1, span, w), lambda bi, ti: (bi, ti, 0)))
    if t_groups:
        assert IN_K in t_groups and IN_V in t_groups
        for _ in range(3):
            out_shape.append(jax.ShapeDtypeStruct((b, t // tm, SB_WIDTH, tm), BF16))
            out_specs.append(pl.BlockSpec((1, tiles, SB_WIDTH, tm), lambda bi, ti: (bi, ti, 0, 0)))
    columns, parts, start = {}, [], 0
    for idx in IN_MIXER_GROUPS:
        if idx not in t_groups:
            parts.append(w_in_bf16[:, offsets[idx]:offsets[idx + 1]])
            columns[idx] = (start, start + widths[idx])
            start += widths[idx]
    w_used = jnp.concatenate(parts, axis=1)
    in_specs = [
        pl.BlockSpec((1, span, d), lambda bi, ti: (bi, ti, 0)),
        pl.BlockSpec((1, d), lambda bi, ti: (0, 0)),
        pl.BlockSpec(w_used.shape, lambda bi, ti: (0, 0)),
    ]
    args = [x, g_pre.reshape(1, d), w_used]
    if t_groups:
        w_t = jnp.concatenate([w_in_bf16[:, offsets[i]:offsets[i + 1]] for i in t_groups], axis=1).T
        in_specs.append(pl.BlockSpec(w_t.shape, lambda bi, ti: (0, 0)))
        args.append(w_t)
    return pl.pallas_call(
        functools.partial(_in_proj_kernel, groups=IN_MIXER_GROUPS, t_groups=t_groups, widths=widths,
                          columns=columns, q_scale=SB_HEAD_DIM ** -0.5 * LOG2_E, tm=tm, tiles=tiles),
        grid=(b, t // span),
        in_specs=in_specs,
        out_specs=tuple(out_specs),
        out_shape=tuple(out_shape),
        compiler_params=pltpu.CompilerParams(
            dimension_semantics=("arbitrary", "arbitrary"), vmem_limit_bytes=V7X_VMEM_LIMIT),
        name="in_proj",
    )(*args)


def _suffix_ones(n, copies):
    r = np.arange(n)
    u = (r[:, None] >= r[None, :]).astype(np.float32)
    return jnp.asarray(np.concatenate([u] * copies, axis=0), dtype=BF16)


def _sb_weights(z, u2, carry, mask):
    sp = jnp.maximum(z, 0.0) + jnp.log(1.0 + jnp.exp2(-jnp.abs(z))) * LOG2_E
    if mask is not None:
        sp = jnp.where(mask, sp, 0.0)
    hi, lo = _split_bf16(sp)
    tot = _dot(jnp.concatenate([hi, lo], axis=1), u2) + carry
    a = jnp.exp2(z - tot)
    if mask is not None:
        a = jnp.where(mask, a, 0.0)
    return a.astype(BF16), tot[:, 0:1]


def _head_split(x2, first):
    zero = jnp.zeros_like(x2)
    return jnp.where(first, x2, zero), jnp.where(first, zero, x2)


def _sb_prompt_kernel(q_ref, kt_ref, ve_ref, vo_ref, u_ref, o_ref, acc_scr, car_scr, *, tq, blocks):
    step = pl.program_id(1)

    late = min(SB_LATE_ROWS, tq - 16)

    def first(rows):
        return lax.broadcasted_iota(jnp.int32, (rows, V7X_LANES), 1) < SB_HEAD_DIM

    def sweep(jobs, row0):
        qs, z, sp16, tot, wts, pv = {}, {}, {}, {}, {}, {}
        masks = []
        for j, (n, (r0, r1), (k0, k1), offset, fresh) in enumerate(jobs):
            if offset is None:
                masks.append(None)
            else:
                r = lax.broadcasted_iota(jnp.int32, (r1 - r0, k1 - k0), 0)
                c = lax.broadcasted_iota(jnp.int32, (r1 - r0, k1 - k0), 1)
                masks.append(c < r + offset)
            for p in range(SB_PAIRS):
                q2 = q_ref[0, row0 + r0:row0 + r1, V7X_LANES * p:V7X_LANES * (p + 1)]
                qs[j, 2 * p], qs[j, 2 * p + 1] = _head_split(q2, first(r1 - r0))

        def pair_rows(h):
            return slice(V7X_LANES * (h // 2), V7X_LANES * (h // 2 + 1))

        def scores(j, h):
            n, _, (k0, k1), _, _ = jobs[j]
            z[j, h] = _dot(qs.pop((j, h)), kt_ref[0, n, pair_rows(h), k0:k1])

        def softplus(j, h):
            zz = z[j, h]
            sp = jnp.maximum(zz, 0.0) + jnp.log(1.0 + jnp.exp2(-jnp.abs(zz))) * LOG2_E
            if masks[j] is not None:
                sp = jnp.where(masks[j], sp, 0.0)
            sp16[j, h] = sp.astype(BF16)

        def suffix_sums(j, h):
            _, (r0, r1), (k0, k1), _, fresh = jobs[j]
            carry = jnp.zeros((r1 - r0, 1), F32) if fresh else car_scr[h, r0:r1]
            tot[j, h] = _dot(sp16.pop((j, h)), u_ref[:k1 - k0, :k1 - k0]) + carry

        def weights(j, h):
            a = jnp.exp2(z.pop((j, h)) - tot[j, h])
            if masks[j] is not None:
                a = jnp.where(masks[j], a, 0.0)
            wts[j, h] = a.astype(BF16)
            tot[j, h] = tot[j, h][:, 0:1]

        def weighted_values(j, h):
            n, _, (k0, k1), _, _ = jobs[j]
            v_ref = vo_ref if h % 2 else ve_ref
            pv[j, h] = _dot_nt(wts.pop((j, h)), v_ref[0, n, pair_rows(h), k0:k1])

        stages = (scores, softplus, suffix_sums, weights, weighted_values)
        n_items = len(jobs) * SB_HEADS
        for step in range(n_items + len(stages) - 1):
            for k, stage in enumerate(stages):
                if 0 <= step - k < n_items:
                    stage(*divmod(step - k, SB_HEADS))

        for j, (n, (r0, r1), _, _, fresh) in enumerate(jobs):
            for p in range(SB_PAIRS):
                new = pv[j, 2 * p] + pv[j, 2 * p + 1]
                acc_scr[p, r0:r1] = new if fresh else acc_scr[p, r0:r1] + new
            for h in range(SB_HEADS):
                car_scr[h, r0:r1] = tot[j, h]

    def lowest_total(r0, r1):
        low = car_scr[0, r0:r1]
        for h in range(1, SB_HEADS):
            low = jnp.minimum(low, car_scr[h, r0:r1])
        return jnp.min(low)

    def cond(st):
        n, low_early, low_late = st
        return jnp.logical_and(n >= 0, jnp.minimum(low_early, low_late) <= SB_LOG2_UNDERFLOW)

    for blk in range(blocks):
        i = step * blocks + blk
        row0 = blk * tq
        sweep([(i, (0, tq), (0, tq), 0, True)], row0)

        def older(n, r0, r1, low, row0=row0):
            def run():
                sweep([(n, (r0, r1), (0, tq), None, False)], row0)
                return lowest_total(r0, r1)
            return lax.cond(low <= SB_LOG2_UNDERFLOW, run, lambda: low)

        def body(st, older=older):
            n, low_early, low_late = st
            return n - 1, older(n, 0, late, low_early), older(n, late, tq, low_late)

        lax.while_loop(cond, body, (i - 1, lowest_total(0, late), lowest_total(late, tq)))
        for p in range(SB_PAIRS):
            o_ref[0, row0:row0 + tq, V7X_LANES * p:V7X_LANES * (p + 1)] = acc_scr[p]


def _sb_prompt(q, kt, ve, vo, *, tq, blocks):
    b, t, w = q.shape
    n_chunks = t // tq
    assert w == SB_WIDTH and t % (tq * blocks) == 0 and kt.shape == (b, n_chunks, w, tq)
    kv_spec = pl.BlockSpec((1, n_chunks, w, tq), lambda bi, i: (bi, 0, 0, 0))
    q_spec = pl.BlockSpec((1, tq * blocks, w), lambda bi, i: (bi, i, 0))
    return pl.pallas_call(
        functools.partial(_sb_prompt_kernel, tq=tq, blocks=blocks),
        grid=(b, n_chunks // blocks),
        in_specs=[q_spec, kv_spec, kv_spec, kv_spec, pl.BlockSpec((tq, tq), lambda bi, i: (0, 0))],
        out_specs=q_spec,
        out_shape=jax.ShapeDtypeStruct((b, t, w), F32),
        scratch_shapes=[pltpu.VMEM((SB_PAIRS, tq, V7X_LANES), F32), pltpu.VMEM((SB_HEADS, tq, 1), F32)],
        compiler_params=pltpu.CompilerParams(
            dimension_semantics=("arbitrary", "arbitrary"), vmem_limit_bytes=V7X_VMEM_LIMIT),
        name="sb_prompt",
    )(q, kt, ve, vo, _suffix_ones(tq, 1))


def _sb_cached_kernel(q_ref, k_ref, v_ref, ckt_ref, cvt_ref, ud_ref, uc_ref, o_ref, acc_scr, car_scr,
                      *, tq, kn, n_chunks):
    first = lax.broadcasted_iota(jnp.int32, (tq, V7X_LANES), 1) < SB_HEAD_DIM
    r = lax.broadcasted_iota(jnp.int32, (tq, tq), 0)
    c = lax.broadcasted_iota(jnp.int32, (tq, tq), 1)
    causal = c < r
    even = (lax.broadcasted_iota(jnp.int32, (V7X_LANES, kn), 0) // SB_HEAD_DIM) % 2 == 0
    qs = []
    for p in range(SB_PAIRS):
        qs.extend(_head_split(q_ref[0, :, V7X_LANES * p:V7X_LANES * (p + 1)], first))

    def pair_rows(h):
        return slice(V7X_LANES * (h // 2), V7X_LANES * (h // 2 + 1))

    def sweep(n):
        z, wts, tot, pv = {}, {}, {}, {}

        def scores(h):
            if n is None:
                z[h] = _dot_nt(qs[h], k_ref[0, :, pair_rows(h)].astype(BF16))
            else:
                z[h] = _dot(qs[h], ckt_ref[0, pair_rows(h), n * kn:(n + 1) * kn].astype(BF16))

        def weights(h):
            if n is None:
                wts[h], tot[h] = _sb_weights(z.pop(h), ud_ref[...], jnp.zeros((tq, 1), F32), causal)
            else:
                wts[h], tot[h] = _sb_weights(z.pop(h), uc_ref[...], car_scr[h], None)

        def weighted_values(h):
            if n is None:
                v = _head_split(v_ref[0, :, pair_rows(h)], first)[h % 2]
                pv[h] = _dot(wts.pop(h), v.astype(BF16))
            else:
                v = cvt_ref[0, pair_rows(h), n * kn:(n + 1) * kn]
                v = jnp.where(even, 0.0, v) if h % 2 else jnp.where(even, v, 0.0)
                pv[h] = _dot_nt(wts.pop(h), v.astype(BF16))

        stages = (scores, weights, weighted_values)
        for step in range(SB_HEADS + len(stages) - 1):
            for k, stage in enumerate(stages):
                if 0 <= step - k < SB_HEADS:
                    stage(step - k)
        low = None
        for p in range(SB_PAIRS):
            new = pv[2 * p] + pv[2 * p + 1]
            acc_scr[p] = new if n is None else acc_scr[p] + new
        for h in range(SB_HEADS):
            car_scr[h] = tot[h]
            low = tot[h] if low is None else jnp.minimum(low, tot[h])
        return jnp.min(low)

    lowest = sweep(None)
    for n in range(n_chunks - 1, -1, -1):
        lowest = lax.cond(lowest <= SB_LOG2_UNDERFLOW, functools.partial(sweep, n), lambda low=lowest: low)
    for p in range(SB_PAIRS):
        o_ref[0, :, V7X_LANES * p:V7X_LANES * (p + 1)] = acc_scr[p]


def _sb_cached(q, k, v, cache_kt, cache_vt, *, kn):
    b, tq, w = q.shape
    past = cache_kt.shape[2]
    assert w == SB_WIDTH and past % kn == 0 and cache_kt.shape == (b, w, past)
    new_spec = pl.BlockSpec((1, tq, w), lambda bi: (bi, 0, 0))
    cache_spec = pl.BlockSpec((1, w, past), lambda bi: (bi, 0, 0))
    return pl.pallas_call(
        functools.partial(_sb_cached_kernel, tq=tq, kn=kn, n_chunks=past // kn),
        grid=(b,),
        in_specs=[new_spec, new_spec, new_spec, cache_spec, cache_spec,
                  pl.BlockSpec((2 * tq, tq), lambda bi: (0, 0)),
                  pl.BlockSpec((2 * kn, kn), lambda bi: (0, 0))],
        out_specs=new_spec,
        out_shape=jax.ShapeDtypeStruct((b, tq, w), F32),
        scratch_shapes=[pltpu.VMEM((SB_PAIRS, tq, V7X_LANES), F32), pltpu.VMEM((SB_HEADS, tq, 1), F32)],
        compiler_params=pltpu.CompilerParams(
            dimension_semantics=("arbitrary",), vmem_limit_bytes=V7X_VMEM_LIMIT),
        name="sb_cached",
    )(q, k, v, cache_kt, cache_vt, _suffix_ones(tq, 2), _suffix_ones(kn, 2))


def _hgrn_consts(c):
    t = np.arange(c)
    sub = t // HG_BLOCK
    same = sub[:, None] == sub[None, :]
    tri = (same & (t[None, :] <= t[:, None])).astype(np.float32)
    ones = same.astype(np.float32)
    x = sub[:, None] ^ sub[None, :]
    level = np.where(x == 0, 0, np.floor(np.log2(np.maximum(x, 1))).astype(np.int64) + 1)
    level = np.where(t[None, :] <= t[:, None], level, -1).astype(np.int32)
    pre = np.block([[tri, tri], [ones, ones]])
    return jnp.asarray(pre, dtype=BF16), jnp.asarray(level)


def _prod(vecs):
    out = None
    for v in vecs:
        out = v if out is None else out * v
    return out


def _scale_rows(x, vecs):
    blocks = [x[HG_BLOCK * i:HG_BLOCK * (i + 1)] for i in range(len(vecs))]
    return jnp.concatenate([b if v is None else b * v for b, v in zip(blocks, vecs)], axis=0)


def _hgrn_group(read, write, lb, g, pre, lvl, states, *, c, group, v_t):
    nsub = c // HG_BLOCK
    levels = int(round(math.log2(nsub)))
    states = list(states)
    level_masks = [lvl == l for l in range(levels + 1)]
    ch = [dict() for _ in range(group)]
    item = {}

    def head(h):
        return slice(HG_KEY_DIM * h, HG_KEY_DIM * (h + 1))

    def gates(gi):
        hq, hf, v = read(gi)
        f = lb + (1.0 - lb) * jax.nn.sigmoid(hf)
        ch[gi].update(hq=hq, kb=1.0 - f,
                      v_bf=v.astype(BF16), split=jnp.concatenate(_split_bf16(jnp.log(f)), axis=0))

    def prefix(gi):
        ch[gi]["sums"] = _dot(pre, ch[gi].pop("split"))

    def decays(gi):
        sums, hq, kb = (ch[gi].pop(k) for k in ("sums", "hq", "kb"))
        b16 = sums[:c]
        bend = sums[c:]
        q_dec = hq * jnp.exp(b16)
        dmat = jnp.exp(bend)
        k_inv = kb * jnp.exp(-b16)
        k_end = k_inv * dmat
        d = [dmat[HG_BLOCK * i:HG_BLOCK * i + 1, :] for i in range(nsub)]
        q_lv = [q_dec.astype(BF16)]
        k_lv = [k_inv.astype(BF16)]
        if levels >= 1:
            q_lv.append(q_lv[0])
            k_lv.append(k_end.astype(BF16))
        for l in range(2, levels + 1):
            half = 2 ** (l - 1)
            alpha, beta = [], []
            for i in range(nsub):
                mid = (i // (2 * half)) * 2 * half + half
                alpha.append(_prod(d[mid:i]) if i > mid else None)
                beta.append(_prod(d[i + 1:mid]) if i + 1 < mid else None)
            q_lv.append(_scale_rows(q_dec, alpha).astype(BF16))
            k_lv.append(_scale_rows(k_end, beta).astype(BF16))
        if nsub > 1:
            alpha_c = [_prod(d[:i]) if i > 0 else None for i in range(nsub)]
            beta_c = [_prod(d[i + 1:]) if i + 1 < nsub else None for i in range(nsub)]
            q_c = _scale_rows(q_dec, alpha_c).astype(BF16)
            k_c = _scale_rows(k_end, beta_c).astype(BF16)
        else:
            q_c = q_lv[0]
            k_c = k_end.astype(BF16)
        ch[gi].update(q_lv=q_lv, k_lv=k_lv, q_c=q_c, k_c=k_c, d_all=_prod(d))

    def level_scores(gi, h):
        q_lv, k_lv = ch[gi]["q_lv"], ch[gi]["k_lv"]
        item[gi, h] = [_dot_nt(q_lv[l][:, head(h)], k_lv[l][:, head(h)]) for l in range(levels + 1)]

    def combine(gi, h):
        att = jnp.zeros((c, c), F32)
        for l, scores in enumerate(item[gi, h]):
            att = jnp.where(level_masks[l], scores, att)
        item[gi, h] = att.astype(BF16)

    def outputs(gi, h):
        att = item[gi, h]
        v_bf, q_c, k_c = ch[gi]["v_bf"], ch[gi]["q_c"], ch[gi]["k_c"]
        if v_t:
            vt_h = v_bf[head(h), :]
            o_intra = _dot_nt(att, vt_h)
            st_add = _dot(vt_h, k_c[:, head(h)])
        else:
            o_intra = _dot(att, v_bf[:, head(h)])
            st_add = _dot_tn(v_bf[:, head(h)], k_c[:, head(h)])
        item[gi, h] = (o_intra + _dot_nt(q_c[:, head(h)], states[h].astype(BF16)), st_add)

    def finish(gi, h):
        o, st_add = item.pop((gi, h))
        ms = jnp.mean(o * o, axis=-1, keepdims=True)
        write(gi, h, o * lax.rsqrt(ms + NORM_EPS) * g[:, head(h)])
        states[h] = states[h] * ch[gi]["d_all"][:, head(h)] + st_add

    chunk_stages = (gates, prefix, decays)
    item_stages = (level_scores, combine, outputs, finish)
    n_items = group * HG_HEADS
    for step in range(-len(chunk_stages), n_items + len(item_stages) - 1):
        for gi in range(group):
            for k, stage in enumerate(chunk_stages):
                if step == HG_HEADS * gi - len(chunk_stages) + k:
                    stage(gi)
        for k, stage in enumerate(item_stages):
            if 0 <= step - k < n_items:
                stage(*divmod(step - k, HG_HEADS))
    return states


def _hgrn_kernel(*refs, c, group, layer, has_s0, v_t, n_steps):
    if has_s0:
        (hq_ref, hf_ref, hi_ref, lbraw_ref, g_ref, pre_ref, lvl_ref, s0_ref,
         o_ref, sout_ref, st_scr) = refs
    else:
        (hq_ref, hf_ref, hi_ref, lbraw_ref, g_ref, pre_ref, lvl_ref,
         o_ref, sout_ref, st_scr) = refs
        s0_ref = None
    ci = pl.program_id(1)

    @pl.when(ci == 0)
    def _():
        for h in range(HG_HEADS):
            if has_s0:
                st_scr[h] = s0_ref[0, h].T
            else:
                st_scr[h] = jnp.zeros((HG_VAL_DIM, HG_KEY_DIM), F32)

    raw = lbraw_ref[...]
    e = jnp.exp(raw - jnp.max(raw, axis=0, keepdims=True))
    lb = jnp.sum(e[:layer + 1], axis=0, keepdims=True) / jnp.sum(e, axis=0, keepdims=True)

    pre = pre_ref[...]
    lvl = lvl_ref[...]
    g = g_ref[...]
    def read(gi):
        rs = slice(gi * c, (gi + 1) * c)
        return hq_ref[0, rs, :], hf_ref[0, rs, :], hi_ref[0, :, rs] if v_t else hi_ref[0, rs, :]

    def write(gi, h, o):
        o_ref[0, gi * c:(gi + 1) * c, HG_VAL_DIM * h:HG_VAL_DIM * (h + 1)] = o

    states = _hgrn_group(read, write, lb, g, pre, lvl, [st_scr[h] for h in range(HG_HEADS)],
                         c=c, group=group, v_t=v_t)
    for h in range(HG_HEADS):
        st_scr[h] = states[h]

    @pl.when(ci == n_steps - 1)
    def _():
        for h in range(HG_HEADS):
            sout_ref[0, h] = st_scr[h].T


def _hgrn(hq, hf, hi, lb_raw, norm_g, s0, *, layer, c, group, v_t):
    b, t, w = hq.shape
    span = c * group
    assert w == HG_WIDTH and t % span == 0 and c % HG_BLOCK == 0
    n_steps = t // span
    pre, level = _hgrn_consts(c)
    tok = pl.BlockSpec((1, span, w), lambda bi, ci: (bi, ci, 0))
    tok_t = pl.BlockSpec((1, w, span), lambda bi, ci: (bi, 0, ci))
    const2 = lambda shape: pl.BlockSpec(shape, lambda bi, ci: (0, 0))
    st_spec = pl.BlockSpec((1, HG_HEADS, HG_KEY_DIM, HG_VAL_DIM), lambda bi, ci: (bi, 0, 0, 0))
    in_specs = [tok, tok, tok_t if v_t else tok, const2(lb_raw.shape), const2((1, w)),
                const2(pre.shape), const2(level.shape)]
    args = [hq, hf, hi, lb_raw, norm_g.reshape(1, w), pre, level]
    if s0 is not None:
        in_specs.append(st_spec)
        args.append(s0)
    return pl.pallas_call(
        functools.partial(_hgrn_kernel, c=c, group=group, layer=layer, has_s0=s0 is not None, v_t=v_t,
                          n_steps=n_steps),
        grid=(b, n_steps),
        in_specs=in_specs,
        out_specs=(tok, st_spec),
        out_shape=(jax.ShapeDtypeStruct((b, t, w), F32),
                   jax.ShapeDtypeStruct((b, HG_HEADS, HG_KEY_DIM, HG_VAL_DIM), F32)),
        scratch_shapes=[pltpu.VMEM((HG_HEADS, HG_VAL_DIM, HG_KEY_DIM), F32)],
        compiler_params=pltpu.CompilerParams(
            dimension_semantics=("arbitrary", "arbitrary"), vmem_limit_bytes=V7X_VMEM_LIMIT),
        name="hgrn2",
    )(*args)


def _out_kernel(x_ref, osb_ref, ohg_ref, gpre_ref, wg_ref, bg_ref, wa_ref, wb_ref, wo_ref, gp_ref, y_ref,
                *, tm, tiles):
    d_model = x_ref.shape[-1]
    off = np.cumsum([0, SB_WIDTH, HG_WIDTH, d_model, d_model])
    bg = bg_ref[...]
    for t in range(tiles):
        rows = slice(t * tm, (t + 1) * tm)
        x = x_ref[rows, :]
        xn = _rms_norm_bf16(x, gpre_ref[...])
        z_a, hz, ga, gb = (_dot(xn, wg_ref[:, off[i]:off[i + 1]]) for i in range(4))
        u_a = _dot((osb_ref[rows, :] * jax.nn.silu(z_a)).astype(BF16), wa_ref[...])
        u_b = _dot((ohg_ref[rows, :] * jax.nn.silu(hz)).astype(BF16), wb_ref[...])
        merged = (jax.nn.sigmoid(ga + bg[:, :d_model]) * u_a + jax.nn.sigmoid(gb + bg[:, d_model:]) * u_b)
        m = _dot(merged.astype(BF16), wo_ref[...])
        ms = jnp.mean(m * m, axis=-1, keepdims=True)
        y_ref[rows, :] = x + m * lax.rsqrt(ms + NORM_EPS) * gp_ref[...]


def _out_stage(x2d, o_sb, o_hg, g_pre, w_in_bf16, b_gate, wa, wb, wo, g_post, *, tm, tiles):
    n, d = x2d.shape
    span = tm * tiles
    assert n % span == 0
    _, offsets = _in_widths(d)
    w_gate = jnp.concatenate([w_in_bf16[:, offsets[i]:offsets[i + 1]] for i in IN_GATE_GROUPS], axis=1)
    row = lambda w: pl.BlockSpec((span, w), lambda i: (i, 0))
    const = lambda shape: pl.BlockSpec(shape, lambda i: (0, 0))
    return pl.pallas_call(
        functools.partial(_out_kernel, tm=tm, tiles=tiles),
        grid=(n // span,),
        in_specs=[row(d), row(SB_WIDTH), row(HG_WIDTH), const((1, d)), const(w_gate.shape),
                  const((1, 2 * d)), const(wa.shape), const(wb.shape), const(wo.shape), const((1, d))],
        out_specs=row(d),
        out_shape=jax.ShapeDtypeStruct((n, d), F32),
        compiler_params=pltpu.CompilerParams(
            dimension_semantics=("arbitrary",), vmem_limit_bytes=V7X_VMEM_LIMIT),
        name="out_stage",
    )(x2d, o_sb, o_hg, g_pre.reshape(1, d), w_gate, b_gate.reshape(1, 2 * d), wa, wb, wo, g_post.reshape(1, d))


PROJ_TILE, IN_PROJ_TILES, OUT_TILES = 256, 4, 4
SB_BLOCK, SB_BLOCKS = 256, 8
HG_CHUNK, HG_CHUNKS = 128, 16


def _largest_tile(n, cap):
    t = cap
    while n % t:
        t //= 2
    return t


def _trunk_layer(x, cache_k, cache_v, s0, layer, g_pre, w_in, b_gate, lb_raw, hg_norm_g,
                 w_a, w_b, w_out, g_post):
    b, t, d = x.shape
    n = b * t
    heads = (SB_HEADS, SB_HEAD_DIM)
    if cache_k is None:
        tm = _largest_tile(t, PROJ_TILE)
        q, kt, vt, hq, hf, hi, kt16, ve16, vo16 = _in_proj(
            x, g_pre, w_in, tm=tm, tiles=_largest_tile(t // tm, IN_PROJ_TILES), t_groups=(IN_K, IN_V, IN_HG_V))
        o_sb = _sb_prompt(q, kt16, ve16, vo16, tq=tm, blocks=_largest_tile(t // tm, SB_BLOCKS))
        k_out, v_out = (a.reshape(b, *heads, t).transpose(0, 3, 1, 2) for a in (kt, vt))
    else:
        tm = _largest_tile(n, PROJ_TILE)
        outs = _in_proj(x.reshape(1, n, d), g_pre, w_in, tm=tm, tiles=_largest_tile(n // tm, IN_PROJ_TILES),
                        t_groups=())
        q, k, v, hq, hf, hi = (a.reshape(b, t, a.shape[-1]) for a in outs)
        past = cache_k.shape[1]
        cache_kt, cache_vt = (a.transpose(0, 2, 3, 1).reshape(b, SB_WIDTH, past) for a in (cache_k, cache_v))
        o_sb = _sb_cached(q, k, v, cache_kt, cache_vt, kn=_largest_tile(past, SB_BLOCK))
        k_out, v_out = k.reshape(b, t, *heads), v.reshape(b, t, *heads)
    c = _largest_tile(t, HG_CHUNK)
    o_hg, s_new = _hgrn(hq, hf, hi, lb_raw, hg_norm_g.reshape(-1), s0, layer=layer, c=c,
                        group=_largest_tile(t // c, HG_CHUNKS), v_t=cache_k is None)
    flat = lambda a: a.reshape(n, a.shape[-1])
    tm = _largest_tile(n, PROJ_TILE)
    y = _out_stage(flat(x), flat(o_sb), flat(o_hg), g_pre, w_in, b_gate, w_a, w_b, w_out, g_post,
                   tm=tm, tiles=_largest_tile(n // tm, OUT_TILES))
    return y.reshape(b, t, d), k_out, v_out, s_new


def kernel(x_prompt, x_sample, cache_sb_k, cache_sb_v, state_hgrn, g_pre, w_in, b_gate, hg_lb_raw,
           hg_norm_g, w_branch_a, w_branch_b, w_out, g_post):
    depth = w_in.shape[0]
    y_p, y_s = x_prompt, x_sample
    outs = [[] for _ in range(6)]
    for l in range(depth):
        weights = (g_pre[l], w_in[l].astype(BF16), b_gate[l], hg_lb_raw, hg_norm_g[l],
                   w_branch_a[l].astype(BF16), w_branch_b[l].astype(BF16), w_out[l].astype(BF16), g_post[l])
        y_p, kp, vp, sp = _trunk_layer(y_p, None, None, None, l, *weights)
        y_s, ks, vs, ss = _trunk_layer(y_s, cache_sb_k[l], cache_sb_v[l], state_hgrn[l], l, *weights)
        for lst, a in zip(outs, (kp, vp, sp, ks, vs, ss)):
            lst.append(a)
    return (y_p, y_s) + tuple(jnp.stack(lst) for lst in outs)
```
